```python
import math
import jax
import jax.numpy as jnp
from jax import lax
import numpy as np

D_MODEL = 1024
BATCH = 8
SEQ = 4096
DEPTH = 2

CTX_LEN = 256
GRID_W = 64
N_EVEN = (DEPTH + 1) // 2
N_ODD = DEPTH // 2
RMS_EPS = 1e-6
NEG_INF = -1e30
MOD_INIT = 0.5

SSD_HEADS = 16
SSD_HEAD_DIM = 64
SSD_INNER = SSD_HEADS * SSD_HEAD_DIM
SSD_GROUPS = 2
SSD_STATE = 128
SSD_CONV_W = 3
SSD_CHUNK = 128
DT_MIN = 0.001
DT_MAX = 0.1
SSD_GN = SSD_GROUPS * SSD_STATE

NA_HEADS = 16
NA_HEAD_DIM = 64
NA_INNER = NA_HEADS * NA_HEAD_DIM
NA_WIN_R = 8
NA_WIN_C = 16
NA_QBLOCK = 16
NA_KCOLS = 2 * NA_WIN_C

SC_INNER = D_MODEL
SC_CONV_W = 3

D_FF = 4 * D_MODEL

OFF_X = 0
OFF_B = OFF_X + SSD_INNER
OFF_DT = OFF_B + SSD_GN
OFF_K = OFF_DT + 2 * SSD_HEADS
OFF_V = OFF_K + NA_INNER
OFF_C = OFF_V + NA_INNER
OFF_Z = OFF_C + SSD_GN
OFF_Q = OFF_Z + SSD_INNER
IN0_COLS = OFF_Q + NA_INNER
CONV0_CH = SSD_INNER + 2 * SSD_GN

kernel_name = "hybrid_ssd_natten_shortconv_dit"


def rmsnorm(x, w):
    xf = x.astype(jnp.float32)
    xf = xf * lax.rsqrt(jnp.mean(xf * xf, axis=-1, keepdims=True) + RMS_EPS)
    return (xf * w.astype(jnp.float32)).astype(x.dtype)


def modulate(h, shift, scale):
    return h * (1 + scale) + shift


def dwconv_centred(u, w, b=None):
    width, ch = w.shape
    y = lax.conv_general_dilated(
        u, w[:, None, :].astype(u.dtype), window_strides=(1,),
        padding=[(width // 2, width // 2)],
        dimension_numbers=('NWC', 'WIO', 'NWC'), feature_group_count=ch)
    return y if b is None else y + b


def ssd_chunk_inputs(x, dt, A, bm):
    bsz, length, heads, hd = x.shape
    nc, rep = length // SSD_CHUNK, heads // SSD_GROUPS
    xdt = (x * dt[..., None]).reshape(bsz, nc, SSD_CHUNK, SSD_GROUPS, rep, hd)
    a_cum = jnp.cumsum((dt * A).reshape(bsz, nc, SSD_CHUNK, SSD_GROUPS, rep), axis=2)
    bc = bm.reshape(bsz, nc, SSD_CHUNK, SSD_GROUPS, SSD_STATE)
    return xdt, a_cum, bc


def ssd_carry(xdt, a_cum, bc, h0):
    to_end = jnp.exp(a_cum[:, :, -1:] - a_cum)
    chunk_states = jnp.einsum('bcqgn,bcqgrp->bcgrpn', bc, to_end[..., None] * xdt)
    chunk_decay = jnp.exp(a_cum[:, :, -1])

    def step(h, inp):
        s, dcy = inp
        return dcy[..., None, None] * h + s, h

    h_final, h_start = lax.scan(step, h0, (jnp.moveaxis(chunk_states, 1, 0), jnp.moveaxis(chunk_decay, 1, 0)))
    return jnp.moveaxis(h_start, 0, 1), h_final


def ssd_readout(xdt, a_cum, bc, cc, h_start):
    q = a_cum.shape[2]
    lower = jnp.tril(jnp.ones((q, q), dtype=bool))
    a_t = jnp.moveaxis(a_cum, 2, -1)
    seg = a_t[..., :, None] - a_t[..., None, :]
    decay = jnp.exp(jnp.where(lower, seg, -jnp.inf))
    cb = jnp.einsum('bcign,bcjgn->bcgij', cc, bc)
    y_diag = jnp.einsum('bcgrij,bcjgrp->bcigrp', cb[:, :, :, None] * decay, xdt)
    y_off = jnp.einsum('bcign,bcgrpn->bcigrp', cc, h_start) * jnp.exp(a_cum)[..., None]
    return y_diag + y_off


def bidirectional_ssd(x_l, b_l, c_l, dt_l, x_c, b_c, dt_c, dt_bias, a_log, d_skip):
    out_dtype = x_l.dtype
    f32 = jnp.float32
    x_l, b_l, c_l, x_c, b_c = (t.astype(f32) for t in (x_l, b_l, c_l, x_c, b_c))
    bsz, seq, heads, hd = x_l.shape
    rep = heads // SSD_GROUPS
    y = d_skip.astype(f32)[:, None] * x_l
    for d in range(2):
        rev = (lambda t: jnp.flip(t, axis=1)) if d == 1 else (lambda t: t)
        A = -jnp.exp(a_log[d].astype(f32))
        dtc = jax.nn.softplus(dt_c[:, :, d].astype(f32) + dt_bias[d].astype(f32))
        dtl = jax.nn.softplus(dt_l[:, :, d].astype(f32) + dt_bias[d].astype(f32))
        h0 = jnp.zeros((bsz, SSD_GROUPS, rep, hd, SSD_STATE), f32)
        _, h_ctx = ssd_carry(*ssd_chunk_inputs(rev(x_c), rev(dtc), A, rev(b_c)), h0)
        xdt, a_cum, bc = ssd_chunk_inputs(rev(x_l), rev(dtl), A, rev(b_l))
        h_start, _ = ssd_carry(xdt, a_cum, bc, h_ctx)
        cc = rev(c_l).reshape(bc.shape)
        y = y + rev(ssd_readout(xdt, a_cum, bc, cc, h_start).reshape(bsz, seq, heads, hd))
    return y.reshape(bsz, seq, heads * hd).astype(out_dtype)


def neighbourhood_attention(q, k, v, k_ctx, v_ctx, rpb):
    bsz, seq, heads, hd = q.shape
    rows = seq // GRID_W
    wr = min(NA_WIN_R, rows)
    n_blk = GRID_W // NA_QBLOCK
    qg = (q * hd ** -0.5).reshape(bsz, rows, GRID_W, heads, hd)
    kg = k.reshape(bsz, rows, GRID_W, heads, hd)
    vg = v.reshape(bsz, rows, GRID_W, heads, hd)

    q_col = jnp.arange(GRID_W).reshape(n_blk, NA_QBLOCK)
    kc0 = jnp.clip(jnp.arange(n_blk) * NA_QBLOCK - NA_WIN_C // 2, 0, GRID_W - NA_KCOLS)
    k_col = kc0[:, None] + jnp.arange(NA_KCOLS)
    win_start = jnp.clip(q_col - NA_WIN_C // 2, 0, GRID_W - NA_WIN_C)
    kcb = k_col[:, None, :]
    col_ok = (kcb >= win_start[..., None]) & (kcb < win_start[..., None] + NA_WIN_C)
    dc_idx = jnp.clip(kcb - q_col[..., None], -(NA_WIN_C - 1), NA_WIN_C - 1) + NA_WIN_C - 1
    rpb_c = rpb[:, :, dc_idx]
    n_win = wr * NA_KCOLS

    def row_fn(r):
        rs = jnp.clip(r - NA_WIN_R // 2, 0, rows - wr)
        k_win = lax.dynamic_slice_in_dim(kg, rs, wr, axis=1)[:, :, k_col]
        v_win = lax.dynamic_slice_in_dim(vg, rs, wr, axis=1)[:, :, k_col]
        q_r = lax.dynamic_index_in_dim(qg, r, axis=1, keepdims=False).reshape(bsz, n_blk, NA_QBLOCK, heads, hd)
        s_win = jnp.einsum('bjqhd,bwjkhd->bhjqwk', q_r, k_win).astype(jnp.float32)
        dr_idx = rs + jnp.arange(wr) - r + NA_WIN_R - 1
        bias = jnp.take(rpb_c, dr_idx, axis=1).transpose(0, 2, 3, 1, 4).astype(jnp.float32)
        s_win = jnp.where(col_ok[:, :, None, :], s_win + bias, NEG_INF)
        s_ctx = jnp.einsum('bjqhd,bchd->bhjqc', q_r, k_ctx).astype(jnp.float32)
        s = jnp.concatenate([s_win.reshape(bsz, heads, n_blk, NA_QBLOCK, n_win), s_ctx], axis=-1)
        p = jax.nn.softmax(s, axis=-1).astype(v.dtype)
        p_win = p[..., :n_win].reshape(bsz, heads, n_blk, NA_QBLOCK, wr, NA_KCOLS)
        o = (jnp.einsum('bhjqwk,bwjkhd->bjqhd', p_win, v_win)
             + jnp.einsum('bhjqc,bchd->bjqhd', p[..., n_win:], v_ctx))
        return o.reshape(bsz, GRID_W, heads * hd)

    out = lax.map(row_fn, jnp.arange(rows))
    return jnp.moveaxis(out, 0, 1).reshape(bsz, seq, heads * hd)


def ssd_na_mixer(h_lat, h_ctx, in_w, conv_w, conv_b, dt_bias, a_log, d_skip, norm_w, rpb, out_w):
    bsz, seq, _ = h_lat.shape
    n_ctx = h_ctx.shape[1]
    pc = h_ctx @ in_w[:, :OFF_C]
    xb_c = jax.nn.silu(dwconv_centred(pc[..., :OFF_DT], conv_w[:, :OFF_DT], conv_b[:OFF_DT]))
    x_c = xb_c[..., :SSD_INNER].reshape(bsz, n_ctx, SSD_HEADS, SSD_HEAD_DIM)
    b_c = xb_c[..., SSD_INNER:].reshape(bsz, n_ctx, SSD_GROUPS, SSD_STATE)
    dt_c = pc[..., OFF_DT:OFF_K].reshape(bsz, n_ctx, 2, SSD_HEADS)
    k_c = pc[..., OFF_K:OFF_V].reshape(bsz, n_ctx, NA_HEADS, NA_HEAD_DIM)
    v_c = pc[..., OFF_V:OFF_C].reshape(bsz, n_ctx, NA_HEADS, NA_HEAD_DIM)
    pl = h_lat @ in_w
    xbc = jax.nn.silu(dwconv_centred(
        jnp.concatenate([pl[..., :OFF_DT], pl[..., OFF_C:OFF_Z]], axis=-1), conv_w, conv_b))
    x_l = xbc[..., :SSD_INNER].reshape(bsz, seq, SSD_HEADS, SSD_HEAD_DIM)
    b_l = xbc[..., SSD_INNER:OFF_DT].reshape(bsz, seq, SSD_GROUPS, SSD_STATE)
    c_l = xbc[..., OFF_DT:].reshape(bsz, seq, SSD_GROUPS, SSD_STATE)
    dt_l = pl[..., OFF_DT:OFF_K].reshape(bsz, seq, 2, SSD_HEADS)
    k_l = pl[..., OFF_K:OFF_V].reshape(bsz, seq, NA_HEADS, NA_HEAD_DIM)
    v_l = pl[..., OFF_V:OFF_C].reshape(bsz, seq, NA_HEADS, NA_HEAD_DIM)
    z = pl[..., OFF_Z:OFF_Q]
    q_l = pl[..., OFF_Q:].reshape(bsz, seq, NA_HEADS, NA_HEAD_DIM)

    y_ssd = bidirectional_ssd(x_l, b_l, c_l, dt_l, x_c, b_c, dt_c, dt_bias, a_log, d_skip)
    y_ssd = rmsnorm(y_ssd * jax.nn.silu(z), norm_w)
    y_na = neighbourhood_attention(q_l, k_l, v_l, k_c, v_c, rpb)
    return jnp.concatenate([y_ssd.astype(y_na.dtype), y_na], axis=-1) @ out_w


def short_conv_mixer(h, in_w, conv_w, out_w):
    gate_b, gate_c, val = jnp.split(h @ in_w, 3, axis=-1)
    return (gate_b * dwconv_centred(gate_c * val, conv_w)) @ out_w


def sq_relu_mlp(h, w1, w2):
    return jnp.square(jax.nn.relu(h @ w1)) @ w2


def setup_inputs(seed: int = 0) -> dict:
    key = jax.random.key(seed)
    ks = jax.random.split(key, 23)
    D = D_MODEL
    nrm = jax.random.normal
    f32 = jnp.float32
    x = nrm(ks[0], (BATCH, SEQ, D), f32)
    c = nrm(ks[1], (BATCH, D), f32)
    ctx = nrm(ks[2], (BATCH, CTX_LEN, D), f32)
    c_ctx = nrm(ks[3], (D,), f32)
    mod_w = nrm(ks[4], (DEPTH, D, 6 * D), f32) * (MOD_INIT * D ** -0.5)
    mod_b = 0.02 * nrm(ks[5], (DEPTH, 6 * D), f32)
    norm_mix_w = 1.0 + 0.02 * nrm(ks[6], (DEPTH, D), f32)
    norm_mlp_w = 1.0 + 0.02 * nrm(ks[7], (DEPTH, D), f32)
    mlp_w1 = nrm(ks[8], (DEPTH, D, D_FF), f32) * D ** -0.5
    mlp_w2 = nrm(ks[9], (DEPTH, D_FF, D), f32) * D_FF ** -0.5
    ssdna_in_w = nrm(ks[10], (N_EVEN, D, IN0_COLS), f32) * D ** -0.5
    ssdna_conv_w = nrm(ks[11], (N_EVEN, SSD_CONV_W, CONV0_CH), f32) * SSD_CONV_W ** -0.5
    ssdna_conv_b = 0.02 * nrm(ks[12], (N_EVEN, CONV0_CH), f32)
    dt0 = jnp.exp(jax.random.uniform(ks[13], (N_EVEN, 2, SSD_HEADS), f32,
                                     minval=math.log(DT_MIN), maxval=math.log(DT_MAX)))
    ssd_dt_bias = dt0 + jnp.log(-jnp.expm1(-dt0))
    ssd_a_log = jnp.log(jax.random.uniform(ks[14], (N_EVEN, 2, SSD_HEADS), f32, minval=1.0, maxval=16.0))
    ssd_d = 1.0 + 0.02 * nrm(ks[15], (N_EVEN, SSD_HEADS), f32)
    ssd_norm_w = 1.0 + 0.02 * nrm(ks[16], (N_EVEN, SSD_INNER), f32)
    na_rpb = 0.02 * nrm(ks[17], (N_EVEN, NA_HEADS, 2 * NA_WIN_R - 1, 2 * NA_WIN_C - 1), f32)
    ssdna_out_w = nrm(ks[18], (N_EVEN, SSD_INNER + NA_INNER, D), f32) * (SSD_INNER + NA_INNER) ** -0.5
    sc_in_w = nrm(ks[19], (N_ODD, D, 3 * SC_INNER), f32) * D ** -0.5
    sc_conv_w = nrm(ks[20], (N_ODD, SC_CONV_W, SC_INNER), f32) * SC_CONV_W ** -0.5
    sc_out_w = nrm(ks[21], (N_ODD, SC_INNER, D), f32) * SC_INNER ** -0.5
    final_norm_w = 1.0 + 0.02 * nrm(ks[22], (D,), f32)
    return {"x": x, "c": c, "ctx": ctx, "c_ctx": c_ctx,
            "mod_w": mod_w, "mod_b": mod_b, "norm_mix_w": norm_mix_w, "norm_mlp_w": norm_mlp_w,
            "mlp_w1": mlp_w1, "mlp_w2": mlp_w2,
            "ssdna_in_w": ssdna_in_w, "ssdna_conv_w": ssdna_conv_w, "ssdna_conv_b": ssdna_conv_b,
            "ssd_dt_bias": ssd_dt_bias, "ssd_a_log": ssd_a_log, "ssd_d": ssd_d, "ssd_norm_w": ssd_norm_w,
            "na_rpb": na_rpb, "ssdna_out_w": ssdna_out_w,
            "sc_in_w": sc_in_w, "sc_conv_w": sc_conv_w, "sc_out_w": sc_out_w,
            "final_norm_w": final_norm_w}


def reference(x, c, ctx, c_ctx, mod_w, mod_b, norm_mix_w, norm_mlp_w, mlp_w1, mlp_w2,
              ssdna_in_w, ssdna_conv_w, ssdna_conv_b, ssd_dt_bias, ssd_a_log, ssd_d, ssd_norm_w,
              na_rpb, ssdna_out_w, sc_in_w, sc_conv_w, sc_out_w, final_norm_w):
    D = D_MODEL
    for i in range(DEPTH):
        mod = jax.nn.silu(c) @ mod_w[i] + mod_b[i]
        shift_a, scale_a, gate_a, shift_f, scale_f, gate_f = jnp.split(mod[:, None, :], 6, axis=-1)
        h = modulate(rmsnorm(x, norm_mix_w[i]), shift_a, scale_a)
        if i % 2 == 0:
            e = i // 2
            mod_ctx = jax.nn.silu(c_ctx) @ mod_w[i][:, :2 * D] + mod_b[i][:2 * D]
            h_ctx = modulate(rmsnorm(ctx, norm_mix_w[i]), mod_ctx[:D], mod_ctx[D:])
            y = ssd_na_mixer(h, h_ctx, ssdna_in_w[e], ssdna_conv_w[e], ssdna_conv_b[e], ssd_dt_bias[e],
                             ssd_a_log[e], ssd_d[e], ssd_norm_w[e], na_rpb[e], ssdna_out_w[e])
        else:
            o = i // 2
            y = short_conv_mixer(h, sc_in_w[o], sc_conv_w[o], sc_out_w[o])
        x = x + gate_a * y
        h = modulate(rmsnorm(x, norm_mlp_w[i]), shift_f, scale_f)
        x = x + gate_f * sq_relu_mlp(h, mlp_w1[i], mlp_w2[i])
    return rmsnorm(x, final_norm_w)
```

```python
import functools

import numpy as np
import jax
import jax.numpy as jnp
from jax import lax
from jax.experimental import pallas as pl
from jax.experimental.pallas import tpu as pltpu

F32 = jnp.float32
BF16 = jnp.bfloat16

RMS_EPS = 1e-6
MASK_VALUE = -1e30

GRID_W = 64
SSD_HEADS = 16
SSD_HEAD_DIM = 64
SSD_GROUPS = 2
SSD_STATE = 128
SSD_CHUNK = 128
NA_HEADS = 16
NA_HEAD_DIM = 64
NA_WIN_R = 8
NA_WIN_C = 16

LANES = 128
SUBLANES = 8
VMEM_LIMIT = 56 * 1024 * 1024


def _cparams(semantics):
    return pltpu.CompilerParams(dimension_semantics=semantics, vmem_limit_bytes=VMEM_LIMIT)


def _resident(block_shape, index_map):
    return pl.BlockSpec(block_shape, index_map, pipeline_mode=pl.Buffered(1))


def _rms_mod(x, nw, shift, scale):
    ms = jnp.mean(x * x, axis=-1, keepdims=True)
    return (x * lax.rsqrt(ms + RMS_EPS)) * nw * (1.0 + scale) + shift


def _silu(x):
    return x * jax.nn.sigmoid(x)


def _softplus(x):
    return jnp.maximum(x, 0.0) + jnp.log1p(jnp.exp(-jnp.abs(x)))


def _dot(a, b):
    return jnp.dot(a, b, preferred_element_type=F32)


def _dot_nt(a, b):
    return lax.dot_general(a, b, (((1,), (1,)), ((), ())), preferred_element_type=F32)


def _split_bf16(a, parts):
    out = []
    r = a
    for _ in range(parts):
        h = r.astype(BF16)
        out.append(h)
        r = r - h.astype(F32)
    return out


def _modvec_kernel(c_ref, w_ref, b_ref, o_ref):
    s = _silu(c_ref[...]).astype(BF16)
    o_ref[...] = _dot(s, w_ref[...].astype(BF16)) + b_ref[...]


def modvec(cc, mod_w, mod_b, tn=1536):
    depth, d, n = mod_w.shape
    rows = cc.shape[0]
    return pl.pallas_call(
        _modvec_kernel,
        grid=(depth, n // tn),
        in_specs=[
            pl.BlockSpec((rows, d), lambda i, j: (0, 0)),
            pl.BlockSpec((None, d, tn), lambda i, j: (i, 0, j)),
            pl.BlockSpec((None, 1, tn), lambda i, j: (i, 0, j)),
        ],
        out_specs=pl.BlockSpec((None, rows, tn), lambda i, j: (i, 0, j)),
        out_shape=jax.ShapeDtypeStruct((depth, rows, n), F32),
        compiler_params=_cparams(("arbitrary", "arbitrary")),
        name="modvec",
    )(cc, mod_w, mod_b.reshape(depth, 1, n))


XBC_COLS = SSD_HEADS * SSD_HEAD_DIM + 2 * SSD_GROUPS * SSD_STATE
DT_COLS = 2 * LANES


def _inproj0_kernel(x_ref, nw_ref, sh_ref, sc_ref, w_ref,
                    xbc_ref, z_ref, q_ref, k_ref, v_ref, dt_ref):
    d = x_ref.shape[-1]
    h = _rms_mod(x_ref[...], nw_ref[...], sh_ref[...], sc_ref[...]).astype(BF16)
    c0 = 0
    for ref, width in ((xbc_ref, XBC_COLS), (z_ref, d), (q_ref, d), (k_ref, d), (v_ref, d),
                       (dt_ref, DT_COLS)):
        ref[...] = _dot(h, w_ref[:, c0:c0 + width]).astype(ref.dtype)
        c0 += width


def inproj0(x, nw, shift, scale, w, tm):
    bsz, seq, d = x.shape
    ncols = w.shape[1]
    tok = lambda width: pl.BlockSpec((None, tm, width), lambda b, i: (b, i, 0))
    vec = pl.BlockSpec((None, 1, d), lambda b, i: (b, 0, 0))
    out_shapes = (
        jax.ShapeDtypeStruct((bsz, seq, XBC_COLS), F32),
        jax.ShapeDtypeStruct((bsz, seq, d), F32),
        jax.ShapeDtypeStruct((bsz, seq, d), BF16),
        jax.ShapeDtypeStruct((bsz, seq, d), BF16),
        jax.ShapeDtypeStruct((bsz, seq, d), BF16),
        jax.ShapeDtypeStruct((bsz, seq, DT_COLS), F32),
    )
    return pl.pallas_call(
        _inproj0_kernel,
        grid=(bsz, seq // tm),
        in_specs=[tok(d), pl.BlockSpec((1, d), lambda b, i: (0, 0)), vec, vec,
                  _resident((d, ncols), lambda b, i: (0, 0))],
        out_specs=(tok(XBC_COLS), tok(d), tok(d), tok(d), tok(d), tok(DT_COLS)),
        out_shape=out_shapes,
        compiler_params=_cparams(("arbitrary", "arbitrary")),
        name="inproj0",
    )(x, nw, shift, scale, w)


def _ssd_kernel(backward, final, n_ctx_chunks, n_lat_chunks, *refs):
    if final:
        (xbc_ref, hp_ref, hn_ref, dt_ref, xbcc_ref, dtc_ref, cw_ref, cb_ref, dtb_ref, alog_ref,
         dsk_ref, e_ref, z_ref, yprev_ref, gnw_ref, y_ref, state_ref) = refs
    else:
        (xbc_ref, hp_ref, hn_ref, dt_ref, xbcc_ref, dtc_ref, cw_ref, cb_ref, dtb_ref, alog_ref,
         dsk_ref, e_ref, y_ref, state_ref) = refs
    q = SSD_CHUNK
    inner = SSD_HEADS * SSD_HEAD_DIM
    gn = SSD_STATE
    gcols = inner // SSD_GROUPS
    c = pl.program_id(1)

    ri = lax.broadcasted_iota(jnp.int32, (q, q), 0)
    ci = lax.broadcasted_iota(jnp.int32, (q, q), 1)
    causal = (ri <= ci) if backward else (ri >= ci)
    tri = jnp.where(causal, 1.0, 0.0).astype(BF16)
    lane = lax.broadcasted_iota(jnp.int32, (q, LANES), 1)
    low_half = lane < SSD_HEAD_DIM
    rows_full = lax.broadcasted_iota(jnp.int32, (q, XBC_COLS), 0)

    def process(u, prev_row, next_row, dtraw, want_y):
        up = jnp.where(rows_full == 0, prev_row, pltpu.roll(u, 1, 0))
        un = jnp.where(rows_full == q - 1, next_row, pltpu.roll(u, q - 1, 0))
        xc = _silu(up * cw_ref[0:1, :] + u * cw_ref[1:2, :] + un * cw_ref[2:3, :] + cb_ref[...])
        x = xc[:, :inner]
        bm = xc[:, inner:inner + SSD_GROUPS * gn]
        cm = xc[:, inner + SSD_GROUPS * gn:]

        dt = _softplus(dtraw + dtb_ref[...])
        a = dt * (-jnp.exp(alog_ref[...]))
        hi, mid, lo = _split_bf16(a, 3)
        a_cum = _dot(tri, hi) + _dot(tri, mid) + _dot(tri, lo)
        edge = a_cum[0:1, :] if backward else a_cum[q - 1:q, :]
        ea = jnp.exp(a_cum)
        dtw = dt * jnp.exp(edge - a_cum)
        e = e_ref[...]
        p0, p1, p2 = _split_bf16(ea, 3)
        ea_full = _dot(p0, e) + _dot(p1, e) + _dot(p2, e)
        p0, p1 = _split_bf16(dtw, 2)
        dtw_full = _dot(p0, e) + _dot(p1, e)
        chunk_decay = ea_full[0:1, :] if backward else ea_full[q - 1:q, :]

        xw = (x * dtw_full).astype(BF16)
        x16 = x.astype(BF16)
        bm16 = bm.astype(BF16)
        cm16 = cm.astype(BF16)

        if want_y:
            a_cum_t = a_cum.T
            dt_t = dt.T
            ys = []
            for g in range(SSD_GROUPS):
                cb = _dot_nt(cm16[:, g * gn:(g + 1) * gn], bm16[:, g * gn:(g + 1) * gn])
                heads_per_group = SSD_HEADS // SSD_GROUPS
                for pair in range(heads_per_group // 2):
                    ms = []
                    for hh in range(2):
                        h = g * heads_per_group + pair * 2 + hh
                        seg = a_cum[:, h:h + 1] - a_cum_t[h:h + 1, :]
                        decay = jnp.exp(jnp.where(causal, seg, -jnp.inf))
                        ms.append((cb * decay * dt_t[h:h + 1, :]).astype(BF16))
                    m_pair = jnp.concatenate(ms, axis=1)
                    col0 = (g * heads_per_group + pair * 2) * SSD_HEAD_DIM
                    xp = x16[:, col0:col0 + LANES]
                    zero = jnp.zeros_like(xp)
                    x_bd = jnp.concatenate([jnp.where(low_half, xp, zero),
                                            jnp.where(low_half, zero, xp)], axis=0)
                    ys.append(_dot(m_pair, x_bd))
            y_diag = jnp.concatenate(ys, axis=1)
            y_off = jnp.concatenate(
                [_dot(cm16[:, g * gn:(g + 1) * gn], state_ref[:, g * gcols:(g + 1) * gcols].astype(BF16))
                 for g in range(SSD_GROUPS)], axis=1) * ea_full
            y = y_diag + y_off
            if final:
                y = y + yprev_ref[...]
                y = y * _silu(z_ref[...])
                ms_ = jnp.mean(y * y, axis=-1, keepdims=True)
                y = (y * lax.rsqrt(ms_ + RMS_EPS)) * gnw_ref[...]
            else:
                y = y + x * dsk_ref[...]
            y_ref[...] = y.astype(y_ref.dtype)

        for g in range(SSD_GROUPS):
            bt = bm[:, g * gn:(g + 1) * gn].T.astype(BF16)
            upd = _dot(bt, xw[:, g * gcols:(g + 1) * gcols])
            sl = slice(g * gcols, (g + 1) * gcols)
            state_ref[:, sl] = state_ref[:, sl] * chunk_decay[:, sl] + upd

    @pl.when(c == 0)
    def _():
        state_ref[...] = jnp.zeros_like(state_ref)

    zero_row = jnp.zeros((1, XBC_COLS), F32)
    for step in range(n_ctx_chunks):
        cc = (n_ctx_chunks - 1 - step) if backward else step

        @pl.when(c == step)
        def _(cc=cc):
            r0 = cc * q
            prev_row = xbcc_ref[r0 - 1:r0, :] if cc > 0 else zero_row
            next_row = xbcc_ref[r0 + q:r0 + q + 1, :] if cc < n_ctx_chunks - 1 else zero_row
            process(xbcc_ref[r0:r0 + q, :], prev_row, next_row, dtc_ref[r0:r0 + q, :], False)

    @pl.when(c >= n_ctx_chunks)
    def _():
        lc = c - n_ctx_chunks
        if backward:
            lc = n_lat_chunks - 1 - lc
        prev_row = jnp.where(lc > 0, hp_ref[SUBLANES - 1:SUBLANES, :], 0.0)
        next_row = jnp.where(lc < n_lat_chunks - 1, hn_ref[0:1, :], 0.0)
        process(xbc_ref[...], prev_row, next_row, dt_ref[...], True)


def ssd_pass(backward, xbc, dt, xbc_c, dt_c, conv_w, conv_b, dt_bias_row, a_log_row, d_skip_row, expand,
             z=None, y_prev=None, gn_w=None):
    final = z is not None
    bsz, seq, _ = xbc.shape
    n_ctx = xbc_c.shape[1]
    q = SSD_CHUNK
    inner = SSD_HEADS * SSD_HEAD_DIM
    ncc, nlc = n_ctx // q, seq // q
    halo_per_chunk = q // SUBLANES
    n_halo = seq // SUBLANES

    def lat(c):
        lc = jnp.maximum(c - ncc, 0)
        return (nlc - 1 - lc) if backward else lc

    chunk = lambda width: pl.BlockSpec((None, q, width), lambda b, c: (b, lat(c), 0))
    const = lambda shape: pl.BlockSpec(shape, lambda b, c: (0,) * len(shape))
    in_specs = [
        chunk(XBC_COLS),
        pl.BlockSpec((None, SUBLANES, XBC_COLS),
                     lambda b, c: (b, jnp.maximum(lat(c) * halo_per_chunk - 1, 0), 0)),
        pl.BlockSpec((None, SUBLANES, XBC_COLS),
                     lambda b, c: (b, jnp.minimum((lat(c) + 1) * halo_per_chunk, n_halo - 1), 0)),
        pl.BlockSpec((None, q, LANES), lambda b, c: (b, lat(c), 1 if backward else 0)),
        pl.BlockSpec((None, n_ctx, XBC_COLS), lambda b, c: (b, 0, 0)),
        pl.BlockSpec((None, n_ctx, LANES), lambda b, c: (b, 0, 1 if backward else 0)),
        const((3, XBC_COLS)), const((1, XBC_COLS)), const((1, LANES)), const((1, LANES)),
        const((1, inner)), const((LANES, inner)),
    ]
    args = [xbc, xbc, xbc, dt, xbc_c, dt_c, conv_w, conv_b, dt_bias_row, a_log_row, d_skip_row, expand]
    if final:
        in_specs += [chunk(inner), chunk(inner), const((1, inner))]
        args += [z, y_prev, gn_w]
    return pl.pallas_call(
        functools.partial(_ssd_kernel, backward, final, ncc, nlc),
        grid=(bsz, ncc + nlc),
        in_specs=in_specs,
        out_specs=chunk(inner),
        out_shape=jax.ShapeDtypeStruct((bsz, seq, inner), BF16 if final else F32),
        scratch_shapes=[pltpu.VMEM((SSD_STATE, inner), F32)],
        compiler_params=_cparams(("arbitrary", "arbitrary")),
        name="ssd_bwd" if backward else "ssd_fwd",
    )(*args)


def _natten_kernel(grid_rows, q_ref, k_ref, v_ref, kc_ref, vc_ref, bias_ref, o_ref):
    w = GRID_W
    r = pl.program_id(1)
    rs = jnp.clip(r - NA_WIN_R // 2, 0, grid_rows - NA_WIN_R)
    k0 = pl.multiple_of(rs * w, w)
    nwin = NA_WIN_R * w
    dr0 = rs - r + NA_WIN_R - 1
    lane = lax.broadcasted_iota(jnp.int32, (w, LANES), 1)
    low_half = lane < NA_HEAD_DIM
    for pair in range(NA_HEADS // 2):
        cols = slice(pair * LANES, (pair + 1) * LANES)
        qp = q_ref[:, cols]
        zero = jnp.zeros_like(qp)
        wq = jnp.concatenate([jnp.where(low_half, qp, zero), jnp.where(low_half, zero, qp)], axis=0)
        kw = k_ref[pl.ds(k0, nwin), cols]
        vw = v_ref[pl.ds(k0, nwin), cols]
        s_win = _dot_nt(wq, kw)
        s_win = jnp.concatenate(
            [s_win[:, t * LANES:(t + 1) * LANES] + bias_ref[pair, dr0 + 2 * t]
             for t in range(nwin // LANES)], axis=1)
        s_ctx = _dot_nt(wq, kc_ref[:, cols])
        m = jnp.maximum(jnp.max(s_win, axis=-1, keepdims=True), jnp.max(s_ctx, axis=-1, keepdims=True))
        p_win = jnp.exp(s_win - m)
        p_ctx = jnp.exp(s_ctx - m)
        denom = jnp.sum(p_win, axis=-1, keepdims=True) + jnp.sum(p_ctx, axis=-1, keepdims=True)
        o = _dot(p_win.astype(BF16), vw) + _dot(p_ctx.astype(BF16), vc_ref[:, cols])
        o = o / denom
        o_ref[:, cols] = jnp.where(low_half, o[:w, :], o[w:, :]).astype(o_ref.dtype)


def natten(q, k, v, kc, vc, bias2):
    bsz, seq, d = q.shape
    n_ctx = kc.shape[1]
    grid_rows = seq // GRID_W
    return pl.pallas_call(
        functools.partial(_natten_kernel, grid_rows),
        grid=(bsz, grid_rows),
        in_specs=[
            pl.BlockSpec((None, GRID_W, d), lambda b, r: (b, r, 0)),
            _resident((None, seq, d), lambda b, r: (b, 0, 0)),
            _resident((None, seq, d), lambda b, r: (b, 0, 0)),
            _resident((None, n_ctx, d), lambda b, r: (b, 0, 0)),
            _resident((None, n_ctx, d), lambda b, r: (b, 0, 0)),
            _resident(bias2.shape, lambda b, r: (0, 0, 0, 0)),
        ],
        out_specs=pl.BlockSpec((None, GRID_W, d), lambda b, r: (b, r, 0)),
        out_shape=jax.ShapeDtypeStruct((bsz, seq, d), BF16),
        compiler_params=_cparams(("arbitrary", "arbitrary")),
        name="natten",
    )(q, k, v, kc, vc, bias2)


def natten_bias_table(rpb):
    w = GRID_W
    qc = np.arange(w)[:, None]
    kc = np.arange(w)[None, :]
    win_start = np.clip(qc - NA_WIN_C // 2, 0, w - NA_WIN_C)
    col_ok = (kc >= win_start) & (kc < win_start + NA_WIN_C)
    dc_idx = np.clip(kc - qc, -(NA_WIN_C - 1), NA_WIN_C - 1) + NA_WIN_C - 1
    t = jnp.where(col_ok[None, None], rpb[:, :, dc_idx], MASK_VALUE)
    ndr = 2 * NA_WIN_R - 2
    t2 = jnp.concatenate([t[:, :ndr], t[:, 1:ndr + 1]], axis=-1)
    t2 = t2.reshape(NA_HEADS // 2, 2, ndr, w, 2 * w).transpose(0, 2, 1, 3, 4)
    return t2.reshape(NA_HEADS // 2, ndr, 2 * w, 2 * w).astype(F32)


def _outproj_kernel(x_ref, ya_ref, yb_ref, w_ref, gate_ref, o_ref):
    ka = ya_ref.shape[-1]
    y = _dot(ya_ref[...], w_ref[:ka, :]) + _dot(yb_ref[...], w_ref[ka:, :])
    o_ref[...] = x_ref[...] + gate_ref[...] * y


def outproj(x, ya, yb, w, gate, tm):
    bsz, seq, d = x.shape
    tok = lambda width: pl.BlockSpec((None, tm, width), lambda b, i: (b, i, 0))
    return pl.pallas_call(
        _outproj_kernel,
        grid=(bsz, seq // tm),
        in_specs=[tok(d), tok(ya.shape[-1]), tok(yb.shape[-1]),
                  _resident(w.shape, lambda b, i: (0, 0)),
                  pl.BlockSpec((None, 1, d), lambda b, i: (b, 0, 0))],
        out_specs=tok(d),
        out_shape=jax.ShapeDtypeStruct((bsz, seq, d), F32),
        compiler_params=_cparams(("arbitrary", "arbitrary")),
        name="outproj",
    )(x, ya, yb, w, gate)


def _mlp_kernel(ff_chunk, final_norm, *refs):
    if final_norm:
        x_ref, nw_ref, sh_ref, sc_ref, gate_ref, w1_ref, w2_ref, fnw_ref, o_ref = refs
    else:
        x_ref, nw_ref, sh_ref, sc_ref, gate_ref, w1_ref, w2_ref, o_ref = refs
    x = x_ref[...]
    h = _rms_mod(x, nw_ref[...], sh_ref[...], sc_ref[...]).astype(BF16)
    dff = w1_ref.shape[1]
    acc = None
    for c0 in range(0, dff, ff_chunk):
        a = jnp.maximum(_dot(h, w1_ref[:, c0:c0 + ff_chunk]), 0.0)
        part = _dot((a * a).astype(BF16), w2_ref[c0:c0 + ff_chunk, :])
        acc = part if acc is None else acc + part
    y = x + gate_ref[...] * acc
    if final_norm:
        ms = jnp.mean(y * y, axis=-1, keepdims=True)
        y = (y * lax.rsqrt(ms + RMS_EPS)) * fnw_ref[...]
    o_ref[...] = y


def mlp(x, nw, shift, scale, gate, w1, w2, tm, ff_chunk, final_nw=None):
    bsz, seq, d = x.shape
    tok = pl.BlockSpec((None, tm, d), lambda b, i: (b, i, 0))
    vec = pl.BlockSpec((None, 1, d), lambda b, i: (b, 0, 0))
    row = pl.BlockSpec((1, d), lambda b, i: (0, 0))
    in_specs = [tok, row, vec, vec, vec,
                _resident(w1.shape, lambda b, i: (0, 0)), _resident(w2.shape, lambda b, i: (0, 0))]
    args = [x, nw, shift, scale, gate, w1, w2]
    if final_nw is not None:
        in_specs.append(row)
        args.append(final_nw)
    return pl.pallas_call(
        functools.partial(_mlp_kernel, ff_chunk, final_nw is not None),
        grid=(bsz, seq // tm),
        in_specs=in_specs,
        out_specs=tok,
        out_shape=jax.ShapeDtypeStruct((bsz, seq, d), F32),
        compiler_params=_cparams(("arbitrary", "arbitrary")),
        name="mlp_final" if final_nw is not None else "mlp",
    )(*args)


def _shortconv_kernel(n_tiles, x_ref, xp_ref, xn_ref, nw_ref, sh_ref, sc_ref, gate_ref,
                      win_ref, cw_ref, wout_ref, o_ref):
    i = pl.program_id(1)
    tm, d = x_ref.shape
    inner = wout_ref.shape[0]
    hb = SUBLANES
    x = x_ref[...]
    x_ext = jnp.concatenate([xp_ref[...], x, xn_ref[...]], axis=0)
    h = _rms_mod(x_ext, nw_ref[...], sh_ref[...], sc_ref[...]).astype(BF16)
    gate_c = _dot(h, win_ref[:, inner:2 * inner])
    val = _dot(h, win_ref[:, 2 * inner:])
    u = gate_c * val
    rows = lax.broadcasted_iota(jnp.int32, u.shape, 0)
    outside = ((rows < hb) & (i == 0)) | ((rows >= tm + hb) & (i == n_tiles - 1))
    u = jnp.where(outside, 0.0, u)
    ext = tm + 2 * hb
    conv = (pltpu.roll(u, 1, 0) * cw_ref[0:1, :] + u * cw_ref[1:2, :]
            + pltpu.roll(u, ext - 1, 0) * cw_ref[2:3, :])[hb:hb + tm, :]
    gate_b = _dot(h[hb:hb + tm, :], win_ref[:, :inner])
    y = _dot((gate_b * conv).astype(BF16), wout_ref[...])
    o_ref[...] = x + gate_ref[...] * y


def shortconv(x, nw, shift, scale, gate, w_in, conv_w, w_out, tm):
    bsz, seq, d = x.shape
    n_tiles = seq // tm
    per_tile = tm // SUBLANES
    n_halo = seq // SUBLANES
    tok = pl.BlockSpec((None, tm, d), lambda b, i: (b, i, 0))
    vec = pl.BlockSpec((None, 1, d), lambda b, i: (b, 0, 0))
    return pl.pallas_call(
        functools.partial(_shortconv_kernel, n_tiles),
        grid=(bsz, n_tiles),
        in_specs=[
            tok,
            pl.BlockSpec((None, SUBLANES, d), lambda b, i: (b, jnp.maximum(i * per_tile - 1, 0), 0)),
            pl.BlockSpec((None, SUBLANES, d), lambda b, i: (b, jnp.minimum((i + 1) * per_tile, n_halo - 1), 0)),
            pl.BlockSpec((1, d), lambda b, i: (0, 0)), vec, vec, vec,
            _resident(w_in.shape, lambda b, i: (0, 0)),
            pl.BlockSpec(conv_w.shape, lambda b, i: (0, 0)),
            _resident(w_out.shape, lambda b, i: (0, 0)),
        ],
        out_specs=tok,
        out_shape=jax.ShapeDtypeStruct((bsz, seq, d), F32),
        compiler_params=_cparams(("arbitrary", "arbitrary")),
        name="shortconv",
    )(x, x, x, nw, shift, scale, gate, w_in, conv_w, w_out)


def _pad_lanes(row):
    return jnp.zeros((1, LANES), F32).at[0, :row.shape[0]].set(row.astype(F32))


def kernel(x, c, ctx, c_ctx, mod_w, mod_b, norm_mix_w, norm_mlp_w, mlp_w1, mlp_w2, ssdna_in_w, ssdna_conv_w,
           ssdna_conv_b, ssd_dt_bias, ssd_a_log, ssd_d, ssd_norm_w, na_rpb, ssdna_out_w, sc_in_w, sc_conv_w,
           sc_out_w, final_norm_w):
    bsz, seq, d = x.shape
    n_ctx = ctx.shape[1]
    tm = min(512, seq)
    inner = SSD_HEADS * SSD_HEAD_DIM
    gn2 = SSD_GROUPS * SSD_STATE

    mrows = -(-(bsz + 1) // SUBLANES) * SUBLANES
    cc = jnp.zeros((mrows, d), F32).at[:bsz].set(c).at[bsz].set(c_ctx)
    mod = modvec(cc, mod_w, mod_b).reshape(mod_w.shape[0], mrows, 6, d)
    vecs = lambda i: [mod[i, :bsz, j].reshape(bsz, 1, d) for j in range(6)]
    row = lambda v: v.reshape(1, -1).astype(F32)

    shift_a, scale_a, gate_a, shift_f, scale_f, gate_f = vecs(0)
    shift_c = jnp.broadcast_to(mod[0, bsz, 0].reshape(1, 1, d), (bsz, 1, d))
    scale_c = jnp.broadcast_to(mod[0, bsz, 1].reshape(1, 1, d), (bsz, 1, d))

    w = ssdna_in_w[0]
    o_b, o_dt, o_k, o_v = inner, inner + gn2, inner + gn2 + 2 * SSD_HEADS, inner + gn2 + 2 * SSD_HEADS + d
    o_c = o_v + d
    o_z, o_q = o_c + gn2, o_c + gn2 + inner
    dt_pad = jnp.zeros((d, LANES - SSD_HEADS), F32)
    w0 = jnp.concatenate([
        w[:, :o_dt], w[:, o_c:o_z],
        w[:, o_z:o_q],
        w[:, o_q:] * (NA_HEAD_DIM ** -0.5),
        w[:, o_k:o_v], w[:, o_v:o_c],
        w[:, o_dt:o_dt + SSD_HEADS], dt_pad, w[:, o_dt + SSD_HEADS:o_k], dt_pad,
    ], axis=1).astype(BF16)
    nw0 = row(norm_mix_w[0])
    xbc, z, q_l, k_l, v_l, dt_l = inproj0(x, nw0, shift_a, scale_a, w0, tm)
    xbc_c, _, _, k_c, v_c, dt_c = inproj0(ctx, nw0, shift_c, scale_c, w0, n_ctx)

    conv_w = jnp.concatenate([ssdna_conv_w[0][:, :o_dt], ssdna_conv_w[0][:, o_dt:]], axis=1)
    conv_b = row(ssdna_conv_b[0])
    d_skip_row = row(jnp.repeat(ssd_d[0], SSD_HEAD_DIM))
    expand = (np.arange(LANES)[:, None] == (np.arange(inner)[None, :] // SSD_HEAD_DIM)).astype(np.float32)
    expand = jnp.asarray(expand, BF16)
    common = (xbc, dt_l, xbc_c, dt_c, conv_w, conv_b)
    y_fwd = ssd_pass(False, *common, _pad_lanes(ssd_dt_bias[0][0]), _pad_lanes(ssd_a_log[0][0]),
                     d_skip_row, expand)
    y_ssd = ssd_pass(True, *common, _pad_lanes(ssd_dt_bias[0][1]), _pad_lanes(ssd_a_log[0][1]),
                     d_skip_row, expand, z=z, y_prev=y_fwd, gn_w=row(ssd_norm_w[0]))

    y_na = natten(q_l, k_l, v_l, k_c, v_c, natten_bias_table(na_rpb[0]))
    x = outproj(x, y_ssd, y_na, ssdna_out_w[0].astype(BF16), gate_a, tm)
    x = mlp(x, row(norm_mlp_w[0]), shift_f, scale_f, gate_f, mlp_w1[0].astype(BF16), mlp_w2[0].astype(BF16),
            tm, 512)

    shift_a, scale_a, gate_a, shift_f, scale_f, gate_f = vecs(1)
    x = shortconv(x, row(norm_mix_w[1]), shift_a, scale_a, gate_a, sc_in_w[0].astype(BF16), sc_conv_w[0],
                  sc_out_w[0].astype(BF16), tm)
    x = mlp(x, row(norm_mlp_w[1]), shift_f, scale_f, gate_f, mlp_w1[1].astype(BF16), mlp_w2[1].astype(BF16),
            tm, 512, final_nw=row(final_norm_w))
    return x
```

```python
import functools

import numpy as np
import jax
import jax.numpy as jnp
from jax import lax
from jax.experimental import pallas as pl
from jax.experimental.pallas import tpu as pltpu

F32 = jnp.float32
BF16 = jnp.bfloat16

RMS_EPS = 1e-6
MASK_VALUE = -1e30

GRID_W = 64
SSD_HEADS = 16
SSD_HEAD_DIM = 64
SSD_GROUPS = 2
SSD_STATE = 128
SSD_CHUNK = 128
NA_HEADS = 16
NA_HEAD_DIM = 64
NA_WIN_R = 8
NA_WIN_C = 16

LANES = 128
SUBLANES = 8
VMEM_LIMIT = 56 * 1024 * 1024


def _cparams(semantics):
    return pltpu.CompilerParams(dimension_semantics=semantics, vmem_limit_bytes=VMEM_LIMIT)


def _resident(block_shape, index_map):
    return pl.BlockSpec(block_shape, index_map, pipeline_mode=pl.Buffered(1))


def _rms_mod(x, nw, shift, scale):
    ms = jnp.mean(x * x, axis=-1, keepdims=True)
    return (x * lax.rsqrt(ms + RMS_EPS)) * nw * (1.0 + scale) + shift


def _silu(x):
    return x * jax.nn.sigmoid(x)


def _softplus(x):
    return jnp.maximum(x, 0.0) + jnp.log1p(jnp.exp(-jnp.abs(x)))


def _dot(a, b):
    return jnp.dot(a, b, preferred_element_type=F32)


def _dot_nt(a, b):
    return lax.dot_general(a, b, (((1,), (1,)), ((), ())), preferred_element_type=F32)


def _split_bf16(a, parts):
    out = []
    r = a
    for _ in range(parts):
        h = r.astype(BF16)
        out.append(h)
        r = r - h.astype(F32)
    return out


def _modvec_kernel(c_ref, w_ref, b_ref, o_ref):
    s = _silu(c_ref[...]).astype(BF16)
    o_ref[...] = _dot(s, w_ref[...].astype(BF16)) + b_ref[...]


def modvec(cc, mod_w, mod_b, tn=1536):
    depth, d, n = mod_w.shape
    rows = cc.shape[0]
    return pl.pallas_call(
        _modvec_kernel,
        grid=(depth, n // tn),
        in_specs=[
            pl.BlockSpec((rows, d), lambda i, j: (0, 0)),
            pl.BlockSpec((None, d, tn), lambda i, j: (i, 0, j)),
            pl.BlockSpec((None, 1, tn), lambda i, j: (i, 0, j)),
        ],
        out_specs=pl.BlockSpec((None, rows, tn), lambda i, j: (i, 0, j)),
        out_shape=jax.ShapeDtypeStruct((depth, rows, n), F32),
        compiler_params=_cparams(("arbitrary", "arbitrary")),
        name="modvec",
    )(cc, mod_w, mod_b.reshape(depth, 1, n))


XBC_COLS = SSD_HEADS * SSD_HEAD_DIM + 2 * SSD_GROUPS * SSD_STATE
DT_COLS = 2 * LANES


def _inproj0_kernel(x_ref, nw_ref, sh_ref, sc_ref, w_ref,
                    xbc_ref, z_ref, q_ref, k_ref, v_ref, dt_ref):
    d = x_ref.shape[-1]
    h = _rms_mod(x_ref[...], nw_ref[...], sh_ref[...], sc_ref[...]).astype(BF16)
    c0 = 0
    for ref, width in ((xbc_ref, XBC_COLS), (z_ref, d), (q_ref, d), (k_ref, d), (v_ref, d),
                       (dt_ref, DT_COLS)):
        y = _dot(h, w_ref[:, c0:c0 + width]).astype(ref.dtype)
        if len(ref.shape) == 3:
            for p in range(ref.shape[0]):
                ref[p] = y[:, p * LANES:(p + 1) * LANES]
        else:
            ref[...] = y
        c0 += width


def inproj0(x, nw, shift, scale, w, tm):
    bsz, seq, d = x.shape
    ncols = w.shape[1]
    npair = d // LANES
    tok = lambda width: pl.BlockSpec((None, tm, width), lambda b, i: (b, i, 0))
    pair_major = pl.BlockSpec((None, npair, tm, LANES), lambda b, i: (b, 0, i, 0))
    vec = pl.BlockSpec((None, 1, d), lambda b, i: (b, 0, 0))
    out_shapes = (
        jax.ShapeDtypeStruct((bsz, seq, XBC_COLS), F32),
        jax.ShapeDtypeStruct((bsz, seq, d), F32),
        jax.ShapeDtypeStruct((bsz, npair, seq, LANES), BF16),
        jax.ShapeDtypeStruct((bsz, npair, seq, LANES), BF16),
        jax.ShapeDtypeStruct((bsz, seq, d), BF16),
        jax.ShapeDtypeStruct((bsz, seq, DT_COLS), F32),
    )
    return pl.pallas_call(
        _inproj0_kernel,
        grid=(bsz, seq // tm),
        in_specs=[tok(d), pl.BlockSpec((1, d), lambda b, i: (0, 0)), vec, vec,
                  _resident((d, ncols), lambda b, i: (0, 0))],
        out_specs=(tok(XBC_COLS), tok(d), pair_major, pair_major, tok(d), tok(DT_COLS)),
        out_shape=out_shapes,
        compiler_params=_cparams(("arbitrary", "arbitrary")),
        name="inproj0",
    )(x, nw, shift, scale, w)


def _ssd_kernel(backward, final, n_ctx_chunks, n_lat_chunks, *refs):
    if final:
        (xbc_ref, hp_ref, hn_ref, dt_ref, xbcc_ref, dtc_ref, cw_ref, cb_ref, dtb_ref, alog_ref,
         dsk_ref, e_ref, z_ref, yprev_ref, gnw_ref, y_ref, state_ref) = refs
    else:
        (xbc_ref, hp_ref, hn_ref, dt_ref, xbcc_ref, dtc_ref, cw_ref, cb_ref, dtb_ref, alog_ref,
         dsk_ref, e_ref, y_ref, state_ref) = refs
    q = SSD_CHUNK
    inner = SSD_HEADS * SSD_HEAD_DIM
    gn = SSD_STATE
    gcols = inner // SSD_GROUPS
    c = pl.program_id(1)

    ri = lax.broadcasted_iota(jnp.int32, (q, q), 0)
    ci = lax.broadcasted_iota(jnp.int32, (q, q), 1)
    causal = (ri <= ci) if backward else (ri >= ci)
    tri = jnp.where(causal, 1.0, 0.0).astype(BF16)
    lane = lax.broadcasted_iota(jnp.int32, (q, LANES), 1)
    low_half = lane < SSD_HEAD_DIM
    rows_full = lax.broadcasted_iota(jnp.int32, (q, XBC_COLS), 0)

    def process(u, prev_row, next_row, dtraw, want_y):
        up = jnp.where(rows_full == 0, prev_row, pltpu.roll(u, 1, 0))
        un = jnp.where(rows_full == q - 1, next_row, pltpu.roll(u, q - 1, 0))
        xc = _silu(up * cw_ref[0:1, :] + u * cw_ref[1:2, :] + un * cw_ref[2:3, :] + cb_ref[...])
        x = xc[:, :inner]
        bm = xc[:, inner:inner + SSD_GROUPS * gn]
        cm = xc[:, inner + SSD_GROUPS * gn:]

        dt = _softplus(dtraw + dtb_ref[...])
        a = dt * (-jnp.exp(alog_ref[...]))
        hi, mid, lo = _split_bf16(a, 3)
        a_cum = _dot(tri, hi) + _dot(tri, mid) + _dot(tri, lo)
        edge = a_cum[0:1, :] if backward else a_cum[q - 1:q, :]
        ea = jnp.exp(a_cum)
        dtw = dt * jnp.exp(edge - a_cum)
        e = e_ref[...]
        p0, p1, p2 = _split_bf16(ea, 3)
        ea_full = _dot(p0, e) + _dot(p1, e) + _dot(p2, e)
        p0, p1 = _split_bf16(dtw, 2)
        dtw_full = _dot(p0, e) + _dot(p1, e)
        chunk_decay = ea_full[0:1, :] if backward else ea_full[q - 1:q, :]

        xw = (x * dtw_full).astype(BF16)
        x16 = x.astype(BF16)
        bm16 = bm.astype(BF16)
        cm16 = cm.astype(BF16)

        if want_y:
            a_cum_t = a_cum.T
            dt_t = dt.T
            ys = []
            for g in range(SSD_GROUPS):
                cb = _dot_nt(cm16[:, g * gn:(g + 1) * gn], bm16[:, g * gn:(g + 1) * gn])
                heads_per_group = SSD_HEADS // SSD_GROUPS
                for pair in range(heads_per_group // 2):
                    ms = []
                    for hh in range(2):
                        h = g * heads_per_group + pair * 2 + hh
                        seg = a_cum[:, h:h + 1] - a_cum_t[h:h + 1, :]
                        decay = jnp.exp(jnp.where(causal, seg, -jnp.inf))
                        ms.append((cb * decay * dt_t[h:h + 1, :]).astype(BF16))
                    m_pair = jnp.concatenate(ms, axis=1)
                    col0 = (g * heads_per_group + pair * 2) * SSD_HEAD_DIM
                    xp = x16[:, col0:col0 + LANES]
                    zero = jnp.zeros_like(xp)
                    x_bd = jnp.concatenate([jnp.where(low_half, xp, zero),
                                            jnp.where(low_half, zero, xp)], axis=0)
                    ys.append(_dot(m_pair, x_bd))
            y_diag = jnp.concatenate(ys, axis=1)
            y_off = jnp.concatenate(
                [_dot(cm16[:, g * gn:(g + 1) * gn], state_ref[:, g * gcols:(g + 1) * gcols].astype(BF16))
                 for g in range(SSD_GROUPS)], axis=1) * ea_full
            y = y_diag + y_off
            if final:
                y = y + yprev_ref[...]
                y = y * _silu(z_ref[...])
                ms_ = jnp.mean(y * y, axis=-1, keepdims=True)
                y = (y * lax.rsqrt(ms_ + RMS_EPS)) * gnw_ref[...]
            else:
                y = y + x * dsk_ref[...]
            y_ref[...] = y.astype(y_ref.dtype)

        for g in range(SSD_GROUPS):
            bt = bm[:, g * gn:(g + 1) * gn].T.astype(BF16)
            upd = _dot(bt, xw[:, g * gcols:(g + 1) * gcols])
            sl = slice(g * gcols, (g + 1) * gcols)
            state_ref[:, sl] = state_ref[:, sl] * chunk_decay[:, sl] + upd

    @pl.when(c == 0)
    def _():
        state_ref[...] = jnp.zeros_like(state_ref)

    zero_row = jnp.zeros((1, XBC_COLS), F32)
    for step in range(n_ctx_chunks):
        cc = (n_ctx_chunks - 1 - step) if backward else step

        @pl.when(c == step)
        def _(cc=cc):
            r0 = cc * q
            prev_row = xbcc_ref[r0 - 1:r0, :] if cc > 0 else zero_row
            next_row = xbcc_ref[r0 + q:r0 + q + 1, :] if cc < n_ctx_chunks - 1 else zero_row
            process(xbcc_ref[r0:r0 + q, :], prev_row, next_row, dtc_ref[r0:r0 + q, :], False)

    @pl.when(c >= n_ctx_chunks)
    def _():
        lc = c - n_ctx_chunks
        if backward:
            lc = n_lat_chunks - 1 - lc
        prev_row = jnp.where(lc > 0, hp_ref[SUBLANES - 1:SUBLANES, :], 0.0)
        next_row = jnp.where(lc < n_lat_chunks - 1, hn_ref[0:1, :], 0.0)
        process(xbc_ref[...], prev_row, next_row, dt_ref[...], True)


def ssd_pass(backward, xbc, dt, xbc_c, dt_c, conv_w, conv_b, dt_bias_row, a_log_row, d_skip_row, expand,
             z=None, y_prev=None, gn_w=None):
    final = z is not None
    bsz, seq, _ = xbc.shape
    n_ctx = xbc_c.shape[1]
    q = SSD_CHUNK
    inner = SSD_HEADS * SSD_HEAD_DIM
    ncc, nlc = n_ctx // q, seq // q
    halo_per_chunk = q // SUBLANES
    n_halo = seq // SUBLANES

    def lat(c):
        lc = jnp.maximum(c - ncc, 0)
        return (nlc - 1 - lc) if backward else lc

    chunk = lambda width: pl.BlockSpec((None, q, width), lambda b, c: (b, lat(c), 0))
    const = lambda shape: pl.BlockSpec(shape, lambda b, c: (0,) * len(shape))
    in_specs = [
        chunk(XBC_COLS),
        pl.BlockSpec((None, SUBLANES, XBC_COLS),
                     lambda b, c: (b, jnp.maximum(lat(c) * halo_per_chunk - 1, 0), 0)),
        pl.BlockSpec((None, SUBLANES, XBC_COLS),
                     lambda b, c: (b, jnp.minimum((lat(c) + 1) * halo_per_chunk, n_halo - 1), 0)),
        pl.BlockSpec((None, q, LANES), lambda b, c: (b, lat(c), 1 if backward else 0)),
        pl.BlockSpec((None, n_ctx, XBC_COLS), lambda b, c: (b, 0, 0)),
        pl.BlockSpec((None, n_ctx, LANES), lambda b, c: (b, 0, 1 if backward else 0)),
        const((3, XBC_COLS)), const((1, XBC_COLS)), const((1, LANES)), const((1, LANES)),
        const((1, inner)), const((LANES, inner)),
    ]
    args = [xbc, xbc, xbc, dt, xbc_c, dt_c, conv_w, conv_b, dt_bias_row, a_log_row, d_skip_row, expand]
    if final:
        in_specs += [chunk(inner), chunk(inner), const((1, inner))]
        args += [z, y_prev, gn_w]
    return pl.pallas_call(
        functools.partial(_ssd_kernel, backward, final, ncc, nlc),
        grid=(bsz, ncc + nlc),
        in_specs=in_specs,
        out_specs=chunk(inner),
        out_shape=jax.ShapeDtypeStruct((bsz, seq, inner), BF16 if final else F32),
        scratch_shapes=[pltpu.VMEM((SSD_STATE, inner), F32)],
        compiler_params=_cparams(("arbitrary", "arbitrary")),
        name="ssd_bwd" if backward else "ssd_fwd",
    )(*args)


NA_GROUP_HEADS = 4
NA_ROWS_PER_STEP = 4


def _natten_kernel(grid_rows, q_ref, k_ref, v_ref, kc_ref, vc_ref, bias_ref, o_ref):
    w = GRID_W
    nwin = NA_WIN_R * w
    hg = NA_GROUP_HEADS
    gd = hg * NA_HEAD_DIM
    ngroups = NA_HEADS // hg
    lane_head = lax.broadcasted_iota(jnp.int32, (w, gd), 1) // NA_HEAD_DIM
    low_half = lax.broadcasted_iota(jnp.int32, (w, LANES), 1) < NA_HEAD_DIM

    def one_row(i, carry):
        r = pl.program_id(1) * NA_ROWS_PER_STEP + i
        rs = jnp.clip(r - NA_WIN_R // 2, 0, grid_rows - NA_WIN_R)
        k0 = pl.multiple_of(rs * w, w)
        q0 = pl.multiple_of(i * w, w)
        dr0 = rs - r + NA_WIN_R - 1

        def scores(g):
            s_win, s_ctx = [], []
            for pp in range(hg // 2):
                pair = g * (hg // 2) + pp
                qp = q_ref[pair, pl.ds(q0, w), :]
                zero = jnp.zeros_like(qp)
                wq = jnp.concatenate([jnp.where(low_half, qp, zero), jnp.where(low_half, zero, qp)], axis=0)
                s_win.append(_dot_nt(wq, k_ref[pair, pl.ds(k0, nwin), :]))
                s_ctx.append(_dot_nt(wq, kc_ref[pair]))
            return jnp.concatenate(s_win, axis=0), jnp.concatenate(s_ctx, axis=0)

        def softmax(g, s):
            s_win, s_ctx = s
            s_win = jnp.concatenate(
                [s_win[:, t * LANES:(t + 1) * LANES] + bias_ref[g, dr0 + 2 * t]
                 for t in range(nwin // LANES)], axis=1)
            m = jnp.maximum(jnp.max(s_win, axis=-1, keepdims=True), jnp.max(s_ctx, axis=-1, keepdims=True))
            p_win = jnp.exp(s_win - m)
            p_ctx = jnp.exp(s_ctx - m)
            denom = jnp.sum(p_win, axis=-1, keepdims=True) + jnp.sum(p_ctx, axis=-1, keepdims=True)
            return p_win.astype(BF16), p_ctx.astype(BF16), denom

        def values(g, p):
            cols = slice(g * gd, (g + 1) * gd)
            p_win, p_ctx, denom = p
            o = (_dot(p_win, v_ref[pl.ds(k0, nwin), cols]) + _dot(p_ctx, vc_ref[:, cols])) / denom
            acc = o[:w, :]
            for hh in range(1, hg):
                acc = jnp.where(lane_head == hh, o[hh * w:(hh + 1) * w, :], acc)
            o_ref[pl.ds(q0, w), cols] = acc.astype(o_ref.dtype)

        s, p = {}, {}
        for step in range(ngroups + 2):
            if step < ngroups:
                s[step] = scores(step)
            if 0 <= step - 1 < ngroups:
                p[step - 1] = softmax(step - 1, s.pop(step - 1))
            if 0 <= step - 2 < ngroups:
                values(step - 2, p.pop(step - 2))
        return carry

    lax.fori_loop(0, NA_ROWS_PER_STEP, one_row, 0)


def natten(q, k, v, kc, vc, bias2):
    bsz, seq, d = v.shape
    n_ctx = vc.shape[1]
    grid_rows = seq // GRID_W
    npair = NA_HEADS // 2
    rows_tok = NA_ROWS_PER_STEP * GRID_W
    return pl.pallas_call(
        functools.partial(_natten_kernel, grid_rows),
        grid=(bsz, grid_rows // NA_ROWS_PER_STEP),
        in_specs=[
            pl.BlockSpec((None, npair, rows_tok, LANES), lambda b, r: (b, 0, r, 0)),
            _resident((None, npair, seq, LANES), lambda b, r: (b, 0, 0, 0)),
            _resident((None, seq, d), lambda b, r: (b, 0, 0)),
            _resident((None, npair, n_ctx, LANES), lambda b, r: (b, 0, 0, 0)),
            _resident((None, n_ctx, d), lambda b, r: (b, 0, 0)),
            _resident(bias2.shape, lambda b, r: (0, 0, 0, 0)),
        ],
        out_specs=pl.BlockSpec((None, rows_tok, d), lambda b, r: (b, r, 0)),
        out_shape=jax.ShapeDtypeStruct((bsz, seq, d), BF16),
        compiler_params=_cparams(("arbitrary", "arbitrary")),
        name="natten",
    )(q, k, v, kc, vc, bias2)


def natten_bias_table(rpb):
    w = GRID_W
    hg = NA_GROUP_HEADS
    qc = np.arange(w)[:, None]
    kc = np.arange(w)[None, :]
    win_start = np.clip(qc - NA_WIN_C // 2, 0, w - NA_WIN_C)
    col_ok = (kc >= win_start) & (kc < win_start + NA_WIN_C)
    dc_idx = np.clip(kc - qc, -(NA_WIN_C - 1), NA_WIN_C - 1) + NA_WIN_C - 1
    ndc = 2 * NA_WIN_C - 1
    onehot = (np.arange(ndc)[:, None, None] == dc_idx[None]).astype(np.float32)
    t = jnp.einsum("hrc,cqk->hrqk", rpb.astype(F32), jnp.asarray(onehot), precision=lax.Precision.HIGHEST)
    t = jnp.where(col_ok[None, None], t, MASK_VALUE)
    ndr = 2 * NA_WIN_R - 2
    t2 = jnp.concatenate([t[:, :ndr], t[:, 1:ndr + 1]], axis=-1)
    t2 = t2.reshape(NA_HEADS // hg, hg, ndr, w, 2 * w).transpose(0, 2, 1, 3, 4)
    return t2.reshape(NA_HEADS // hg, ndr, hg * w, 2 * w)


def _outproj_kernel(x_ref, ya_ref, yb_ref, w_ref, gate_ref, o_ref):
    ka = ya_ref.shape[-1]
    y = _dot(ya_ref[...], w_ref[:ka, :]) + _dot(yb_ref[...], w_ref[ka:, :])
    o_ref[...] = x_ref[...] + gate_ref[...] * y


def outproj(x, ya, yb, w, gate, tm):
    bsz, seq, d = x.shape
    tok = lambda width: pl.BlockSpec((None, tm, width), lambda b, i: (b, i, 0))
    return pl.pallas_call(
        _outproj_kernel,
        grid=(bsz, seq // tm),
        in_specs=[tok(d), tok(ya.shape[-1]), tok(yb.shape[-1]),
                  _resident(w.shape, lambda b, i: (0, 0)),
                  pl.BlockSpec((None, 1, d), lambda b, i: (b, 0, 0))],
        out_specs=tok(d),
        out_shape=jax.ShapeDtypeStruct((bsz, seq, d), F32),
        compiler_params=_cparams(("arbitrary", "arbitrary")),
        name="outproj",
    )(x, ya, yb, w, gate)


def _mlp_kernel(ff_chunk, final_norm, *refs):
    if final_norm:
        x_ref, nw_ref, sh_ref, sc_ref, gate_ref, w1_ref, w2_ref, fnw_ref, o_ref = refs
    else:
        x_ref, nw_ref, sh_ref, sc_ref, gate_ref, w1_ref, w2_ref, o_ref = refs
    x = x_ref[...]
    h = _rms_mod(x, nw_ref[...], sh_ref[...], sc_ref[...]).astype(BF16)
    dff = w1_ref.shape[1]
    acc = None
    for c0 in range(0, dff, ff_chunk):
        a = jnp.maximum(_dot(h, w1_ref[:, c0:c0 + ff_chunk]), 0.0)
        part = _dot((a * a).astype(BF16), w2_ref[c0:c0 + ff_chunk, :])
        acc = part if acc is None else acc + part
    y = x + gate_ref[...] * acc
    if final_norm:
        ms = jnp.mean(y * y, axis=-1, keepdims=True)
        y = (y * lax.rsqrt(ms + RMS_EPS)) * fnw_ref[...]
    o_ref[...] = y


def mlp(x, nw, shift, scale, gate, w1, w2, tm, ff_chunk, final_nw=None):
    bsz, seq, d = x.shape
    tok = pl.BlockSpec((None, tm, d), lambda b, i: (b, i, 0))
    vec = pl.BlockSpec((None, 1, d), lambda b, i: (b, 0, 0))
    row = pl.BlockSpec((1, d), lambda b, i: (0, 0))
    in_specs = [tok, row, vec, vec, vec,
                _resident(w1.shape, lambda b, i: (0, 0)), _resident(w2.shape, lambda b, i: (0, 0))]
    args = [x, nw, shift, scale, gate, w1, w2]
    if final_nw is not None:
        in_specs.append(row)
        args.append(final_nw)
    return pl.pallas_call(
        functools.partial(_mlp_kernel, ff_chunk, final_nw is not None),
        grid=(bsz, seq // tm),
        in_specs=in_specs,
        out_specs=tok,
        out_shape=jax.ShapeDtypeStruct((bsz, seq, d), F32),
        compiler_params=_cparams(("arbitrary", "arbitrary")),
        name="mlp_final" if final_nw is not None else "mlp",
    )(*args)


def _shortconv_kernel(n_tiles, x_ref, xp_ref, xn_ref, nw_ref, sh_ref, sc_ref, gate_ref,
                      win_ref, cw_ref, wout_ref, o_ref):
    i = pl.program_id(1)
    tm, d = x_ref.shape
    inner = wout_ref.shape[0]
    hb = SUBLANES
    x = x_ref[...]
    x_ext = jnp.concatenate([xp_ref[...], x, xn_ref[...]], axis=0)
    h = _rms_mod(x_ext, nw_ref[...], sh_ref[...], sc_ref[...]).astype(BF16)
    gate_c = _dot(h, win_ref[:, inner:2 * inner])
    val = _dot(h, win_ref[:, 2 * inner:])
    u = gate_c * val
    rows = lax.broadcasted_iota(jnp.int32, u.shape, 0)
    outside = ((rows < hb) & (i == 0)) | ((rows >= tm + hb) & (i == n_tiles - 1))
    u = jnp.where(outside, 0.0, u)
    ext = tm + 2 * hb
    conv = (pltpu.roll(u, 1, 0) * cw_ref[0:1, :] + u * cw_ref[1:2, :]
            + pltpu.roll(u, ext - 1, 0) * cw_ref[2:3, :])[hb:hb + tm, :]
    gate_b = _dot(h[hb:hb + tm, :], win_ref[:, :inner])
    y = _dot((gate_b * conv).astype(BF16), wout_ref[...])
    o_ref[...] = x + gate_ref[...] * y


def shortconv(x, nw, shift, scale, gate, w_in, conv_w, w_out, tm):
    bsz, seq, d = x.shape
    n_tiles = seq // tm
    per_tile = tm // SUBLANES
    n_halo = seq // SUBLANES
    tok = pl.BlockSpec((None, tm, d), lambda b, i: (b, i, 0))
    vec = pl.BlockSpec((None, 1, d), lambda b, i: (b, 0, 0))
    return pl.pallas_call(
        functools.partial(_shortconv_kernel, n_tiles),
        grid=(bsz, n_tiles),
        in_specs=[
            tok,
            pl.BlockSpec((None, SUBLANES, d), lambda b, i: (b, jnp.maximum(i * per_tile - 1, 0), 0)),
            pl.BlockSpec((None, SUBLANES, d), lambda b, i: (b, jnp.minimum((i + 1) * per_tile, n_halo - 1), 0)),
            pl.BlockSpec((1, d), lambda b, i: (0, 0)), vec, vec, vec,
            _resident(w_in.shape, lambda b, i: (0, 0)),
            pl.BlockSpec(conv_w.shape, lambda b, i: (0, 0)),
            _resident(w_out.shape, lambda b, i: (0, 0)),
        ],
        out_specs=tok,
        out_shape=jax.ShapeDtypeStruct((bsz, seq, d), F32),
        compiler_params=_cparams(("arbitrary", "arbitrary")),
        name="shortconv",
    )(x, x, x, nw, shift, scale, gate, w_in, conv_w, w_out)


def _pad_lanes(row):
    return jnp.zeros((1, LANES), F32).at[0, :row.shape[0]].set(row.astype(F32))


def kernel(x, c, ctx, c_ctx, mod_w, mod_b, norm_mix_w, norm_mlp_w, mlp_w1, mlp_w2, ssdna_in_w, ssdna_conv_w,
           ssdna_conv_b, ssd_dt_bias, ssd_a_log, ssd_d, ssd_norm_w, na_rpb, ssdna_out_w, sc_in_w, sc_conv_w,
           sc_out_w, final_norm_w):
    bsz, seq, d = x.shape
    n_ctx = ctx.shape[1]
    tm = min(512, seq)
    inner = SSD_HEADS * SSD_HEAD_DIM
    gn2 = SSD_GROUPS * SSD_STATE

    mrows = -(-(bsz + 1) // SUBLANES) * SUBLANES
    cc = jnp.zeros((mrows, d), F32).at[:bsz].set(c).at[bsz].set(c_ctx)
    mod = modvec(cc, mod_w, mod_b).reshape(mod_w.shape[0], mrows, 6, d)
    vecs = lambda i: [mod[i, :bsz, j].reshape(bsz, 1, d) for j in range(6)]
    row = lambda v: v.reshape(1, -1).astype(F32)

    shift_a, scale_a, gate_a, shift_f, scale_f, gate_f = vecs(0)
    shift_c = jnp.broadcast_to(mod[0, bsz, 0].reshape(1, 1, d), (bsz, 1, d))
    scale_c = jnp.broadcast_to(mod[0, bsz, 1].reshape(1, 1, d), (bsz, 1, d))

    w = ssdna_in_w[0]
    o_b, o_dt, o_k, o_v = inner, inner + gn2, inner + gn2 + 2 * SSD_HEADS, inner + gn2 + 2 * SSD_HEADS + d
    o_c = o_v + d
    o_z, o_q = o_c + gn2, o_c + gn2 + inner
    dt_pad = jnp.zeros((d, LANES - SSD_HEADS), F32)
    w0 = jnp.concatenate([
        w[:, :o_dt], w[:, o_c:o_z],
        w[:, o_z:o_q],
        w[:, o_q:] * (NA_HEAD_DIM ** -0.5),
        w[:, o_k:o_v], w[:, o_v:o_c],
        w[:, o_dt:o_dt + SSD_HEADS], dt_pad, w[:, o_dt + SSD_HEADS:o_k], dt_pad,
    ], axis=1).astype(BF16)
    nw0 = row(norm_mix_w[0])
    xbc, z, q_l, k_l, v_l, dt_l = inproj0(x, nw0, shift_a, scale_a, w0, tm)
    xbc_c, _, _, k_c, v_c, dt_c = inproj0(ctx, nw0, shift_c, scale_c, w0, n_ctx)

    conv_w = ssdna_conv_w[0]
    conv_b = row(ssdna_conv_b[0])
    d_skip_row = row(jnp.repeat(ssd_d[0], SSD_HEAD_DIM))
    expand = (np.arange(LANES)[:, None] == (np.arange(inner)[None, :] // SSD_HEAD_DIM)).astype(np.float32)
    expand = jnp.asarray(expand, BF16)
    common = (xbc, dt_l, xbc_c, dt_c, conv_w, conv_b)
    y_fwd = ssd_pass(False, *common, _pad_lanes(ssd_dt_bias[0][0]), _pad_lanes(ssd_a_log[0][0]),
                     d_skip_row, expand)
    y_ssd = ssd_pass(True, *common, _pad_lanes(ssd_dt_bias[0][1]), _pad_lanes(ssd_a_log[0][1]),
                     d_skip_row, expand, z=z, y_prev=y_fwd, gn_w=row(ssd_norm_w[0]))

    y_na = natten(q_l, k_l, v_l, k_c, v_c, natten_bias_table(na_rpb[0]))
    x = outproj(x, y_ssd, y_na, ssdna_out_w[0].astype(BF16), gate_a, tm)
    x = mlp(x, row(norm_mlp_w[0]), shift_f, scale_f, gate_f, mlp_w1[0].astype(BF16), mlp_w2[0].astype(BF16),
            tm, 512)

    shift_a, scale_a, gate_a, shift_f, scale_f, gate_f = vecs(1)
    x = shortconv(x, row(norm_mix_w[1]), shift_a, scale_a, gate_a, sc_in_w[0].astype(BF16), sc_conv_w[0],
                  sc_out_w[0].astype(BF16), tm)
    x = mlp(x, row(norm_mlp_w[1]), shift_f, scale_f, gate_f, mlp_w1[1].astype(BF16), mlp_w2[1].astype(BF16),
            tm, 512, final_nw=row(final_norm_w))
    return x
```

```python
import functools

import numpy as np
import jax
import jax.numpy as jnp
from jax import lax
from jax.experimental import pallas as pl
from jax.experimental.pallas import tpu as pltpu

F32 = jnp.float32
BF16 = jnp.bfloat16

RMS_EPS = 1e-6
MASK_VALUE = -1e30

GRID_W = 64
SSD_HEADS = 16
SSD_HEAD_DIM = 64
SSD_GROUPS = 2
SSD_STATE = 128
SSD_CHUNK = 128
NA_HEADS = 16
NA_HEAD_DIM = 64
NA_WIN_R = 8
NA_WIN_C = 16

LANES = 128
SUBLANES = 8
VMEM_LIMIT = 56 * 1024 * 1024


def _cparams(semantics):
    return pltpu.CompilerParams(dimension_semantics=semantics, vmem_limit_bytes=VMEM_LIMIT)


def _resident(block_shape, index_map):
    return pl.BlockSpec(block_shape, index_map, pipeline_mode=pl.Buffered(1))


def _rms_mod(x, nw, shift, scale):
    ms = jnp.mean(x * x, axis=-1, keepdims=True)
    return (x * lax.rsqrt(ms + RMS_EPS)) * nw * (1.0 + scale) + shift


def _silu(x):
    return x * jax.nn.sigmoid(x)


def _softplus(x):
    return jnp.maximum(x, 0.0) + jnp.log1p(jnp.exp(-jnp.abs(x)))


def _dot(a, b):
    return jnp.dot(a, b, preferred_element_type=F32)


def _dot_nt(a, b):
    return lax.dot_general(a, b, (((1,), (1,)), ((), ())), preferred_element_type=F32)


def _split_bf16(a, parts):
    out = []
    r = a
    for _ in range(parts):
        h = r.astype(BF16)
        out.append(h)
        r = r - h.astype(F32)
    return out


def _modvec_kernel(c_ref, w_ref, b_ref, o_ref):
    s = _silu(c_ref[...]).astype(BF16)
    o_ref[...] = _dot(s, w_ref[...].astype(BF16)) + b_ref[...]


def modvec(cc, mod_w, mod_b, tn=1536):
    depth, d, n = mod_w.shape
    rows = cc.shape[0]
    return pl.pallas_call(
        _modvec_kernel,
        grid=(depth, n // tn),
        in_specs=[
            pl.BlockSpec((rows, d), lambda i, j: (0, 0)),
            pl.BlockSpec((None, d, tn), lambda i, j: (i, 0, j)),
            pl.BlockSpec((None, 1, tn), lambda i, j: (i, 0, j)),
        ],
        out_specs=pl.BlockSpec((None, rows, tn), lambda i, j: (i, 0, j)),
        out_shape=jax.ShapeDtypeStruct((depth, rows, n), F32),
        compiler_params=_cparams(("arbitrary", "arbitrary")),
        name="modvec",
    )(cc, mod_w, mod_b.reshape(depth, 1, n))


SSD_INNER = SSD_HEADS * SSD_HEAD_DIM
SSD_BC = 2 * SSD_GROUPS * SSD_STATE
XBC_COLS = SSD_INNER + SSD_BC
DT_COLS = 2 * LANES


def _inproj0_kernel(n_tiles, x_ref, xp_ref, xn_ref, nw_ref, sh_ref, sc_ref, w_ref, cw_ref, cb_ref,
                    xs_ref, bc_ref, z_ref, q_ref, k_ref, v_ref, dt_ref):
    i = pl.program_id(1)
    tm, d = x_ref.shape
    hb = SUBLANES
    ext = tm + 2 * hb
    x_ext = jnp.concatenate([xp_ref[...], x_ref[...], xn_ref[...]], axis=0)
    h_ext = _rms_mod(x_ext, nw_ref[...], sh_ref[...], sc_ref[...]).astype(BF16)
    u = _dot(h_ext, w_ref[:, :XBC_COLS])
    rows = lax.broadcasted_iota(jnp.int32, u.shape, 0)
    outside = ((rows < hb) & (i == 0)) | ((rows >= tm + hb) & (i == n_tiles - 1))
    u = jnp.where(outside, 0.0, u)
    xc = (pltpu.roll(u, 1, 0) * cw_ref[0:1, :] + u * cw_ref[1:2, :]
          + pltpu.roll(u, ext - 1, 0) * cw_ref[2:3, :])[hb:hb + tm, :] + cb_ref[...]
    xc = _silu(xc)
    xs_ref[...] = xc[:, :SSD_INNER]
    bc_ref[...] = xc[:, SSD_INNER:].astype(bc_ref.dtype)

    h = h_ext[hb:hb + tm, :]
    c0 = XBC_COLS
    for ref, width in ((z_ref, d), (q_ref, d), (k_ref, d), (v_ref, d), (dt_ref, DT_COLS)):
        y = _dot(h, w_ref[:, c0:c0 + width]).astype(ref.dtype)
        if len(ref.shape) == 3:
            for p in range(ref.shape[0]):
                ref[p] = y[:, p * LANES:(p + 1) * LANES]
        else:
            ref[...] = y
        c0 += width


def inproj0(x, nw, shift, scale, w, conv_w, conv_b, tm):
    bsz, seq, d = x.shape
    ncols = w.shape[1]
    npair = d // LANES
    n_tiles = seq // tm
    per_tile = tm // SUBLANES
    n_halo = seq // SUBLANES
    tok = lambda width: pl.BlockSpec((None, tm, width), lambda b, i: (b, i, 0))
    pair_major = pl.BlockSpec((None, npair, tm, LANES), lambda b, i: (b, 0, i, 0))
    vec = pl.BlockSpec((None, 1, d), lambda b, i: (b, 0, 0))
    const = lambda shape: pl.BlockSpec(shape, lambda b, i: (0,) * len(shape))
    out_shapes = (
        jax.ShapeDtypeStruct((bsz, seq, SSD_INNER), F32),
        jax.ShapeDtypeStruct((bsz, seq, SSD_BC), BF16),
        jax.ShapeDtypeStruct((bsz, seq, d), F32),
        jax.ShapeDtypeStruct((bsz, npair, seq, LANES), BF16),
        jax.ShapeDtypeStruct((bsz, npair, seq, LANES), BF16),
        jax.ShapeDtypeStruct((bsz, seq, d), BF16),
        jax.ShapeDtypeStruct((bsz, seq, DT_COLS), F32),
    )
    return pl.pallas_call(
        functools.partial(_inproj0_kernel, n_tiles),
        grid=(bsz, n_tiles),
        in_specs=[tok(d),
                  pl.BlockSpec((None, SUBLANES, d), lambda b, i: (b, jnp.maximum(i * per_tile - 1, 0), 0)),
                  pl.BlockSpec((None, SUBLANES, d),
                               lambda b, i: (b, jnp.minimum((i + 1) * per_tile, n_halo - 1), 0)),
                  const((1, d)), vec, vec,
                  _resident((d, ncols), lambda b, i: (0, 0)),
                  const((3, XBC_COLS)), const((1, XBC_COLS))],
        out_specs=(tok(SSD_INNER), tok(SSD_BC), tok(d), pair_major, pair_major, tok(d), tok(DT_COLS)),
        out_shape=out_shapes,
        compiler_params=_cparams(("arbitrary", "arbitrary")),
        name="inproj0",
    )(x, x, x, nw, shift, scale, w, conv_w, conv_b)


SSD_CHUNKS_PER_STEP = 2


def _ssd_kernel(backward, final, n_ctx_steps, n_lat_steps, *refs):
    if final:
        (xs_ref, bc_ref, dt_ref, xsc_ref, bcc_ref, dtc_ref, dtb_ref, alog_ref,
         dsk_ref, e_ref, z_ref, yprev_ref, gnw_ref, y_ref, state_ref) = refs
    else:
        (xs_ref, bc_ref, dt_ref, xsc_ref, bcc_ref, dtc_ref, dtb_ref, alog_ref,
         dsk_ref, e_ref, y_ref, state_ref) = refs
    q = SSD_CHUNK
    nch = SSD_CHUNKS_PER_STEP
    inner = SSD_INNER
    gn = SSD_STATE
    gcols = inner // SSD_GROUPS
    c = pl.program_id(1)

    ri = lax.broadcasted_iota(jnp.int32, (q, q), 0)
    ci = lax.broadcasted_iota(jnp.int32, (q, q), 1)
    causal = (ri <= ci) if backward else (ri >= ci)
    tri = jnp.where(causal, 1.0, 0.0).astype(BF16)
    lane = lax.broadcasted_iota(jnp.int32, (q, LANES), 1)
    low_half = lane < SSD_HEAD_DIM

    def process(x, bc16, dtraw, out_rows):
        want_y = out_rows is not None
        bm16 = bc16[:, :SSD_GROUPS * gn]
        cm16 = bc16[:, SSD_GROUPS * gn:]

        dt = _softplus(dtraw + dtb_ref[...])
        a = dt * (-jnp.exp(alog_ref[...]))
        hi, mid, lo = _split_bf16(a, 3)
        a_cum = _dot(tri, hi) + _dot(tri, mid) + _dot(tri, lo)
        edge = a_cum[0:1, :] if backward else a_cum[q - 1:q, :]
        ea = jnp.exp(a_cum)
        dtw = dt * jnp.exp(edge - a_cum)
        e = e_ref[...]
        p0, p1, p2 = _split_bf16(ea, 3)
        ea_full = _dot(p0, e) + _dot(p1, e) + _dot(p2, e)
        p0, p1 = _split_bf16(dtw, 2)
        dtw_full = _dot(p0, e) + _dot(p1, e)
        chunk_decay = ea_full[0:1, :] if backward else ea_full[q - 1:q, :]

        xw = (x * dtw_full).astype(BF16)
        x16 = x.astype(BF16)
        bts = [bm16[:, g * gn:(g + 1) * gn].astype(F32).T.astype(BF16) for g in range(SSD_GROUPS)]

        y_part = None
        if want_y:
            a_cum_t = a_cum.T
            dt_t = dt.T
            ys = []
            for g in range(SSD_GROUPS):
                cb = _dot_nt(cm16[:, g * gn:(g + 1) * gn], bm16[:, g * gn:(g + 1) * gn])
                heads_per_group = SSD_HEADS // SSD_GROUPS
                for pair in range(heads_per_group // 2):
                    ms = []
                    for hh in range(2):
                        h = g * heads_per_group + pair * 2 + hh
                        seg = a_cum[:, h:h + 1] - a_cum_t[h:h + 1, :]
                        decay = jnp.exp(jnp.where(causal, seg, -jnp.inf))
                        ms.append((cb * decay * dt_t[h:h + 1, :]).astype(BF16))
                    m_pair = jnp.concatenate(ms, axis=1)
                    col0 = (g * heads_per_group + pair * 2) * SSD_HEAD_DIM
                    xp = x16[:, col0:col0 + LANES]
                    zero = jnp.zeros_like(xp)
                    x_bd = jnp.concatenate([jnp.where(low_half, xp, zero),
                                            jnp.where(low_half, zero, xp)], axis=0)
                    ys.append(_dot(m_pair, x_bd))
            y_part = jnp.concatenate(ys, axis=1)
            if final:
                y_part = y_part + yprev_ref[out_rows, :]
            else:
                y_part = y_part + x * dsk_ref[...]

        def finish():
            if want_y:
                y_off = jnp.concatenate(
                    [_dot(cm16[:, g * gn:(g + 1) * gn], state_ref[:, g * gcols:(g + 1) * gcols].astype(BF16))
                     for g in range(SSD_GROUPS)], axis=1) * ea_full
                y = y_part + y_off
                if final:
                    y = y * _silu(z_ref[out_rows, :])
                    ms_ = jnp.mean(y * y, axis=-1, keepdims=True)
                    y = (y * lax.rsqrt(ms_ + RMS_EPS)) * gnw_ref[...]
                y_ref[out_rows, :] = y.astype(y_ref.dtype)
            for g in range(SSD_GROUPS):
                sl = slice(g * gcols, (g + 1) * gcols)
                state_ref[:, sl] = state_ref[:, sl] * chunk_decay[:, sl] + _dot(bts[g], xw[:, sl])

        return finish

    def run_block(x_ref_, bc_ref_, dt_ref_, r0, is_latent):
        order = range(nch - 1, -1, -1) if backward else range(nch)
        finishers = []
        for k in order:
            rows = slice(r0 + k * q, r0 + (k + 1) * q)
            out_rows = slice(k * q, (k + 1) * q) if is_latent else None
            finishers.append(process(x_ref_[rows, :], bc_ref_[rows, :], dt_ref_[rows, :], out_rows))
        for fin in finishers:
            fin()

    @pl.when(c == 0)
    def _():
        state_ref[...] = jnp.zeros_like(state_ref)

    blk = nch * q
    for step in range(n_ctx_steps):
        cs = (n_ctx_steps - 1 - step) if backward else step

        @pl.when(c == step)
        def _(cs=cs):
            run_block(xsc_ref, bcc_ref, dtc_ref, cs * blk, False)

    @pl.when(c >= n_ctx_steps)
    def _():
        run_block(xs_ref, bc_ref, dt_ref, 0, True)


def ssd_pass(backward, xs, bc, dt, xs_c, bc_c, dt_c, dt_bias_row, a_log_row, d_skip_row, expand,
             z=None, y_prev=None, gn_w=None):
    final = z is not None
    bsz, seq, inner = xs.shape
    n_ctx = xs_c.shape[1]
    blk = SSD_CHUNK * SSD_CHUNKS_PER_STEP
    assert n_ctx % blk == 0 and seq % blk == 0
    ncs, nls = n_ctx // blk, seq // blk

    def lat(c):
        ls = jnp.maximum(c - ncs, 0)
        return (nls - 1 - ls) if backward else ls

    direction = 1 if backward else 0
    block = lambda width: pl.BlockSpec((None, blk, width), lambda b, c: (b, lat(c), 0))
    whole = lambda width: pl.BlockSpec((None, n_ctx, width), lambda b, c: (b, 0, 0))
    const = lambda shape: pl.BlockSpec(shape, lambda b, c: (0,) * len(shape))
    in_specs = [
        block(inner), block(SSD_BC),
        pl.BlockSpec((None, blk, LANES), lambda b, c: (b, lat(c), direction)),
        whole(inner), whole(SSD_BC),
        pl.BlockSpec((None, n_ctx, LANES), lambda b, c: (b, 0, direction)),
        const((1, LANES)), const((1, LANES)), const((1, inner)), const((LANES, inner)),
    ]
    args = [xs, bc, dt, xs_c, bc_c, dt_c, dt_bias_row, a_log_row, d_skip_row, expand]
    if final:
        in_specs += [block(inner), block(inner), const((1, inner))]
        args += [z, y_prev, gn_w]
    return pl.pallas_call(
        functools.partial(_ssd_kernel, backward, final, ncs, nls),
        grid=(bsz, ncs + nls),
        in_specs=in_specs,
        out_specs=block(inner),
        out_shape=jax.ShapeDtypeStruct((bsz, seq, inner), BF16 if final else F32),
        scratch_shapes=[pltpu.VMEM((SSD_STATE, inner), F32)],
        compiler_params=_cparams(("arbitrary", "arbitrary")),
        name="ssd_bwd" if backward else "ssd_fwd",
    )(*args)


NA_GROUP_HEADS = 4
NA_ROWS_PER_STEP = 4


def _natten_kernel(grid_rows, q_ref, k_ref, v_ref, kc_ref, vc_ref, bias_ref, o_ref):
    w = GRID_W
    nwin = NA_WIN_R * w
    hg = NA_GROUP_HEADS
    gd = hg * NA_HEAD_DIM
    ngroups = NA_HEADS // hg
    lane_head = lax.broadcasted_iota(jnp.int32, (w, gd), 1) // NA_HEAD_DIM
    low_half = lax.broadcasted_iota(jnp.int32, (w, LANES), 1) < NA_HEAD_DIM

    def one_row(i, carry):
        r = pl.program_id(1) * NA_ROWS_PER_STEP + i
        rs = jnp.clip(r - NA_WIN_R // 2, 0, grid_rows - NA_WIN_R)
        k0 = pl.multiple_of(rs * w, w)
        q0 = pl.multiple_of(i * w, w)
        dr0 = rs - r + NA_WIN_R - 1

        def scores(g):
            s_win, s_ctx = [], []
            for pp in range(hg // 2):
                pair = g * (hg // 2) + pp
                qp = q_ref[pair, pl.ds(q0, w), :]
                zero = jnp.zeros_like(qp)
                wq = jnp.concatenate([jnp.where(low_half, qp, zero), jnp.where(low_half, zero, qp)], axis=0)
                s_win.append(_dot_nt(wq, k_ref[pair, pl.ds(k0, nwin), :]))
                s_ctx.append(_dot_nt(wq, kc_ref[pair]))
            return jnp.concatenate(s_win, axis=0), jnp.concatenate(s_ctx, axis=0)

        def softmax(g, s):
            s_win, s_ctx = s
            s_win = jnp.concatenate(
                [s_win[:, t * LANES:(t + 1) * LANES] + bias_ref[g, dr0 + 2 * t]
                 for t in range(nwin // LANES)], axis=1)
            m = jnp.maximum(jnp.max(s_win, axis=-1, keepdims=True), jnp.max(s_ctx, axis=-1, keepdims=True))
            p_win = jnp.exp(s_win - m)
            p_ctx = jnp.exp(s_ctx - m)
            denom = jnp.sum(p_win, axis=-1, keepdims=True) + jnp.sum(p_ctx, axis=-1, keepdims=True)
            return p_win.astype(BF16), p_ctx.astype(BF16), denom

        def values(g, p):
            cols = slice(g * gd, (g + 1) * gd)
            p_win, p_ctx, denom = p
            o = (_dot(p_win, v_ref[pl.ds(k0, nwin), cols]) + _dot(p_ctx, vc_ref[:, cols])) / denom
            acc = o[:w, :]
            for hh in range(1, hg):
                acc = jnp.where(lane_head == hh, o[hh * w:(hh + 1) * w, :], acc)
            o_ref[pl.ds(q0, w), cols] = acc.astype(o_ref.dtype)

        s, p = {}, {}
        for step in range(ngroups + 2):
            if step < ngroups:
                s[step] = scores(step)
            if 0 <= step - 1 < ngroups:
                p[step - 1] = softmax(step - 1, s.pop(step - 1))
            if 0 <= step - 2 < ngroups:
                values(step - 2, p.pop(step - 2))
        return carry

    lax.fori_loop(0, NA_ROWS_PER_STEP, one_row, 0)


def natten(q, k, v, kc, vc, bias2):
    bsz, seq, d = v.shape
    n_ctx = vc.shape[1]
    grid_rows = seq // GRID_W
    npair = NA_HEADS // 2
    rows_tok = NA_ROWS_PER_STEP * GRID_W
    return pl.pallas_call(
        functools.partial(_natten_kernel, grid_rows),
        grid=(bsz, grid_rows // NA_ROWS_PER_STEP),
        in_specs=[
            pl.BlockSpec((None, npair, rows_tok, LANES), lambda b, r: (b, 0, r, 0)),
            _resident((None, npair, seq, LANES), lambda b, r: (b, 0, 0, 0)),
            _resident((None, seq, d), lambda b, r: (b, 0, 0)),
            _resident((None, npair, n_ctx, LANES), lambda b, r: (b, 0, 0, 0)),
            _resident((None, n_ctx, d), lambda b, r: (b, 0, 0)),
            _resident(bias2.shape, lambda b, r: (0, 0, 0, 0)),
        ],
        out_specs=pl.BlockSpec((None, rows_tok, d), lambda b, r: (b, r, 0)),
        out_shape=jax.ShapeDtypeStruct((bsz, seq, d), BF16),
        compiler_params=_cparams(("arbitrary", "arbitrary")),
        name="natten",
    )(q, k, v, kc, vc, bias2)


def natten_bias_table(rpb):
    w = GRID_W
    hg = NA_GROUP_HEADS
    qc = np.arange(w)[:, None]
    kc = np.arange(w)[None, :]
    win_start = np.clip(qc - NA_WIN_C // 2, 0, w - NA_WIN_C)
    col_ok = (kc >= win_start) & (kc < win_start + NA_WIN_C)
    dc_idx = np.clip(kc - qc, -(NA_WIN_C - 1), NA_WIN_C - 1) + NA_WIN_C - 1
    ndc = 2 * NA_WIN_C - 1
    onehot = (np.arange(ndc)[:, None, None] == dc_idx[None]).astype(np.float32)
    t = jnp.einsum("hrc,cqk->hrqk", rpb.astype(F32), jnp.asarray(onehot), precision=lax.Precision.HIGHEST)
    t = jnp.where(col_ok[None, None], t, MASK_VALUE)
    ndr = 2 * NA_WIN_R - 2
    t2 = jnp.concatenate([t[:, :ndr], t[:, 1:ndr + 1]], axis=-1)
    t2 = t2.reshape(NA_HEADS // hg, hg, ndr, w, 2 * w).transpose(0, 2, 1, 3, 4)
    return t2.reshape(NA_HEADS // hg, ndr, hg * w, 2 * w)


def _outproj_kernel(x_ref, ya_ref, yb_ref, w_ref, gate_ref, o_ref):
    ka = ya_ref.shape[-1]
    y = _dot(ya_ref[...], w_ref[:ka, :]) + _dot(yb_ref[...], w_ref[ka:, :])
    o_ref[...] = x_ref[...] + gate_ref[...] * y


def outproj(x, ya, yb, w, gate, tm):
    bsz, seq, d = x.shape
    tok = lambda width: pl.BlockSpec((None, tm, width), lambda b, i: (b, i, 0))
    return pl.pallas_call(
        _outproj_kernel,
        grid=(bsz, seq // tm),
        in_specs=[tok(d), tok(ya.shape[-1]), tok(yb.shape[-1]),
                  _resident(w.shape, lambda b, i: (0, 0)),
                  pl.BlockSpec((None, 1, d), lambda b, i: (b, 0, 0))],
        out_specs=tok(d),
        out_shape=jax.ShapeDtypeStruct((bsz, seq, d), F32),
        compiler_params=_cparams(("arbitrary", "arbitrary")),
        name="outproj",
    )(x, ya, yb, w, gate)


def _mlp_kernel(ff_chunk, final_norm, *refs):
    if final_norm:
        x_ref, nw_ref, sh_ref, sc_ref, gate_ref, w1_ref, w2_ref, fnw_ref, o_ref = refs
    else:
        x_ref, nw_ref, sh_ref, sc_ref, gate_ref, w1_ref, w2_ref, o_ref = refs
    x = x_ref[...]
    h = _rms_mod(x, nw_ref[...], sh_ref[...], sc_ref[...]).astype(BF16)
    dff = w1_ref.shape[1]
    acc = None
    for c0 in range(0, dff, ff_chunk):
        a = jnp.maximum(_dot(h, w1_ref[:, c0:c0 + ff_chunk]), 0.0)
        part = _dot((a * a).astype(BF16), w2_ref[c0:c0 + ff_chunk, :])
        acc = part if acc is None else acc + part
    y = x + gate_ref[...] * acc
    if final_norm:
        ms = jnp.mean(y * y, axis=-1, keepdims=True)
        y = (y * lax.rsqrt(ms + RMS_EPS)) * fnw_ref[...]
    o_ref[...] = y


def mlp(x, nw, shift, scale, gate, w1, w2, tm, ff_chunk, final_nw=None):
    bsz, seq, d = x.shape
    tok = pl.BlockSpec((None, tm, d), lambda b, i: (b, i, 0))
    vec = pl.BlockSpec((None, 1, d), lambda b, i: (b, 0, 0))
    row = pl.BlockSpec((1, d), lambda b, i: (0, 0))
    in_specs = [tok, row, vec, vec, vec,
                _resident(w1.shape, lambda b, i: (0, 0)), _resident(w2.shape, lambda b, i: (0, 0))]
    args = [x, nw, shift, scale, gate, w1, w2]
    if final_nw is not None:
        in_specs.append(row)
        args.append(final_nw)
    return pl.pallas_call(
        functools.partial(_mlp_kernel, ff_chunk, final_nw is not None),
        grid=(bsz, seq // tm),
        in_specs=in_specs,
        out_specs=tok,
        out_shape=jax.ShapeDtypeStruct((bsz, seq, d), F32),
        compiler_params=_cparams(("arbitrary", "arbitrary")),
        name="mlp_final" if final_nw is not None else "mlp",
    )(*args)


def _shortconv_kernel(n_tiles, x_ref, xp_ref, xn_ref, nw_ref, sh_ref, sc_ref, gate_ref,
                      win_ref, cw_ref, wout_ref, o_ref):
    i = pl.program_id(1)
    tm, d = x_ref.shape
    inner = wout_ref.shape[0]
    hb = SUBLANES
    x = x_ref[...]
    x_ext = jnp.concatenate([xp_ref[...], x, xn_ref[...]], axis=0)
    h = _rms_mod(x_ext, nw_ref[...], sh_ref[...], sc_ref[...]).astype(BF16)
    gate_c = _dot(h, win_ref[:, inner:2 * inner])
    val = _dot(h, win_ref[:, 2 * inner:])
    u = gate_c * val
    rows = lax.broadcasted_iota(jnp.int32, u.shape, 0)
    outside = ((rows < hb) & (i == 0)) | ((rows >= tm + hb) & (i == n_tiles - 1))
    u = jnp.where(outside, 0.0, u)
    ext = tm + 2 * hb
    conv = (pltpu.roll(u, 1, 0) * cw_ref[0:1, :] + u * cw_ref[1:2, :]
            + pltpu.roll(u, ext - 1, 0) * cw_ref[2:3, :])[hb:hb + tm, :]
    gate_b = _dot(h[hb:hb + tm, :], win_ref[:, :inner])
    y = _dot((gate_b * conv).astype(BF16), wout_ref[...])
    o_ref[...] = x + gate_ref[...] * y


def shortconv(x, nw, shift, scale, gate, w_in, conv_w, w_out, tm):
    bsz, seq, d = x.shape
    n_tiles = seq // tm
    per_tile = tm // SUBLANES
    n_halo = seq // SUBLANES
    tok = pl.BlockSpec((None, tm, d), lambda b, i: (b, i, 0))
    vec = pl.BlockSpec((None, 1, d), lambda b, i: (b, 0, 0))
    return pl.pallas_call(
        functools.partial(_shortconv_kernel, n_tiles),
        grid=(bsz, n_tiles),
        in_specs=[
            tok,
            pl.BlockSpec((None, SUBLANES, d), lambda b, i: (b, jnp.maximum(i * per_tile - 1, 0), 0)),
            pl.BlockSpec((None, SUBLANES, d), lambda b, i: (b, jnp.minimum((i + 1) * per_tile, n_halo - 1), 0)),
            pl.BlockSpec((1, d), lambda b, i: (0, 0)), vec, vec, vec,
            _resident(w_in.shape, lambda b, i: (0, 0)),
            pl.BlockSpec(conv_w.shape, lambda b, i: (0, 0)),
            _resident(w_out.shape, lambda b, i: (0, 0)),
        ],
        out_specs=tok,
        out_shape=jax.ShapeDtypeStruct((bsz, seq, d), F32),
        compiler_params=_cparams(("arbitrary", "arbitrary")),
        name="shortconv",
    )(x, x, x, nw, shift, scale, gate, w_in, conv_w, w_out)


def _pad_lanes(row):
    return jnp.zeros((1, LANES), F32).at[0, :row.shape[0]].set(row.astype(F32))


def kernel(x, c, ctx, c_ctx, mod_w, mod_b, norm_mix_w, norm_mlp_w, mlp_w1, mlp_w2, ssdna_in_w, ssdna_conv_w,
           ssdna_conv_b, ssd_dt_bias, ssd_a_log, ssd_d, ssd_norm_w, na_rpb, ssdna_out_w, sc_in_w, sc_conv_w,
           sc_out_w, final_norm_w):
    bsz, seq, d = x.shape
    n_ctx = ctx.shape[1]
    tm = min(512, seq)
    inner = SSD_HEADS * SSD_HEAD_DIM
    gn2 = SSD_GROUPS * SSD_STATE

    mrows = -(-(bsz + 1) // SUBLANES) * SUBLANES
    cc = jnp.zeros((mrows, d), F32).at[:bsz].set(c).at[bsz].set(c_ctx)
    mod = modvec(cc, mod_w, mod_b).reshape(mod_w.shape[0], mrows, 6, d)
    vecs = lambda i: [mod[i, :bsz, j].reshape(bsz, 1, d) for j in range(6)]
    row = lambda v: v.reshape(1, -1).astype(F32)

    shift_a, scale_a, gate_a, shift_f, scale_f, gate_f = vecs(0)
    shift_c = jnp.broadcast_to(mod[0, bsz, 0].reshape(1, 1, d), (bsz, 1, d))
    scale_c = jnp.broadcast_to(mod[0, bsz, 1].reshape(1, 1, d), (bsz, 1, d))

    w = ssdna_in_w[0]
    o_b, o_dt, o_k, o_v = inner, inner + gn2, inner + gn2 + 2 * SSD_HEADS, inner + gn2 + 2 * SSD_HEADS + d
    o_c = o_v + d
    o_z, o_q = o_c + gn2, o_c + gn2 + inner
    dt_pad = jnp.zeros((d, LANES - SSD_HEADS), F32)
    w0 = jnp.concatenate([
        w[:, :o_dt], w[:, o_c:o_z],
        w[:, o_z:o_q],
        w[:, o_q:] * (NA_HEAD_DIM ** -0.5),
        w[:, o_k:o_v], w[:, o_v:o_c],
        w[:, o_dt:o_dt + SSD_HEADS], dt_pad, w[:, o_dt + SSD_HEADS:o_k], dt_pad,
    ], axis=1).astype(BF16)
    nw0 = row(norm_mix_w[0])
    conv_w = ssdna_conv_w[0]
    conv_b = row(ssdna_conv_b[0])
    xs, bc, z, q_l, k_l, v_l, dt_l = inproj0(x, nw0, shift_a, scale_a, w0, conv_w, conv_b, tm)
    xs_c, bc_c, _, _, k_c, v_c, dt_c = inproj0(ctx, nw0, shift_c, scale_c, w0, conv_w, conv_b, n_ctx)

    d_skip_row = row(jnp.repeat(ssd_d[0], SSD_HEAD_DIM))
    expand = (np.arange(LANES)[:, None] == (np.arange(inner)[None, :] // SSD_HEAD_DIM)).astype(np.float32)
    expand = jnp.asarray(expand, BF16)
    common = (xs, bc, dt_l, xs_c, bc_c, dt_c)
    y_fwd = ssd_pass(False, *common, _pad_lanes(ssd_dt_bias[0][0]), _pad_lanes(ssd_a_log[0][0]),
                     d_skip_row, expand)
    y_ssd = ssd_pass(True, *common, _pad_lanes(ssd_dt_bias[0][1]), _pad_lanes(ssd_a_log[0][1]),
                     d_skip_row, expand, z=z, y_prev=y_fwd, gn_w=row(ssd_norm_w[0]))

    y_na = natten(q_l, k_l, v_l, k_c, v_c, natten_bias_table(na_rpb[0]))
    x = outproj(x, y_ssd, y_na, ssdna_out_w[0].astype(BF16), gate_a, tm)
    x = mlp(x, row(norm_mlp_w[0]), shift_f, scale_f, gate_f, mlp_w1[0].astype(BF16), mlp_w2[0].astype(BF16),
            tm, 512)

    shift_a, scale_a, gate_a, shift_f, scale_f, gate_f = vecs(1)
    x = shortconv(x, row(norm_mix_w[1]), shift_a, scale_a, gate_a, sc_in_w[0].astype(BF16), sc_conv_w[0],
                  sc_out_w[0].astype(BF16), tm)
    x = mlp(x, row(norm_mlp_w[1]), shift_f, scale_f, gate_f, mlp_w1[1].astype(BF16), mlp_w2[1].astype(BF16),
            tm, 512, final_nw=row(final_norm_w))
    return x
```

```python
import functools

import numpy as np
import jax
import jax.numpy as jnp
from jax import lax
from jax.experimental import pallas as pl
from jax.experimental.pallas import tpu as pltpu

F32 = jnp.float32
BF16 = jnp.bfloat16

RMS_EPS = 1e-6
MASK_VALUE = -1e30

GRID_W = 64
SSD_HEADS = 16
SSD_HEAD_DIM = 64
SSD_GROUPS = 2
SSD_STATE = 128
SSD_CHUNK = 128
NA_HEADS = 16
NA_HEAD_DIM = 64
NA_WIN_R = 8
NA_WIN_C = 16

LANES = 128
SUBLANES = 8
VMEM_LIMIT = 56 * 1024 * 1024


def _cparams(semantics):
    return pltpu.CompilerParams(dimension_semantics=semantics, vmem_limit_bytes=VMEM_LIMIT)


def _resident(block_shape, index_map):
    return pl.BlockSpec(block_shape, index_map, pipeline_mode=pl.Buffered(1))


def _rms_mod(x, nw, shift, scale):
    ms = jnp.mean(x * x, axis=-1, keepdims=True)
    return (x * lax.rsqrt(ms + RMS_EPS)) * nw * (1.0 + scale) + shift


def _silu(x):
    return x * jax.nn.sigmoid(x)


def _softplus(x):
    return jnp.maximum(x, 0.0) + jnp.log1p(jnp.exp(-jnp.abs(x)))


def _dot(a, b):
    return jnp.dot(a, b, preferred_element_type=F32)


def _dot_nt(a, b):
    return lax.dot_general(a, b, (((1,), (1,)), ((), ())), preferred_element_type=F32)


def _split_bf16(a, parts):
    out = []
    r = a
    for _ in range(parts):
        h = r.astype(BF16)
        out.append(h)
        r = r - h.astype(F32)
    return out


def _modvec_kernel(c_ref, w_ref, b_ref, o_ref):
    s = _silu(c_ref[...]).astype(BF16)
    o_ref[...] = _dot(s, w_ref[...].astype(BF16)) + b_ref[...]


def modvec(cc, mod_w, mod_b, tn=1536):
    depth, d, n = mod_w.shape
    rows = cc.shape[0]
    return pl.pallas_call(
        _modvec_kernel,
        grid=(depth, n // tn),
        in_specs=[
            pl.BlockSpec((rows, d), lambda i, j: (0, 0)),
            pl.BlockSpec((None, d, tn), lambda i, j: (i, 0, j)),
            pl.BlockSpec((None, 1, tn), lambda i, j: (i, 0, j)),
        ],
        out_specs=pl.BlockSpec((None, rows, tn), lambda i, j: (i, 0, j)),
        out_shape=jax.ShapeDtypeStruct((depth, rows, n), F32),
        compiler_params=_cparams(("arbitrary", "arbitrary")),
        name="modvec",
    )(cc, mod_w, mod_b.reshape(depth, 1, n))


SSD_INNER = SSD_HEADS * SSD_HEAD_DIM
SSD_BC = 2 * SSD_GROUPS * SSD_STATE
XBC_COLS = SSD_INNER + SSD_BC
DT_COLS = 2 * LANES


def _inproj0_kernel(n_tiles, x_ref, xp_ref, xn_ref, nw_ref, sh_ref, sc_ref, w_ref, cw_ref, cb_ref,
                    xs_ref, bc_ref, z_ref, q_ref, k_ref, v_ref, dt_ref):
    i = pl.program_id(1)
    tm, d = x_ref.shape
    hb = SUBLANES
    ext = tm + 2 * hb
    x_ext = jnp.concatenate([xp_ref[...], x_ref[...], xn_ref[...]], axis=0)
    h_ext = _rms_mod(x_ext, nw_ref[...], sh_ref[...], sc_ref[...]).astype(BF16)
    u = _dot(h_ext, w_ref[:, :XBC_COLS])
    rows = lax.broadcasted_iota(jnp.int32, u.shape, 0)
    outside = ((rows < hb) & (i == 0)) | ((rows >= tm + hb) & (i == n_tiles - 1))
    u = jnp.where(outside, 0.0, u)
    xc = (pltpu.roll(u, 1, 0) * cw_ref[0:1, :] + u * cw_ref[1:2, :]
          + pltpu.roll(u, ext - 1, 0) * cw_ref[2:3, :])[hb:hb + tm, :] + cb_ref[...]
    xc = _silu(xc)
    xs_ref[...] = xc[:, :SSD_INNER]
    bc_ref[...] = xc[:, SSD_INNER:].astype(bc_ref.dtype)

    h = h_ext[hb:hb + tm, :]
    c0 = XBC_COLS
    for ref, width in ((z_ref, d), (q_ref, d), (k_ref, d), (v_ref, d), (dt_ref, DT_COLS)):
        y = _dot(h, w_ref[:, c0:c0 + width]).astype(ref.dtype)
        if len(ref.shape) == 3:
            for p in range(ref.shape[0]):
                ref[p] = y[:, p * LANES:(p + 1) * LANES]
        else:
            ref[...] = y
        c0 += width


def inproj0(x, nw, shift, scale, w, conv_w, conv_b, tm):
    bsz, seq, d = x.shape
    ncols = w.shape[1]
    npair = d // LANES
    n_tiles = seq // tm
    per_tile = tm // SUBLANES
    n_halo = seq // SUBLANES
    tok = lambda width: pl.BlockSpec((None, tm, width), lambda b, i: (b, i, 0))
    pair_major = pl.BlockSpec((None, npair, tm, LANES), lambda b, i: (b, 0, i, 0))
    vec = pl.BlockSpec((None, 1, d), lambda b, i: (b, 0, 0))
    const = lambda shape: pl.BlockSpec(shape, lambda b, i: (0,) * len(shape))
    out_shapes = (
        jax.ShapeDtypeStruct((bsz, seq, SSD_INNER), F32),
        jax.ShapeDtypeStruct((bsz, seq, SSD_BC), BF16),
        jax.ShapeDtypeStruct((bsz, seq, d), F32),
        jax.ShapeDtypeStruct((bsz, npair, seq, LANES), BF16),
        jax.ShapeDtypeStruct((bsz, npair, seq, LANES), BF16),
        jax.ShapeDtypeStruct((bsz, seq, d), BF16),
        jax.ShapeDtypeStruct((bsz, seq, DT_COLS), F32),
    )
    return pl.pallas_call(
        functools.partial(_inproj0_kernel, n_tiles),
        grid=(bsz, n_tiles),
        in_specs=[tok(d),
                  pl.BlockSpec((None, SUBLANES, d), lambda b, i: (b, jnp.maximum(i * per_tile - 1, 0), 0)),
                  pl.BlockSpec((None, SUBLANES, d),
                               lambda b, i: (b, jnp.minimum((i + 1) * per_tile, n_halo - 1), 0)),
                  const((1, d)), vec, vec,
                  _resident((d, ncols), lambda b, i: (0, 0)),
                  const((3, XBC_COLS)), const((1, XBC_COLS))],
        out_specs=(tok(SSD_INNER), tok(SSD_BC), tok(d), pair_major, pair_major, tok(d), tok(DT_COLS)),
        out_shape=out_shapes,
        compiler_params=_cparams(("arbitrary", "arbitrary")),
        name="inproj0",
    )(x, x, x, nw, shift, scale, w, conv_w, conv_b)


SSD_CHUNKS_PER_STEP = 2
SPLIT_CUMSUM = 3
SPLIT_EXPAND = 2


def _ssd_kernel(backward, n_ctx_steps, n_lat_steps, *refs):
    if backward:
        (xs_ref, bc_ref, dt_ref, xsc_ref, bcc_ref, dtc_ref, dtb_ref, alog_ref, e_ref,
         z_ref, yprev_ref, gnw_ref, y_ref, state_ref) = refs
    else:
        (xs_ref, bc_ref, dt_ref, xsc_ref, bcc_ref, dtc_ref, dtb_ref, alog_ref, e_ref,
         dsk_ref, y_ref, state_ref) = refs
    q = SSD_CHUNK
    nch = SSD_CHUNKS_PER_STEP
    inner = SSD_INNER
    gn = SSD_STATE
    gcols = inner // SSD_GROUPS
    heads_per_group = SSD_HEADS // SSD_GROUPS
    c = pl.program_id(1)
    scan_dir = 1 if backward else 0

    ri = lax.broadcasted_iota(jnp.int32, (q, q), 0)
    ci = lax.broadcasted_iota(jnp.int32, (q, q), 1)
    contributes = (ri >= ci, ri <= ci)
    tri = [jnp.concatenate([jnp.where(m, 1.0, 0.0).astype(BF16)] * SPLIT_CUMSUM, axis=1) for m in contributes]
    lane = lax.broadcasted_iota(jnp.int32, (q, LANES), 1)
    low_half = lane < SSD_HEAD_DIM

    def decay_terms(dtraw, direction):
        cols = slice(direction * LANES, (direction + 1) * LANES)
        dt = _softplus(dtraw[:, cols] + dtb_ref[direction:direction + 1, :])
        a = dt * (-jnp.exp(alog_ref[direction:direction + 1, :]))
        a_cum = _dot(tri[direction], jnp.concatenate(_split_bf16(a, SPLIT_CUMSUM), axis=0))
        return dt, a_cum

    def expand(v):
        return _dot(jnp.concatenate(_split_bf16(v, SPLIT_EXPAND), axis=1), e_ref[...])

    def process(x, bc16, dtraw, out_rows):
        want_y = out_rows is not None
        bm16 = bc16[:, :SSD_GROUPS * gn]
        cm16 = bc16[:, SSD_GROUPS * gn:]

        dt, a_cum = decay_terms(dtraw, scan_dir)
        edge = a_cum[0:1, :] if backward else a_cum[q - 1:q, :]
        ea_full = expand(jnp.exp(a_cum))
        dtw_full = expand(dt * jnp.exp(edge - a_cum))
        chunk_decay = ea_full[0:1, :] if backward else ea_full[q - 1:q, :]
        xw = (x * dtw_full).astype(BF16)
        bts = [bm16[:, g * gn:(g + 1) * gn].astype(F32).T.astype(BF16) for g in range(SSD_GROUPS)]

        y_part = None
        if want_y and backward:
            y_part = yprev_ref[out_rows, :]
        elif want_y:
            x16 = x.astype(BF16)
            dirs = [(dt, a_cum), decay_terms(dtraw, 1)]
            tr = [(d_.T, ac.T) for d_, ac in dirs]
            ys = []
            for g in range(SSD_GROUPS):
                cb = _dot_nt(cm16[:, g * gn:(g + 1) * gn], bm16[:, g * gn:(g + 1) * gn])
                for pair in range(heads_per_group // 2):
                    ms = []
                    for hh in range(2):
                        h = g * heads_per_group + pair * 2 + hh
                        w = None
                        for direction in range(2):
                            ac, (dt_t, ac_t) = dirs[direction][1], tr[direction]
                            seg = ac[:, h:h + 1] - ac_t[h:h + 1, :]
                            term = jnp.exp(jnp.where(contributes[direction], seg, -jnp.inf)) * dt_t[h:h + 1, :]
                            w = term if w is None else w + term
                        ms.append((cb * w).astype(BF16))
                    m_pair = jnp.concatenate(ms, axis=1)
                    col0 = (g * heads_per_group + pair * 2) * SSD_HEAD_DIM
                    xp = x16[:, col0:col0 + LANES]
                    zero = jnp.zeros_like(xp)
                    x_bd = jnp.concatenate([jnp.where(low_half, xp, zero),
                                            jnp.where(low_half, zero, xp)], axis=0)
                    ys.append(_dot(m_pair, x_bd))
            y_part = jnp.concatenate(ys, axis=1) + x * dsk_ref[...]

        def finish():
            if want_y:
                y_off = jnp.concatenate(
                    [_dot(cm16[:, g * gn:(g + 1) * gn], state_ref[:, g * gcols:(g + 1) * gcols].astype(BF16))
                     for g in range(SSD_GROUPS)], axis=1) * ea_full
                y = y_part + y_off
                if backward:
                    y = y * _silu(z_ref[out_rows, :])
                    ms_ = jnp.mean(y * y, axis=-1, keepdims=True)
                    y = (y * lax.rsqrt(ms_ + RMS_EPS)) * gnw_ref[...]
                y_ref[out_rows, :] = y.astype(y_ref.dtype)
            for g in range(SSD_GROUPS):
                sl = slice(g * gcols, (g + 1) * gcols)
                state_ref[:, sl] = state_ref[:, sl] * chunk_decay[:, sl] + _dot(bts[g], xw[:, sl])

        return finish

    def run_block(x_ref_, bc_ref_, dt_ref_, r0, is_latent):
        order = range(nch - 1, -1, -1) if backward else range(nch)
        finishers = []
        for k in order:
            rows = slice(r0 + k * q, r0 + (k + 1) * q)
            out_rows = slice(k * q, (k + 1) * q) if is_latent else None
            finishers.append(process(x_ref_[rows, :], bc_ref_[rows, :], dt_ref_[rows, :], out_rows))
        for fin in finishers:
            fin()

    @pl.when(c == 0)
    def _():
        state_ref[...] = jnp.zeros_like(state_ref)

    blk = nch * q
    for step in range(n_ctx_steps):
        cs = (n_ctx_steps - 1 - step) if backward else step

        @pl.when(c == step)
        def _(cs=cs):
            run_block(xsc_ref, bcc_ref, dtc_ref, cs * blk, False)

    @pl.when(c >= n_ctx_steps)
    def _():
        run_block(xs_ref, bc_ref, dt_ref, 0, True)


def ssd_pass(backward, xs, bc, dt, xs_c, bc_c, dt_c, dt_bias_rows, a_log_rows, expand2,
             d_skip_row=None, z=None, y_prev=None, gn_w=None):
    bsz, seq, inner = xs.shape
    n_ctx = xs_c.shape[1]
    blk = SSD_CHUNK * SSD_CHUNKS_PER_STEP
    assert n_ctx % blk == 0 and seq % blk == 0
    ncs, nls = n_ctx // blk, seq // blk

    def lat(c):
        ls = jnp.maximum(c - ncs, 0)
        return (nls - 1 - ls) if backward else ls

    block = lambda width: pl.BlockSpec((None, blk, width), lambda b, c: (b, lat(c), 0))
    whole = lambda width: pl.BlockSpec((None, n_ctx, width), lambda b, c: (b, 0, 0))
    const = lambda shape: pl.BlockSpec(shape, lambda b, c: (0,) * len(shape))
    in_specs = [
        block(inner), block(SSD_BC), block(DT_COLS),
        whole(inner), whole(SSD_BC), whole(DT_COLS),
        const((2, LANES)), const((2, LANES)), const(expand2.shape),
    ]
    args = [xs, bc, dt, xs_c, bc_c, dt_c, dt_bias_rows, a_log_rows, expand2]
    if backward:
        in_specs += [block(inner), block(inner), const((1, inner))]
        args += [z, y_prev, gn_w]
    else:
        in_specs += [const((1, inner))]
        args += [d_skip_row]
    return pl.pallas_call(
        functools.partial(_ssd_kernel, backward, ncs, nls),
        grid=(bsz, ncs + nls),
        in_specs=in_specs,
        out_specs=block(inner),
        out_shape=jax.ShapeDtypeStruct((bsz, seq, inner), BF16 if backward else F32),
        scratch_shapes=[pltpu.VMEM((SSD_STATE, inner), F32)],
        compiler_params=_cparams(("arbitrary", "arbitrary")),
        name="ssd_bwd" if backward else "ssd_fwd",
    )(*args)


NA_GROUP_HEADS = 4
NA_ROWS_PER_STEP = 4


def _natten_kernel(grid_rows, q_ref, k_ref, v_ref, kc_ref, vc_ref, bias_ref, o_ref):
    w = GRID_W
    nwin = NA_WIN_R * w
    hg = NA_GROUP_HEADS
    gd = hg * NA_HEAD_DIM
    ngroups = NA_HEADS // hg
    lane_head = lax.broadcasted_iota(jnp.int32, (w, gd), 1) // NA_HEAD_DIM
    low_half = lax.broadcasted_iota(jnp.int32, (w, LANES), 1) < NA_HEAD_DIM

    def one_row(i, carry):
        r = pl.program_id(1) * NA_ROWS_PER_STEP + i
        rs = jnp.clip(r - NA_WIN_R // 2, 0, grid_rows - NA_WIN_R)
        k0 = pl.multiple_of(rs * w, w)
        q0 = pl.multiple_of(i * w, w)
        dr0 = rs - r + NA_WIN_R - 1

        def scores(g):
            s_win, s_ctx = [], []
            for pp in range(hg // 2):
                pair = g * (hg // 2) + pp
                qp = q_ref[pair, pl.ds(q0, w), :]
                zero = jnp.zeros_like(qp)
                wq = jnp.concatenate([jnp.where(low_half, qp, zero), jnp.where(low_half, zero, qp)], axis=0)
                s_win.append(_dot_nt(wq, k_ref[pair, pl.ds(k0, nwin), :]))
                s_ctx.append(_dot_nt(wq, kc_ref[pair]))
            return jnp.concatenate(s_win, axis=0), jnp.concatenate(s_ctx, axis=0)

        def softmax(g, s):
            s_win, s_ctx = s
            s_win = jnp.concatenate(
                [s_win[:, t * LANES:(t + 1) * LANES] + bias_ref[g, dr0 + 2 * t]
                 for t in range(nwin // LANES)], axis=1)
            m = jnp.maximum(jnp.max(s_win, axis=-1, keepdims=True), jnp.max(s_ctx, axis=-1, keepdims=True))
            p_win = jnp.exp(s_win - m)
            p_ctx = jnp.exp(s_ctx - m)
            denom = jnp.sum(p_win, axis=-1, keepdims=True) + jnp.sum(p_ctx, axis=-1, keepdims=True)
            return p_win.astype(BF16), p_ctx.astype(BF16), denom

        def values(g, p):
            cols = slice(g * gd, (g + 1) * gd)
            p_win, p_ctx, denom = p
            o = (_dot(p_win, v_ref[pl.ds(k0, nwin), cols]) + _dot(p_ctx, vc_ref[:, cols])) / denom
            acc = o[:w, :]
            for hh in range(1, hg):
                acc = jnp.where(lane_head == hh, o[hh * w:(hh + 1) * w, :], acc)
            o_ref[pl.ds(q0, w), cols] = acc.astype(o_ref.dtype)

        s, p = {}, {}
        for step in range(ngroups + 2):
            if step < ngroups:
                s[step] = scores(step)
            if 0 <= step - 1 < ngroups:
                p[step - 1] = softmax(step - 1, s.pop(step - 1))
            if 0 <= step - 2 < ngroups:
                values(step - 2, p.pop(step - 2))
        return carry

    lax.fori_loop(0, NA_ROWS_PER_STEP, one_row, 0)


def natten(q, k, v, kc, vc, bias2):
    bsz, seq, d = v.shape
    n_ctx = vc.shape[1]
    grid_rows = seq // GRID_W
    npair = NA_HEADS // 2
    rows_tok = NA_ROWS_PER_STEP * GRID_W
    return pl.pallas_call(
        functools.partial(_natten_kernel, grid_rows),
        grid=(bsz, grid_rows // NA_ROWS_PER_STEP),
        in_specs=[
            pl.BlockSpec((None, npair, rows_tok, LANES), lambda b, r: (b, 0, r, 0)),
            _resident((None, npair, seq, LANES), lambda b, r: (b, 0, 0, 0)),
            _resident((None, seq, d), lambda b, r: (b, 0, 0)),
            _resident((None, npair, n_ctx, LANES), lambda b, r: (b, 0, 0, 0)),
            _resident((None, n_ctx, d), lambda b, r: (b, 0, 0)),
            _resident(bias2.shape, lambda b, r: (0, 0, 0, 0)),
        ],
        out_specs=pl.BlockSpec((None, rows_tok, d), lambda b, r: (b, r, 0)),
        out_shape=jax.ShapeDtypeStruct((bsz, seq, d), BF16),
        compiler_params=_cparams(("arbitrary", "arbitrary")),
        name="natten",
    )(q, k, v, kc, vc, bias2)


def natten_bias_table(rpb):
    w = GRID_W
    hg = NA_GROUP_HEADS
    qc = np.arange(w)[:, None]
    kc = np.arange(w)[None, :]
    win_start = np.clip(qc - NA_WIN_C // 2, 0, w - NA_WIN_C)
    col_ok = (kc >= win_start) & (kc < win_start + NA_WIN_C)
    dc_idx = np.clip(kc - qc, -(NA_WIN_C - 1), NA_WIN_C - 1) + NA_WIN_C - 1
    ndc = 2 * NA_WIN_C - 1
    onehot = (np.arange(ndc)[:, None, None] == dc_idx[None]).astype(np.float32)
    t = jnp.einsum("hrc,cqk->hrqk", rpb.astype(F32), jnp.asarray(onehot), precision=lax.Precision.HIGHEST)
    t = jnp.where(col_ok[None, None], t, MASK_VALUE)
    ndr = 2 * NA_WIN_R - 2
    t2 = jnp.concatenate([t[:, :ndr], t[:, 1:ndr + 1]], axis=-1)
    t2 = t2.reshape(NA_HEADS // hg, hg, ndr, w, 2 * w).transpose(0, 2, 1, 3, 4)
    return t2.reshape(NA_HEADS // hg, ndr, hg * w, 2 * w)


def _outproj_kernel(x_ref, ya_ref, yb_ref, w_ref, gate_ref, o_ref):
    ka = ya_ref.shape[-1]
    y = _dot(ya_ref[...], w_ref[:ka, :]) + _dot(yb_ref[...], w_ref[ka:, :])
    o_ref[...] = x_ref[...] + gate_ref[...] * y


def outproj(x, ya, yb, w, gate, tm):
    bsz, seq, d = x.shape
    tok = lambda width: pl.BlockSpec((None, tm, width), lambda b, i: (b, i, 0))
    return pl.pallas_call(
        _outproj_kernel,
        grid=(bsz, seq // tm),
        in_specs=[tok(d), tok(ya.shape[-1]), tok(yb.shape[-1]),
                  _resident(w.shape, lambda b, i: (0, 0)),
                  pl.BlockSpec((None, 1, d), lambda b, i: (b, 0, 0))],
        out_specs=tok(d),
        out_shape=jax.ShapeDtypeStruct((bsz, seq, d), F32),
        compiler_params=_cparams(("arbitrary", "arbitrary")),
        name="outproj",
    )(x, ya, yb, w, gate)


def _mlp_kernel(ff_chunk, final_norm, *refs):
    if final_norm:
        x_ref, nw_ref, sh_ref, sc_ref, gate_ref, w1_ref, w2_ref, fnw_ref, o_ref = refs
    else:
        x_ref, nw_ref, sh_ref, sc_ref, gate_ref, w1_ref, w2_ref, o_ref = refs
    x = x_ref[...]
    h = _rms_mod(x, nw_ref[...], sh_ref[...], sc_ref[...]).astype(BF16)
    dff = w1_ref.shape[1]
    acc = None
    for c0 in range(0, dff, ff_chunk):
        a = jnp.maximum(_dot(h, w1_ref[:, c0:c0 + ff_chunk]), 0.0)
        part = _dot((a * a).astype(BF16), w2_ref[c0:c0 + ff_chunk, :])
        acc = part if acc is None else acc + part
    y = x + gate_ref[...] * acc
    if final_norm:
        ms = jnp.mean(y * y, axis=-1, keepdims=True)
        y = (y * lax.rsqrt(ms + RMS_EPS)) * fnw_ref[...]
    o_ref[...] = y


def mlp(x, nw, shift, scale, gate, w1, w2, tm, ff_chunk, final_nw=None):
    bsz, seq, d = x.shape
    tok = pl.BlockSpec((None, tm, d), lambda b, i: (b, i, 0))
    vec = pl.BlockSpec((None, 1, d), lambda b, i: (b, 0, 0))
    row = pl.BlockSpec((1, d), lambda b, i: (0, 0))
    in_specs = [tok, row, vec, vec, vec,
                _resident(w1.shape, lambda b, i: (0, 0)), _resident(w2.shape, lambda b, i: (0, 0))]
    args = [x, nw, shift, scale, gate, w1, w2]
    if final_nw is not None:
        in_specs.append(row)
        args.append(final_nw)
    return pl.pallas_call(
        functools.partial(_mlp_kernel, ff_chunk, final_nw is not None),
        grid=(bsz, seq // tm),
        in_specs=in_specs,
        out_specs=tok,
        out_shape=jax.ShapeDtypeStruct((bsz, seq, d), F32),
        compiler_params=_cparams(("arbitrary", "arbitrary")),
        name="mlp_final" if final_nw is not None else "mlp",
    )(*args)


def _shortconv_kernel(n_tiles, x_ref, xp_ref, xn_ref, nw_ref, sh_ref, sc_ref, gate_ref,
                      win_ref, cw_ref, wout_ref, o_ref):
    i = pl.program_id(1)
    tm, d = x_ref.shape
    inner = wout_ref.shape[0]
    hb = SUBLANES
    x = x_ref[...]
    x_ext = jnp.concatenate([xp_ref[...], x, xn_ref[...]], axis=0)
    h = _rms_mod(x_ext, nw_ref[...], sh_ref[...], sc_ref[...]).astype(BF16)
    gate_c = _dot(h, win_ref[:, inner:2 * inner])
    val = _dot(h, win_ref[:, 2 * inner:])
    u = gate_c * val
    rows = lax.broadcasted_iota(jnp.int32, u.shape, 0)
    outside = ((rows < hb) & (i == 0)) | ((rows >= tm + hb) & (i == n_tiles - 1))
    u = jnp.where(outside, 0.0, u)
    ext = tm + 2 * hb
    conv = (pltpu.roll(u, 1, 0) * cw_ref[0:1, :] + u * cw_ref[1:2, :]
            + pltpu.roll(u, ext - 1, 0) * cw_ref[2:3, :])[hb:hb + tm, :]
    gate_b = _dot(h[hb:hb + tm, :], win_ref[:, :inner])
    y = _dot((gate_b * conv).astype(BF16), wout_ref[...])
    o_ref[...] = x + gate_ref[...] * y


def shortconv(x, nw, shift, scale, gate, w_in, conv_w, w_out, tm):
    bsz, seq, d = x.shape
    n_tiles = seq // tm
    per_tile = tm // SUBLANES
    n_halo = seq // SUBLANES
    tok = pl.BlockSpec((None, tm, d), lambda b, i: (b, i, 0))
    vec = pl.BlockSpec((None, 1, d), lambda b, i: (b, 0, 0))
    return pl.pallas_call(
        functools.partial(_shortconv_kernel, n_tiles),
        grid=(bsz, n_tiles),
        in_specs=[
            tok,
            pl.BlockSpec((None, SUBLANES, d), lambda b, i: (b, jnp.maximum(i * per_tile - 1, 0), 0)),
            pl.BlockSpec((None, SUBLANES, d), lambda b, i: (b, jnp.minimum((i + 1) * per_tile, n_halo - 1), 0)),
            pl.BlockSpec((1, d), lambda b, i: (0, 0)), vec, vec, vec,
            _resident(w_in.shape, lambda b, i: (0, 0)),
            pl.BlockSpec(conv_w.shape, lambda b, i: (0, 0)),
            _resident(w_out.shape, lambda b, i: (0, 0)),
        ],
        out_specs=tok,
        out_shape=jax.ShapeDtypeStruct((bsz, seq, d), F32),
        compiler_params=_cparams(("arbitrary", "arbitrary")),
        name="shortconv",
    )(x, x, x, nw, shift, scale, gate, w_in, conv_w, w_out)


def _pad_lanes(row):
    return jnp.zeros((1, LANES), F32).at[0, :row.shape[0]].set(row.astype(F32))


def kernel(x, c, ctx, c_ctx, mod_w, mod_b, norm_mix_w, norm_mlp_w, mlp_w1, mlp_w2, ssdna_in_w, ssdna_conv_w,
           ssdna_conv_b, ssd_dt_bias, ssd_a_log, ssd_d, ssd_norm_w, na_rpb, ssdna_out_w, sc_in_w, sc_conv_w,
           sc_out_w, final_norm_w):
    bsz, seq, d = x.shape
    n_ctx = ctx.shape[1]
    tm = min(512, seq)
    inner = SSD_HEADS * SSD_HEAD_DIM
    gn2 = SSD_GROUPS * SSD_STATE

    mrows = -(-(bsz + 1) // SUBLANES) * SUBLANES
    cc = jnp.zeros((mrows, d), F32).at[:bsz].set(c).at[bsz].set(c_ctx)
    mod = modvec(cc, mod_w, mod_b).reshape(mod_w.shape[0], mrows, 6, d)
    vecs = lambda i: [mod[i, :bsz, j].reshape(bsz, 1, d) for j in range(6)]
    row = lambda v: v.reshape(1, -1).astype(F32)

    shift_a, scale_a, gate_a, shift_f, scale_f, gate_f = vecs(0)
    shift_c = jnp.broadcast_to(mod[0, bsz, 0].reshape(1, 1, d), (bsz, 1, d))
    scale_c = jnp.broadcast_to(mod[0, bsz, 1].reshape(1, 1, d), (bsz, 1, d))

    w = ssdna_in_w[0]
    o_b, o_dt, o_k, o_v = inner, inner + gn2, inner + gn2 + 2 * SSD_HEADS, inner + gn2 + 2 * SSD_HEADS + d
    o_c = o_v + d
    o_z, o_q = o_c + gn2, o_c + gn2 + inner
    dt_pad = jnp.zeros((d, LANES - SSD_HEADS), F32)
    w0 = jnp.concatenate([
        w[:, :o_dt], w[:, o_c:o_z],
        w[:, o_z:o_q],
        w[:, o_q:] * (NA_HEAD_DIM ** -0.5),
        w[:, o_k:o_v], w[:, o_v:o_c],
        w[:, o_dt:o_dt + SSD_HEADS], dt_pad, w[:, o_dt + SSD_HEADS:o_k], dt_pad,
    ], axis=1).astype(BF16)
    nw0 = row(norm_mix_w[0])
    conv_w = ssdna_conv_w[0]
    conv_b = row(ssdna_conv_b[0])
    xs, bc, z, q_l, k_l, v_l, dt_l = inproj0(x, nw0, shift_a, scale_a, w0, conv_w, conv_b, tm)
    xs_c, bc_c, _, _, k_c, v_c, dt_c = inproj0(ctx, nw0, shift_c, scale_c, w0, conv_w, conv_b, n_ctx)

    d_skip_row = row(jnp.repeat(ssd_d[0], SSD_HEAD_DIM))
    expand = (np.arange(LANES)[:, None] == (np.arange(inner)[None, :] // SSD_HEAD_DIM)).astype(np.float32)
    expand2 = jnp.asarray(np.concatenate([expand] * SPLIT_EXPAND, axis=0), BF16)
    pad2 = lambda p: jnp.concatenate([_pad_lanes(p[0]), _pad_lanes(p[1])], axis=0)
    common = (xs, bc, dt_l, xs_c, bc_c, dt_c, pad2(ssd_dt_bias[0]), pad2(ssd_a_log[0]), expand2)
    y_fwd = ssd_pass(False, *common, d_skip_row=d_skip_row)
    y_ssd = ssd_pass(True, *common, z=z, y_prev=y_fwd, gn_w=row(ssd_norm_w[0]))

    y_na = natten(q_l, k_l, v_l, k_c, v_c, natten_bias_table(na_rpb[0]))
    x = outproj(x, y_ssd, y_na, ssdna_out_w[0].astype(BF16), gate_a, tm)
    x = mlp(x, row(norm_mlp_w[0]), shift_f, scale_f, gate_f, mlp_w1[0].astype(BF16), mlp_w2[0].astype(BF16),
            tm, 512)

    shift_a, scale_a, gate_a, shift_f, scale_f, gate_f = vecs(1)
    x = shortconv(x, row(norm_mix_w[1]), shift_a, scale_a, gate_a, sc_in_w[0].astype(BF16), sc_conv_w[0],
                  sc_out_w[0].astype(BF16), tm)
    x = mlp(x, row(norm_mlp_w[1]), shift_f, scale_f, gate_f, mlp_w1[1].astype(BF16), mlp_w2[1].astype(BF16),
            tm, 512, final_nw=row(final_norm_w))
    return x
```

```python
import functools

import numpy as np
import jax
import jax.numpy as jnp
from jax import lax
from jax.experimental import pallas as pl
from jax.experimental.pallas import tpu as pltpu

F32 = jnp.float32
BF16 = jnp.bfloat16

RMS_EPS = 1e-6
MASK_VALUE = -1e30

GRID_W = 64
SSD_HEADS = 16
SSD_HEAD_DIM = 64
SSD_GROUPS = 2
SSD_STATE = 128
SSD_CHUNK = 128
NA_HEADS = 16
NA_HEAD_DIM = 64
NA_WIN_R = 8
NA_WIN_C = 16

LANES = 128
SUBLANES = 8
VMEM_LIMIT = 56 * 1024 * 1024


def _cparams(semantics):
    return pltpu.CompilerParams(dimension_semantics=semantics, vmem_limit_bytes=VMEM_LIMIT)


def _resident(block_shape, index_map):
    return pl.BlockSpec(block_shape, index_map, pipeline_mode=pl.Buffered(1))


def _rms_mod(x, nw, shift, scale):
    ms = jnp.mean(x * x, axis=-1, keepdims=True)
    return (x * lax.rsqrt(ms + RMS_EPS)) * nw * (1.0 + scale) + shift


def _silu(x):
    return x * jax.nn.sigmoid(x)


def _softplus(x):
    return jnp.maximum(x, 0.0) + jnp.log1p(jnp.exp(-jnp.abs(x)))


def _dot(a, b):
    return jnp.dot(a, b, preferred_element_type=F32)


def _dot_nt(a, b):
    return lax.dot_general(a, b, (((1,), (1,)), ((), ())), preferred_element_type=F32)


def _split_bf16(a, parts):
    out = []
    r = a
    for _ in range(parts):
        h = r.astype(BF16)
        out.append(h)
        r = r - h.astype(F32)
    return out


def _modvec_kernel(c_ref, w_ref, b_ref, o_ref):
    s = _silu(c_ref[...]).astype(BF16)
    o_ref[...] = _dot(s, w_ref[...].astype(BF16)) + b_ref[...]


def modvec(cc, mod_w, mod_b, tn=1536):
    depth, d, n = mod_w.shape
    rows = cc.shape[0]
    return pl.pallas_call(
        _modvec_kernel,
        grid=(depth, n // tn),
        in_specs=[
            pl.BlockSpec((rows, d), lambda i, j: (0, 0)),
            pl.BlockSpec((None, d, tn), lambda i, j: (i, 0, j)),
            pl.BlockSpec((None, 1, tn), lambda i, j: (i, 0, j)),
        ],
        out_specs=pl.BlockSpec((None, rows, tn), lambda i, j: (i, 0, j)),
        out_shape=jax.ShapeDtypeStruct((depth, rows, n), F32),
        compiler_params=_cparams(("arbitrary", "arbitrary")),
        name="modvec",
    )(cc, mod_w, mod_b.reshape(depth, 1, n))


SSD_INNER = SSD_HEADS * SSD_HEAD_DIM
SSD_BC = 2 * SSD_GROUPS * SSD_STATE
XBC_COLS = SSD_INNER + SSD_BC
DT_COLS = 2 * LANES


def _inproj0_kernel(n_tiles, x_ref, xp_ref, xn_ref, nw_ref, sh_ref, sc_ref, w_ref, cw_ref, cb_ref,
                    xs_ref, bc_ref, z_ref, q_ref, k_ref, v_ref, dt_ref):
    i = pl.program_id(1)
    tm, d = x_ref.shape
    hb = SUBLANES
    ext = tm + 2 * hb
    x_ext = jnp.concatenate([xp_ref[...], x_ref[...], xn_ref[...]], axis=0)
    h_ext = _rms_mod(x_ext, nw_ref[...], sh_ref[...], sc_ref[...]).astype(BF16)
    u = _dot(h_ext, w_ref[:, :XBC_COLS])
    rows = lax.broadcasted_iota(jnp.int32, u.shape, 0)
    outside = ((rows < hb) & (i == 0)) | ((rows >= tm + hb) & (i == n_tiles - 1))
    u = jnp.where(outside, 0.0, u)
    xc = (pltpu.roll(u, 1, 0) * cw_ref[0:1, :] + u * cw_ref[1:2, :]
          + pltpu.roll(u, ext - 1, 0) * cw_ref[2:3, :])[hb:hb + tm, :] + cb_ref[...]
    xc = _silu(xc)
    xs_ref[...] = xc[:, :SSD_INNER]
    bc_ref[...] = xc[:, SSD_INNER:].astype(bc_ref.dtype)

    h = h_ext[hb:hb + tm, :]
    c0 = XBC_COLS
    for ref, width in ((z_ref, d), (q_ref, d), (k_ref, d), (v_ref, d), (dt_ref, DT_COLS)):
        y = _dot(h, w_ref[:, c0:c0 + width]).astype(ref.dtype)
        if len(ref.shape) == 3:
            for p in range(ref.shape[0]):
                ref[p] = y[:, p * LANES:(p + 1) * LANES]
        else:
            ref[...] = y
        c0 += width


def inproj0(x, nw, shift, scale, w, conv_w, conv_b, tm):
    bsz, seq, d = x.shape
    ncols = w.shape[1]
    npair = d // LANES
    n_tiles = seq // tm
    per_tile = tm // SUBLANES
    n_halo = seq // SUBLANES
    tok = lambda width: pl.BlockSpec((None, tm, width), lambda b, i: (b, i, 0))
    pair_major = pl.BlockSpec((None, npair, tm, LANES), lambda b, i: (b, 0, i, 0))
    vec = pl.BlockSpec((None, 1, d), lambda b, i: (b, 0, 0))
    const = lambda shape: pl.BlockSpec(shape, lambda b, i: (0,) * len(shape))
    out_shapes = (
        jax.ShapeDtypeStruct((bsz, seq, SSD_INNER), F32),
        jax.ShapeDtypeStruct((bsz, seq, SSD_BC), BF16),
        jax.ShapeDtypeStruct((bsz, seq, d), F32),
        jax.ShapeDtypeStruct((bsz, npair, seq, LANES), BF16),
        jax.ShapeDtypeStruct((bsz, npair, seq, LANES), BF16),
        jax.ShapeDtypeStruct((bsz, seq, d), BF16),
        jax.ShapeDtypeStruct((bsz, seq, DT_COLS), F32),
    )
    return pl.pallas_call(
        functools.partial(_inproj0_kernel, n_tiles),
        grid=(bsz, n_tiles),
        in_specs=[tok(d),
                  pl.BlockSpec((None, SUBLANES, d), lambda b, i: (b, jnp.maximum(i * per_tile - 1, 0), 0)),
                  pl.BlockSpec((None, SUBLANES, d),
                               lambda b, i: (b, jnp.minimum((i + 1) * per_tile, n_halo - 1), 0)),
                  const((1, d)), vec, vec,
                  _resident((d, ncols), lambda b, i: (0, 0)),
                  const((3, XBC_COLS)), const((1, XBC_COLS))],
        out_specs=(tok(SSD_INNER), tok(SSD_BC), tok(d), pair_major, pair_major, tok(d), tok(DT_COLS)),
        out_shape=out_shapes,
        compiler_params=_cparams(("arbitrary", "arbitrary")),
        name="inproj0",
    )(x, x, x, nw, shift, scale, w, conv_w, conv_b)


SSD_CHUNKS_PER_STEP = 2
SPLIT_CUMSUM = 3
SPLIT_EXPAND = 2


def _ssd_kernel(backward, n_ctx_steps, n_lat_steps, *refs):
    if backward:
        (xs_ref, bc_ref, dt_ref, xsc_ref, bcc_ref, dtc_ref, dtb_ref, alog_ref, e_ref,
         z_ref, yprev_ref, gnw_ref, y_ref, state_ref) = refs
    else:
        (xs_ref, bc_ref, dt_ref, xsc_ref, bcc_ref, dtc_ref, dtb_ref, alog_ref, e_ref,
         dsk_ref, y_ref, state_ref) = refs
    q = SSD_CHUNK
    nch = SSD_CHUNKS_PER_STEP
    inner = SSD_INNER
    gn = SSD_STATE
    gcols = inner // SSD_GROUPS
    heads_per_group = SSD_HEADS // SSD_GROUPS
    c = pl.program_id(1)
    scan_dir = 1 if backward else 0

    ri = lax.broadcasted_iota(jnp.int32, (q, q), 0)
    ci = lax.broadcasted_iota(jnp.int32, (q, q), 1)
    contributes = (ri >= ci, ri <= ci)
    tri = [jnp.concatenate([jnp.where(m, 1.0, 0.0).astype(BF16)] * SPLIT_CUMSUM, axis=1) for m in contributes]
    lane = lax.broadcasted_iota(jnp.int32, (q, LANES), 1)
    low_half = lane < SSD_HEAD_DIM

    def decay_terms(dtraw, direction):
        cols = slice(direction * LANES, (direction + 1) * LANES)
        dt = _softplus(dtraw[:, cols] + dtb_ref[direction:direction + 1, :])
        a = dt * (-jnp.exp(alog_ref[direction:direction + 1, :]))
        a_cum = _dot(tri[direction], jnp.concatenate(_split_bf16(a, SPLIT_CUMSUM), axis=0))
        return dt, a_cum

    def expand(v):
        return _dot(jnp.concatenate(_split_bf16(v, SPLIT_EXPAND), axis=1), e_ref[...])

    def process(x, bc16, dtraw, out_rows):
        want_y = out_rows is not None
        bm16 = bc16[:, :SSD_GROUPS * gn]
        cm16 = bc16[:, SSD_GROUPS * gn:]

        dt, a_cum = decay_terms(dtraw, scan_dir)
        edge = a_cum[0:1, :] if backward else a_cum[q - 1:q, :]
        ea_full = expand(jnp.exp(a_cum))
        dtw_full = expand(dt * jnp.exp(edge - a_cum))
        chunk_decay = ea_full[0:1, :] if backward else ea_full[q - 1:q, :]
        xw = (x * dtw_full).astype(BF16)
        bts = [bm16[:, g * gn:(g + 1) * gn].astype(F32).T.astype(BF16) for g in range(SSD_GROUPS)]

        y_part = None
        if want_y and backward:
            y_part = yprev_ref[out_rows, :]
        elif want_y:
            x16 = x.astype(BF16)
            dirs = [(dt, a_cum), decay_terms(dtraw, 1)]
            tr = [(d_.T, ac.T) for d_, ac in dirs]
            ys = []
            for g in range(SSD_GROUPS):
                cb = _dot_nt(cm16[:, g * gn:(g + 1) * gn], bm16[:, g * gn:(g + 1) * gn])
                for pair in range(heads_per_group // 2):
                    ms = []
                    for hh in range(2):
                        h = g * heads_per_group + pair * 2 + hh
                        w = None
                        for direction in range(2):
                            ac, (dt_t, ac_t) = dirs[direction][1], tr[direction]
                            seg = ac[:, h:h + 1] - ac_t[h:h + 1, :]
                            term = jnp.exp(jnp.where(contributes[direction], seg, -jnp.inf)) * dt_t[h:h + 1, :]
                            w = term if w is None else w + term
                        ms.append((cb * w).astype(BF16))
                    m_pair = jnp.concatenate(ms, axis=1)
                    col0 = (g * heads_per_group + pair * 2) * SSD_HEAD_DIM
                    xp = x16[:, col0:col0 + LANES]
                    zero = jnp.zeros_like(xp)
                    x_bd = jnp.concatenate([jnp.where(low_half, xp, zero),
                                            jnp.where(low_half, zero, xp)], axis=0)
                    ys.append(_dot(m_pair, x_bd))
            y_part = jnp.concatenate(ys, axis=1) + x * dsk_ref[...]

        def finish():
            if want_y:
                y_off = jnp.concatenate(
                    [_dot(cm16[:, g * gn:(g + 1) * gn], state_ref[:, g * gcols:(g + 1) * gcols].astype(BF16))
                     for g in range(SSD_GROUPS)], axis=1) * ea_full
                y = y_part + y_off
                if backward:
                    y = y * _silu(z_ref[out_rows, :])
                    ms_ = jnp.mean(y * y, axis=-1, keepdims=True)
                    y = (y * lax.rsqrt(ms_ + RMS_EPS)) * gnw_ref[...]
                y_ref[out_rows, :] = y.astype(y_ref.dtype)
            for g in range(SSD_GROUPS):
                sl = slice(g * gcols, (g + 1) * gcols)
                state_ref[:, sl] = state_ref[:, sl] * chunk_decay[:, sl] + _dot(bts[g], xw[:, sl])

        return finish

    def run_block(x_ref_, bc_ref_, dt_ref_, r0, is_latent):
        order = range(nch - 1, -1, -1) if backward else range(nch)
        finishers = []
        for k in order:
            rows = slice(r0 + k * q, r0 + (k + 1) * q)
            out_rows = slice(k * q, (k + 1) * q) if is_latent else None
            finishers.append(process(x_ref_[rows, :], bc_ref_[rows, :], dt_ref_[rows, :], out_rows))
        for fin in finishers:
            fin()

    @pl.when(c == 0)
    def _():
        state_ref[...] = jnp.zeros_like(state_ref)

    blk = nch * q
    for step in range(n_ctx_steps):
        cs = (n_ctx_steps - 1 - step) if backward else step

        @pl.when(c == step)
        def _(cs=cs):
            run_block(xsc_ref, bcc_ref, dtc_ref, cs * blk, False)

    @pl.when(c >= n_ctx_steps)
    def _():
        run_block(xs_ref, bc_ref, dt_ref, 0, True)


def ssd_pass(backward, xs, bc, dt, xs_c, bc_c, dt_c, dt_bias_rows, a_log_rows, expand2,
             d_skip_row=None, z=None, y_prev=None, gn_w=None):
    bsz, seq, inner = xs.shape
    n_ctx = xs_c.shape[1]
    blk = SSD_CHUNK * SSD_CHUNKS_PER_STEP
    assert n_ctx % blk == 0 and seq % blk == 0
    ncs, nls = n_ctx // blk, seq // blk

    def lat(c):
        ls = jnp.maximum(c - ncs, 0)
        return (nls - 1 - ls) if backward else ls

    block = lambda width: pl.BlockSpec((None, blk, width), lambda b, c: (b, lat(c), 0))
    whole = lambda width: pl.BlockSpec((None, n_ctx, width), lambda b, c: (b, 0, 0))
    const = lambda shape: pl.BlockSpec(shape, lambda b, c: (0,) * len(shape))
    in_specs = [
        block(inner), block(SSD_BC), block(DT_COLS),
        whole(inner), whole(SSD_BC), whole(DT_COLS),
        const((2, LANES)), const((2, LANES)), const(expand2.shape),
    ]
    args = [xs, bc, dt, xs_c, bc_c, dt_c, dt_bias_rows, a_log_rows, expand2]
    if backward:
        in_specs += [block(inner), block(inner), const((1, inner))]
        args += [z, y_prev, gn_w]
    else:
        in_specs += [const((1, inner))]
        args += [d_skip_row]
    return pl.pallas_call(
        functools.partial(_ssd_kernel, backward, ncs, nls),
        grid=(bsz, ncs + nls),
        in_specs=in_specs,
        out_specs=block(inner),
        out_shape=jax.ShapeDtypeStruct((bsz, seq, inner), BF16 if backward else F32),
        scratch_shapes=[pltpu.VMEM((SSD_STATE, inner), F32)],
        compiler_params=_cparams(("arbitrary", "arbitrary")),
        name="ssd_bwd" if backward else "ssd_fwd",
    )(*args)


NA_GROUP_HEADS = 4
NA_ROWS_PER_STEP = 4


def _natten_kernel(grid_rows, q_ref, k_ref, v_ref, kc_ref, vc_ref, bias_ref, o_ref):
    w = GRID_W
    nwin = NA_WIN_R * w
    hg = NA_GROUP_HEADS
    gd = hg * NA_HEAD_DIM
    ngroups = NA_HEADS // hg
    nrows = NA_ROWS_PER_STEP
    lane_head = lax.broadcasted_iota(jnp.int32, (w, gd), 1) // NA_HEAD_DIM
    low_half = lax.broadcasted_iota(jnp.int32, (w, LANES), 1) < NA_HEAD_DIM
    r_base = pl.program_id(1) * nrows
    rs = [jnp.clip(r_base + i - NA_WIN_R // 2, 0, grid_rows - NA_WIN_R) for i in range(nrows)]
    k0 = [pl.multiple_of(rs[i] * w, w) for i in range(nrows)]
    dr0 = [rs[i] - (r_base + i) + NA_WIN_R - 1 for i in range(nrows)]

    def scores(g):
        s_win = [[] for _ in range(nrows)]
        s_ctx = [[] for _ in range(nrows)]
        for pp in range(hg // 2):
            pair = g * (hg // 2) + pp
            wqs = []
            for i in range(nrows):
                qp = q_ref[pair, i * w:(i + 1) * w, :]
                zero = jnp.zeros_like(qp)
                wqs.append(jnp.concatenate([jnp.where(low_half, qp, zero), jnp.where(low_half, zero, qp)], axis=0))
            sc = _dot_nt(jnp.concatenate(wqs, axis=0), kc_ref[pair])
            for i in range(nrows):
                s_win[i].append(_dot_nt(wqs[i], k_ref[pair, pl.ds(k0[i], nwin), :]))
                s_ctx[i].append(sc[i * 2 * w:(i + 1) * 2 * w, :])
        return [(jnp.concatenate(s_win[i], axis=0), jnp.concatenate(s_ctx[i], axis=0)) for i in range(nrows)]

    def softmax(g, i, s):
        s_win, s_ctx = s
        s_win = jnp.concatenate(
            [s_win[:, t * LANES:(t + 1) * LANES] + bias_ref[g, dr0[i] + 2 * t]
             for t in range(nwin // LANES)], axis=1)
        m = jnp.maximum(jnp.max(s_win, axis=-1, keepdims=True), jnp.max(s_ctx, axis=-1, keepdims=True))
        p_win = jnp.exp(s_win - m)
        p_ctx = jnp.exp(s_ctx - m)
        denom = jnp.sum(p_win, axis=-1, keepdims=True) + jnp.sum(p_ctx, axis=-1, keepdims=True)
        return p_win.astype(BF16), p_ctx.astype(BF16), denom

    def values(g, ps):
        cols = slice(g * gd, (g + 1) * gd)
        o_ctx = _dot(jnp.concatenate([p[1] for p in ps], axis=0), vc_ref[:, cols])
        for i in range(nrows):
            p_win, _, denom = ps[i]
            o = (_dot(p_win, v_ref[pl.ds(k0[i], nwin), cols]) + o_ctx[i * hg * w:(i + 1) * hg * w, :]) / denom
            acc = o[:w, :]
            for hh in range(1, hg):
                acc = jnp.where(lane_head == hh, o[hh * w:(hh + 1) * w, :], acc)
            o_ref[i * w:(i + 1) * w, cols] = acc.astype(o_ref.dtype)

    s, p = {}, {}
    for step in range(ngroups + 2):
        if step < ngroups:
            s[step] = scores(step)
        if 0 <= step - 1 < ngroups:
            sg = s.pop(step - 1)
            p[step - 1] = [softmax(step - 1, i, sg[i]) for i in range(nrows)]
        if 0 <= step - 2 < ngroups:
            values(step - 2, p.pop(step - 2))


def natten(q, k, v, kc, vc, bias2):
    bsz, seq, d = v.shape
    n_ctx = vc.shape[1]
    grid_rows = seq // GRID_W
    npair = NA_HEADS // 2
    rows_tok = NA_ROWS_PER_STEP * GRID_W
    return pl.pallas_call(
        functools.partial(_natten_kernel, grid_rows),
        grid=(bsz, grid_rows // NA_ROWS_PER_STEP),
        in_specs=[
            pl.BlockSpec((None, npair, rows_tok, LANES), lambda b, r: (b, 0, r, 0)),
            pl.BlockSpec((None, npair, seq, LANES), lambda b, r: (b, 0, 0, 0)),
            pl.BlockSpec((None, seq, d), lambda b, r: (b, 0, 0)),
            _resident((None, npair, n_ctx, LANES), lambda b, r: (b, 0, 0, 0)),
            _resident((None, n_ctx, d), lambda b, r: (b, 0, 0)),
            _resident(bias2.shape, lambda b, r: (0, 0, 0, 0)),
        ],
        out_specs=pl.BlockSpec((None, rows_tok, d), lambda b, r: (b, r, 0)),
        out_shape=jax.ShapeDtypeStruct((bsz, seq, d), BF16),
        compiler_params=_cparams(("arbitrary", "arbitrary")),
        name="natten",
    )(q, k, v, kc, vc, bias2)


def natten_bias_table(rpb):
    w = GRID_W
    hg = NA_GROUP_HEADS
    qc = np.arange(w)[:, None]
    kc = np.arange(w)[None, :]
    win_start = np.clip(qc - NA_WIN_C // 2, 0, w - NA_WIN_C)
    col_ok = (kc >= win_start) & (kc < win_start + NA_WIN_C)
    dc_idx = np.clip(kc - qc, -(NA_WIN_C - 1), NA_WIN_C - 1) + NA_WIN_C - 1
    ndc = 2 * NA_WIN_C - 1
    onehot = (np.arange(ndc)[:, None, None] == dc_idx[None]).astype(np.float32)
    t = jnp.einsum("hrc,cqk->hrqk", rpb.astype(F32), jnp.asarray(onehot), precision=lax.Precision.HIGHEST)
    t = jnp.where(col_ok[None, None], t, MASK_VALUE)
    ndr = 2 * NA_WIN_R - 2
    t2 = jnp.concatenate([t[:, :ndr], t[:, 1:ndr + 1]], axis=-1)
    t2 = t2.reshape(NA_HEADS // hg, hg, ndr, w, 2 * w).transpose(0, 2, 1, 3, 4)
    return t2.reshape(NA_HEADS // hg, ndr, hg * w, 2 * w)


def _mlp_kernel(ff_chunk, out_proj, final_norm, *refs):
    refs = list(refs)
    x = refs.pop(0)[...]
    if out_proj:
        ya_ref, yb_ref, wo_ref, gate_a_ref = refs[:4]
        del refs[:4]
        ka = ya_ref.shape[-1]
        x = x + gate_a_ref[...] * (_dot(ya_ref[...], wo_ref[:ka, :]) + _dot(yb_ref[...], wo_ref[ka:, :]))
    nw_ref, sh_ref, sc_ref, gate_ref, w1_ref, w2_ref = refs[:6]
    fnw_ref = refs[6] if final_norm else None
    o_ref = refs[-1]
    h = _rms_mod(x, nw_ref[...], sh_ref[...], sc_ref[...]).astype(BF16)
    dff = w1_ref.shape[1]
    acc = None
    for c0 in range(0, dff, ff_chunk):
        a = jnp.maximum(_dot(h, w1_ref[:, c0:c0 + ff_chunk]), 0.0)
        part = _dot((a * a).astype(BF16), w2_ref[c0:c0 + ff_chunk, :])
        acc = part if acc is None else acc + part
    y = x + gate_ref[...] * acc
    if final_norm:
        ms = jnp.mean(y * y, axis=-1, keepdims=True)
        y = (y * lax.rsqrt(ms + RMS_EPS)) * fnw_ref[...]
    o_ref[...] = y


def mlp(x, nw, shift, scale, gate, w1, w2, layer, tm, ff_chunk, final_nw=None, out_proj=None):
    bsz, seq, d = x.shape
    tok = lambda width: pl.BlockSpec((None, tm, width), lambda b, i: (b, i, 0))
    vec = pl.BlockSpec((None, 1, d), lambda b, i: (b, 0, 0))
    row = pl.BlockSpec((1, d), lambda b, i: (0, 0))
    layer_w = lambda w: _resident((None,) + w.shape[1:], lambda b, i: (layer, 0, 0))
    in_specs = [tok(d)]
    args = [x]
    if out_proj is not None:
        ya, yb, wo, gate_a = out_proj
        in_specs += [tok(ya.shape[-1]), tok(yb.shape[-1]), _resident(wo.shape, lambda b, i: (0, 0)), vec]
        args += [ya, yb, wo, gate_a]
    in_specs += [row, vec, vec, vec, layer_w(w1), layer_w(w2)]
    args += [nw, shift, scale, gate, w1, w2]
    if final_nw is not None:
        in_specs.append(row)
        args.append(final_nw)
    return pl.pallas_call(
        functools.partial(_mlp_kernel, ff_chunk, out_proj is not None, final_nw is not None),
        grid=(bsz, seq // tm),
        in_specs=in_specs,
        out_specs=tok(d),
        out_shape=jax.ShapeDtypeStruct((bsz, seq, d), F32),
        compiler_params=_cparams(("arbitrary", "arbitrary")),
        name="mlp_final" if final_nw is not None else "mlp",
    )(*args)


def _shortconv_kernel(n_tiles, x_ref, xp_ref, xn_ref, nw_ref, sh_ref, sc_ref, gate_ref,
                      win_ref, cw_ref, wout_ref, o_ref):
    i = pl.program_id(1)
    tm, d = x_ref.shape
    inner = wout_ref.shape[0]
    hb = SUBLANES
    x = x_ref[...]
    x_ext = jnp.concatenate([xp_ref[...], x, xn_ref[...]], axis=0)
    h = _rms_mod(x_ext, nw_ref[...], sh_ref[...], sc_ref[...]).astype(BF16)
    gate_c = _dot(h, win_ref[:, inner:2 * inner])
    val = _dot(h, win_ref[:, 2 * inner:])
    u = gate_c * val
    rows = lax.broadcasted_iota(jnp.int32, u.shape, 0)
    outside = ((rows < hb) & (i == 0)) | ((rows >= tm + hb) & (i == n_tiles - 1))
    u = jnp.where(outside, 0.0, u)
    ext = tm + 2 * hb
    conv = (pltpu.roll(u, 1, 0) * cw_ref[0:1, :] + u * cw_ref[1:2, :]
            + pltpu.roll(u, ext - 1, 0) * cw_ref[2:3, :])[hb:hb + tm, :]
    gate_b = _dot(h[hb:hb + tm, :], win_ref[:, :inner])
    y = _dot((gate_b * conv).astype(BF16), wout_ref[...])
    o_ref[...] = x + gate_ref[...] * y


def shortconv(x, nw, shift, scale, gate, w_in, conv_w, w_out, tm):
    bsz, seq, d = x.shape
    n_tiles = seq // tm
    per_tile = tm // SUBLANES
    n_halo = seq // SUBLANES
    tok = pl.BlockSpec((None, tm, d), lambda b, i: (b, i, 0))
    vec = pl.BlockSpec((None, 1, d), lambda b, i: (b, 0, 0))
    return pl.pallas_call(
        functools.partial(_shortconv_kernel, n_tiles),
        grid=(bsz, n_tiles),
        in_specs=[
            tok,
            pl.BlockSpec((None, SUBLANES, d), lambda b, i: (b, jnp.maximum(i * per_tile - 1, 0), 0)),
            pl.BlockSpec((None, SUBLANES, d), lambda b, i: (b, jnp.minimum((i + 1) * per_tile, n_halo - 1), 0)),
            pl.BlockSpec((1, d), lambda b, i: (0, 0)), vec, vec, vec,
            _resident(w_in.shape, lambda b, i: (0, 0)),
            pl.BlockSpec(conv_w.shape, lambda b, i: (0, 0)),
            _resident(w_out.shape, lambda b, i: (0, 0)),
        ],
        out_specs=tok,
        out_shape=jax.ShapeDtypeStruct((bsz, seq, d), F32),
        compiler_params=_cparams(("arbitrary", "arbitrary")),
        name="shortconv",
    )(x, x, x, nw, shift, scale, gate, w_in, conv_w, w_out)


def _pad_lanes(row):
    return jnp.zeros((1, LANES), F32).at[0, :row.shape[0]].set(row.astype(F32))


def kernel(x, c, ctx, c_ctx, mod_w, mod_b, norm_mix_w, norm_mlp_w, mlp_w1, mlp_w2, ssdna_in_w, ssdna_conv_w,
           ssdna_conv_b, ssd_dt_bias, ssd_a_log, ssd_d, ssd_norm_w, na_rpb, ssdna_out_w, sc_in_w, sc_conv_w,
           sc_out_w, final_norm_w):
    bsz, seq, d = x.shape
    n_ctx = ctx.shape[1]
    tm = min(512, seq)
    inner = SSD_HEADS * SSD_HEAD_DIM
    gn2 = SSD_GROUPS * SSD_STATE

    mrows = -(-(bsz + 1) // SUBLANES) * SUBLANES
    cc = jnp.zeros((mrows, d), F32).at[:bsz].set(c).at[bsz].set(c_ctx)
    mod = modvec(cc, mod_w, mod_b).reshape(mod_w.shape[0], mrows, 6, d)
    vecs = lambda i: [mod[i, :bsz, j].reshape(bsz, 1, d) for j in range(6)]
    row = lambda v: v.reshape(1, -1).astype(F32)

    shift_a, scale_a, gate_a, shift_f, scale_f, gate_f = vecs(0)
    shift_c = jnp.broadcast_to(mod[0, bsz, 0].reshape(1, 1, d), (bsz, 1, d))
    scale_c = jnp.broadcast_to(mod[0, bsz, 1].reshape(1, 1, d), (bsz, 1, d))

    w = ssdna_in_w[0]
    o_b, o_dt, o_k, o_v = inner, inner + gn2, inner + gn2 + 2 * SSD_HEADS, inner + gn2 + 2 * SSD_HEADS + d
    o_c = o_v + d
    o_z, o_q = o_c + gn2, o_c + gn2 + inner
    dt_pad = jnp.zeros((d, LANES - SSD_HEADS), F32)
    w0 = jnp.concatenate([
        w[:, :o_dt], w[:, o_c:o_z],
        w[:, o_z:o_q],
        w[:, o_q:] * (NA_HEAD_DIM ** -0.5),
        w[:, o_k:o_v], w[:, o_v:o_c],
        w[:, o_dt:o_dt + SSD_HEADS], dt_pad, w[:, o_dt + SSD_HEADS:o_k], dt_pad,
    ], axis=1).astype(BF16)
    nw0 = row(norm_mix_w[0])
    conv_w = ssdna_conv_w[0]
    conv_b = row(ssdna_conv_b[0])
    xs, bc, z, q_l, k_l, v_l, dt_l = inproj0(x, nw0, shift_a, scale_a, w0, conv_w, conv_b, tm)
    xs_c, bc_c, _, _, k_c, v_c, dt_c = inproj0(ctx, nw0, shift_c, scale_c, w0, conv_w, conv_b, n_ctx)

    d_skip_row = row(jnp.repeat(ssd_d[0], SSD_HEAD_DIM))
    expand = (np.arange(LANES)[:, None] == (np.arange(inner)[None, :] // SSD_HEAD_DIM)).astype(np.float32)
    expand2 = jnp.asarray(np.concatenate([expand] * SPLIT_EXPAND, axis=0), BF16)
    pad2 = lambda p: jnp.concatenate([_pad_lanes(p[0]), _pad_lanes(p[1])], axis=0)
    common = (xs, bc, dt_l, xs_c, bc_c, dt_c, pad2(ssd_dt_bias[0]), pad2(ssd_a_log[0]), expand2)
    y_fwd = ssd_pass(False, *common, d_skip_row=d_skip_row)
    y_ssd = ssd_pass(True, *common, z=z, y_prev=y_fwd, gn_w=row(ssd_norm_w[0]))

    y_na = natten(q_l, k_l, v_l, k_c, v_c, natten_bias_table(na_rpb[0]))
    w1_all, w2_all = mlp_w1.astype(BF16), mlp_w2.astype(BF16)
    x = mlp(x, row(norm_mlp_w[0]), shift_f, scale_f, gate_f, w1_all, w2_all, 0, tm, 512,
            out_proj=(y_ssd, y_na, ssdna_out_w[0].astype(BF16), gate_a))

    shift_a, scale_a, gate_a, shift_f, scale_f, gate_f = vecs(1)
    x = shortconv(x, row(norm_mix_w[1]), shift_a, scale_a, gate_a, sc_in_w[0].astype(BF16), sc_conv_w[0],
                  sc_out_w[0].astype(BF16), tm)
    x = mlp(x, row(norm_mlp_w[1]), shift_f, scale_f, gate_f, w1_all, w2_all, 1, tm, 512,
            final_nw=row(final_norm_w))
    return x
```

```python
import functools

import numpy as np
import jax
import jax.numpy as jnp
from jax import lax
from jax.experimental import pallas as pl
from jax.experimental.pallas import tpu as pltpu

F32 = jnp.float32
BF16 = jnp.bfloat16

RMS_EPS = 1e-6
MASK_VALUE = -1e30
LOG2_E = 1.4426950408889634

GRID_W = 64
SSD_HEADS = 16
SSD_HEAD_DIM = 64
SSD_GROUPS = 2
SSD_STATE = 128
SSD_CHUNK = 128
NA_HEADS = 16
NA_HEAD_DIM = 64
NA_WIN_R = 8
NA_WIN_C = 16

LANES = 128
SUBLANES = 8
VMEM_LIMIT = 56 * 1024 * 1024


def _cparams(semantics):
    return pltpu.CompilerParams(dimension_semantics=semantics, vmem_limit_bytes=VMEM_LIMIT)


def _resident(block_shape, index_map):
    return pl.BlockSpec(block_shape, index_map, pipeline_mode=pl.Buffered(1))


def _rms_mod(x, nw, shift, scale):
    ms = jnp.mean(x * x, axis=-1, keepdims=True)
    return (x * lax.rsqrt(ms + RMS_EPS)) * nw * (1.0 + scale) + shift


def _silu(x):
    return x * jax.nn.sigmoid(x)


def _softplus(x):
    return jnp.maximum(x, 0.0) + jnp.log1p(jnp.exp(-jnp.abs(x)))


def _dot(a, b):
    return jnp.dot(a, b, preferred_element_type=F32)


def _dot_nt(a, b):
    return lax.dot_general(a, b, (((1,), (1,)), ((), ())), preferred_element_type=F32)


def _split_bf16(a, parts):
    out = []
    r = a
    for _ in range(parts):
        h = r.astype(BF16)
        out.append(h)
        r = r - h.astype(F32)
    return out


def _modvec_kernel(c_ref, w_ref, b_ref, o_ref):
    s = _silu(c_ref[...]).astype(BF16)
    o_ref[...] = _dot(s, w_ref[...].astype(BF16)) + b_ref[...]


def modvec(cc, mod_w, mod_b, tn=1536):
    depth, d, n = mod_w.shape
    rows = cc.shape[0]
    return pl.pallas_call(
        _modvec_kernel,
        grid=(depth, n // tn),
        in_specs=[
            pl.BlockSpec((rows, d), lambda i, j: (0, 0)),
            pl.BlockSpec((None, d, tn), lambda i, j: (i, 0, j)),
            pl.BlockSpec((None, 1, tn), lambda i, j: (i, 0, j)),
        ],
        out_specs=pl.BlockSpec((None, rows, tn), lambda i, j: (i, 0, j)),
        out_shape=jax.ShapeDtypeStruct((depth, rows, n), F32),
        compiler_params=_cparams(("arbitrary", "arbitrary")),
        name="modvec",
    )(cc, mod_w, mod_b.reshape(depth, 1, n))


SSD_INNER = SSD_HEADS * SSD_HEAD_DIM
SSD_BC = 2 * SSD_GROUPS * SSD_STATE
XBC_COLS = SSD_INNER + SSD_BC
DT_COLS = 2 * LANES


def _inproj0_kernel(n_tiles, x_ref, xp_ref, xn_ref, nw_ref, sh_ref, sc_ref, w_ref, cw_ref, cb_ref,
                    xs_ref, bc_ref, z_ref, q_ref, k_ref, v_ref, dt_ref):
    i = pl.program_id(1)
    tm, d = x_ref.shape
    hb = SUBLANES
    ext = tm + 2 * hb
    x_ext = jnp.concatenate([xp_ref[...], x_ref[...], xn_ref[...]], axis=0)
    h_ext = _rms_mod(x_ext, nw_ref[...], sh_ref[...], sc_ref[...]).astype(BF16)
    u = _dot(h_ext, w_ref[:, :XBC_COLS])
    rows = lax.broadcasted_iota(jnp.int32, u.shape, 0)
    outside = ((rows < hb) & (i == 0)) | ((rows >= tm + hb) & (i == n_tiles - 1))
    u = jnp.where(outside, 0.0, u)
    xc = (pltpu.roll(u, 1, 0) * cw_ref[0:1, :] + u * cw_ref[1:2, :]
          + pltpu.roll(u, ext - 1, 0) * cw_ref[2:3, :])[hb:hb + tm, :] + cb_ref[...]
    xc = _silu(xc)
    xs_ref[...] = xc[:, :SSD_INNER]
    bc_ref[...] = xc[:, SSD_INNER:].astype(bc_ref.dtype)

    h = h_ext[hb:hb + tm, :]
    c0 = XBC_COLS
    for ref, width in ((z_ref, d), (q_ref, d), (k_ref, d), (v_ref, d), (dt_ref, DT_COLS)):
        y = _dot(h, w_ref[:, c0:c0 + width]).astype(ref.dtype)
        if len(ref.shape) == 3:
            for p in range(ref.shape[0]):
                ref[p] = y[:, p * LANES:(p + 1) * LANES]
        else:
            ref[...] = y
        c0 += width


def inproj0(x, nw, shift, scale, w, conv_w, conv_b, tm):
    bsz, seq, d = x.shape
    ncols = w.shape[1]
    npair = d // LANES
    n_tiles = seq // tm
    per_tile = tm // SUBLANES
    n_halo = seq // SUBLANES
    tok = lambda width: pl.BlockSpec((None, tm, width), lambda b, i: (b, i, 0))
    pair_major = pl.BlockSpec((None, npair, tm, LANES), lambda b, i: (b, 0, i, 0))
    vec = pl.BlockSpec((None, 1, d), lambda b, i: (b, 0, 0))
    const = lambda shape: pl.BlockSpec(shape, lambda b, i: (0,) * len(shape))
    out_shapes = (
        jax.ShapeDtypeStruct((bsz, seq, SSD_INNER), F32),
        jax.ShapeDtypeStruct((bsz, seq, SSD_BC), BF16),
        jax.ShapeDtypeStruct((bsz, seq, d), F32),
        jax.ShapeDtypeStruct((bsz, npair, seq, LANES), BF16),
        jax.ShapeDtypeStruct((bsz, npair, seq, LANES), BF16),
        jax.ShapeDtypeStruct((bsz, seq, d), BF16),
        jax.ShapeDtypeStruct((bsz, seq, DT_COLS), F32),
    )
    return pl.pallas_call(
        functools.partial(_inproj0_kernel, n_tiles),
        grid=(bsz, n_tiles),
        in_specs=[tok(d),
                  pl.BlockSpec((None, SUBLANES, d), lambda b, i: (b, jnp.maximum(i * per_tile - 1, 0), 0)),
                  pl.BlockSpec((None, SUBLANES, d),
                               lambda b, i: (b, jnp.minimum((i + 1) * per_tile, n_halo - 1), 0)),
                  const((1, d)), vec, vec,
                  _resident((d, ncols), lambda b, i: (0, 0)),
                  const((3, XBC_COLS)), const((1, XBC_COLS))],
        out_specs=(tok(SSD_INNER), tok(SSD_BC), tok(d), pair_major, pair_major, tok(d), tok(DT_COLS)),
        out_shape=out_shapes,
        compiler_params=_cparams(("arbitrary", "arbitrary")),
        name="inproj0",
    )(x, x, x, nw, shift, scale, w, conv_w, conv_b)


SSD_CHUNKS_PER_STEP = 2
SPLIT_CUMSUM = 3
SPLIT_EXPAND = 2


def _ssd_kernel(backward, n_ctx_steps, n_lat_steps, *refs):
    if backward:
        (xs_ref, bc_ref, dt_ref, xsc_ref, bcc_ref, dtc_ref, dtb_ref, alog_ref, e_ref,
         yprev_ref, y_ref, state_ref) = refs
    else:
        (xs_ref, bc_ref, dt_ref, xsc_ref, bcc_ref, dtc_ref, dtb_ref, alog_ref, e_ref,
         dsk_ref, y_ref, state_ref) = refs
    q = SSD_CHUNK
    nch = SSD_CHUNKS_PER_STEP
    inner = SSD_INNER
    gn = SSD_STATE
    gcols = inner // SSD_GROUPS
    heads_per_group = SSD_HEADS // SSD_GROUPS
    c = pl.program_id(1)
    scan_dir = 1 if backward else 0

    ri = lax.broadcasted_iota(jnp.int32, (q, q), 0)
    ci = lax.broadcasted_iota(jnp.int32, (q, q), 1)
    contributes = (ri >= ci, ri <= ci)
    tri = [jnp.concatenate([jnp.where(m, 1.0, 0.0).astype(BF16)] * SPLIT_CUMSUM, axis=1) for m in contributes]
    lane = lax.broadcasted_iota(jnp.int32, (q, LANES), 1)
    low_half = lane < SSD_HEAD_DIM

    def decay_terms(dtraw, direction):
        cols = slice(direction * LANES, (direction + 1) * LANES)
        dt = _softplus(dtraw[:, cols] + dtb_ref[direction:direction + 1, :])
        a = dt * (-jnp.exp(alog_ref[direction:direction + 1, :]))
        a_cum = _dot(tri[direction], jnp.concatenate(_split_bf16(a, SPLIT_CUMSUM), axis=0))
        return dt, a_cum

    def expand(v):
        return _dot(jnp.concatenate(_split_bf16(v, SPLIT_EXPAND), axis=1), e_ref[...])

    def process(x, bc16, dtraw, out_rows):
        want_y = out_rows is not None
        bm16 = bc16[:, :SSD_GROUPS * gn]
        cm16 = bc16[:, SSD_GROUPS * gn:]

        dt, a_cum = decay_terms(dtraw, scan_dir)
        edge = a_cum[0:1, :] if backward else a_cum[q - 1:q, :]
        ea_full = expand(jnp.exp(a_cum))
        dtw_full = expand(dt * jnp.exp(edge - a_cum))
        chunk_decay = ea_full[0:1, :] if backward else ea_full[q - 1:q, :]
        xw = (x * dtw_full).astype(BF16)
        bts = [bm16[:, g * gn:(g + 1) * gn].astype(F32).T.astype(BF16) for g in range(SSD_GROUPS)]

        y_part = None
        if want_y and backward:
            y_part = yprev_ref[out_rows, :]
        elif want_y:
            x16 = x.astype(BF16)
            dirs = [(d_, ac * LOG2_E) for d_, ac in ((dt, a_cum), decay_terms(dtraw, 1))]
            tr = [(d_.T, ac.T) for d_, ac in dirs]
            ys = []
            for g in range(SSD_GROUPS):
                cb = _dot_nt(cm16[:, g * gn:(g + 1) * gn], bm16[:, g * gn:(g + 1) * gn])
                for pair in range(heads_per_group // 2):
                    ms = []
                    for hh in range(2):
                        h = g * heads_per_group + pair * 2 + hh
                        w = None
                        for direction in range(2):
                            ac, (dt_t, ac_t) = dirs[direction][1], tr[direction]
                            seg = ac[:, h:h + 1] - ac_t[h:h + 1, :]
                            term = jnp.exp2(jnp.where(contributes[direction], seg, -jnp.inf)) * dt_t[h:h + 1, :]
                            w = term if w is None else w + term
                        ms.append((cb * w).astype(BF16))
                    m_pair = jnp.concatenate(ms, axis=1)
                    col0 = (g * heads_per_group + pair * 2) * SSD_HEAD_DIM
                    xp = x16[:, col0:col0 + LANES]
                    zero = jnp.zeros_like(xp)
                    x_bd = jnp.concatenate([jnp.where(low_half, xp, zero),
                                            jnp.where(low_half, zero, xp)], axis=0)
                    ys.append(_dot(m_pair, x_bd))
            y_part = jnp.concatenate(ys, axis=1) + x * dsk_ref[...]

        def finish():
            if want_y:
                y_off = jnp.concatenate(
                    [_dot(cm16[:, g * gn:(g + 1) * gn], state_ref[:, g * gcols:(g + 1) * gcols].astype(BF16))
                     for g in range(SSD_GROUPS)], axis=1) * ea_full
                y_ref[out_rows, :] = y_part + y_off
            for g in range(SSD_GROUPS):
                sl = slice(g * gcols, (g + 1) * gcols)
                state_ref[:, sl] = state_ref[:, sl] * chunk_decay[:, sl] + _dot(bts[g], xw[:, sl])

        return finish

    def run_block(x_ref_, bc_ref_, dt_ref_, r0, is_latent):
        order = range(nch - 1, -1, -1) if backward else range(nch)
        finishers = []
        for k in order:
            rows = slice(r0 + k * q, r0 + (k + 1) * q)
            out_rows = slice(k * q, (k + 1) * q) if is_latent else None
            finishers.append(process(x_ref_[rows, :], bc_ref_[rows, :], dt_ref_[rows, :], out_rows))
        for fin in finishers:
            fin()

    @pl.when(c == 0)
    def _():
        state_ref[...] = jnp.zeros_like(state_ref)

    blk = nch * q
    for step in range(n_ctx_steps):
        cs = (n_ctx_steps - 1 - step) if backward else step

        @pl.when(c == step)
        def _(cs=cs):
            run_block(xsc_ref, bcc_ref, dtc_ref, cs * blk, False)

    @pl.when(c >= n_ctx_steps)
    def _():
        run_block(xs_ref, bc_ref, dt_ref, 0, True)


def ssd_pass(backward, xs, bc, dt, xs_c, bc_c, dt_c, dt_bias_rows, a_log_rows, expand2,
             d_skip_row=None, y_prev=None):
    bsz, seq, inner = xs.shape
    n_ctx = xs_c.shape[1]
    blk = SSD_CHUNK * SSD_CHUNKS_PER_STEP
    assert n_ctx % blk == 0 and seq % blk == 0
    ncs, nls = n_ctx // blk, seq // blk

    def lat(c):
        ls = jnp.maximum(c - ncs, 0)
        return (nls - 1 - ls) if backward else ls

    block = lambda width: pl.BlockSpec((None, blk, width), lambda b, c: (b, lat(c), 0))
    whole = lambda width: pl.BlockSpec((None, n_ctx, width), lambda b, c: (b, 0, 0))
    const = lambda shape: pl.BlockSpec(shape, lambda b, c: (0,) * len(shape))
    in_specs = [
        block(inner), block(SSD_BC), block(DT_COLS),
        whole(inner), whole(SSD_BC), whole(DT_COLS),
        const((2, LANES)), const((2, LANES)), const(expand2.shape),
    ]
    args = [xs, bc, dt, xs_c, bc_c, dt_c, dt_bias_rows, a_log_rows, expand2]
    if backward:
        in_specs += [block(inner)]
        args += [y_prev]
    else:
        in_specs += [const((1, inner))]
        args += [d_skip_row]
    return pl.pallas_call(
        functools.partial(_ssd_kernel, backward, ncs, nls),
        grid=(bsz, ncs + nls),
        in_specs=in_specs,
        out_specs=block(inner),
        out_shape=jax.ShapeDtypeStruct((bsz, seq, inner), F32),
        scratch_shapes=[pltpu.VMEM((SSD_STATE, inner), F32)],
        compiler_params=_cparams(("arbitrary", "arbitrary")),
        name="ssd_bwd" if backward else "ssd_fwd",
    )(*args)


NA_GROUP_HEADS = 4
NA_ROWS_PER_STEP = 4


def _natten_kernel(grid_rows, q_ref, k_ref, v_ref, kc_ref, vc_ref, bias_ref, o_ref):
    w = GRID_W
    nwin = NA_WIN_R * w
    hg = NA_GROUP_HEADS
    gd = hg * NA_HEAD_DIM
    ngroups = NA_HEADS // hg
    nrows = NA_ROWS_PER_STEP
    lane_head = lax.broadcasted_iota(jnp.int32, (w, gd), 1) // NA_HEAD_DIM
    low_half = lax.broadcasted_iota(jnp.int32, (w, LANES), 1) < NA_HEAD_DIM
    r_base = pl.program_id(1) * nrows
    rs = [jnp.clip(r_base + i - NA_WIN_R // 2, 0, grid_rows - NA_WIN_R) for i in range(nrows)]
    k0 = [pl.multiple_of(rs[i] * w, w) for i in range(nrows)]
    dr0 = [rs[i] - (r_base + i) + NA_WIN_R - 1 for i in range(nrows)]

    def scores(g):
        s_win = [[] for _ in range(nrows)]
        s_ctx = [[] for _ in range(nrows)]
        for pp in range(hg // 2):
            pair = g * (hg // 2) + pp
            wqs = []
            for i in range(nrows):
                qp = q_ref[pair, i * w:(i + 1) * w, :]
                zero = jnp.zeros_like(qp)
                wqs.append(jnp.concatenate([jnp.where(low_half, qp, zero), jnp.where(low_half, zero, qp)], axis=0))
            sc = _dot_nt(jnp.concatenate(wqs, axis=0), kc_ref[pair])
            for i in range(nrows):
                s_win[i].append(_dot_nt(wqs[i], k_ref[pair, pl.ds(k0[i], nwin), :]))
                s_ctx[i].append(sc[i * 2 * w:(i + 1) * 2 * w, :])
        return [(jnp.concatenate(s_win[i], axis=0), jnp.concatenate(s_ctx[i], axis=0)) for i in range(nrows)]

    def softmax(g, i, s):
        s_win, s_ctx = s
        s_win = jnp.concatenate(
            [s_win[:, t * LANES:(t + 1) * LANES] + bias_ref[g, dr0[i] + 2 * t]
             for t in range(nwin // LANES)], axis=1)
        m = jnp.maximum(jnp.max(s_win, axis=-1, keepdims=True), jnp.max(s_ctx, axis=-1, keepdims=True))
        p_win = jnp.exp(s_win - m)
        p_ctx = jnp.exp(s_ctx - m)
        denom = jnp.sum(p_win, axis=-1, keepdims=True) + jnp.sum(p_ctx, axis=-1, keepdims=True)
        return p_win.astype(BF16), p_ctx.astype(BF16), denom

    def values(g, ps):
        cols = slice(g * gd, (g + 1) * gd)
        o_ctx = _dot(jnp.concatenate([p[1] for p in ps], axis=0), vc_ref[:, cols])
        for i in range(nrows):
            p_win, _, denom = ps[i]
            o = (_dot(p_win, v_ref[pl.ds(k0[i], nwin), cols]) + o_ctx[i * hg * w:(i + 1) * hg * w, :]) / denom
            acc = o[:w, :]
            for hh in range(1, hg):
                acc = jnp.where(lane_head == hh, o[hh * w:(hh + 1) * w, :], acc)
            o_ref[i * w:(i + 1) * w, cols] = acc.astype(o_ref.dtype)

    s, p = {}, {}
    for step in range(ngroups + 2):
        if step < ngroups:
            s[step] = scores(step)
        if 0 <= step - 1 < ngroups:
            sg = s.pop(step - 1)
            p[step - 1] = [softmax(step - 1, i, sg[i]) for i in range(nrows)]
        if 0 <= step - 2 < ngroups:
            values(step - 2, p.pop(step - 2))


def natten(q, k, v, kc, vc, bias2):
    bsz, seq, d = v.shape
    n_ctx = vc.shape[1]
    grid_rows = seq // GRID_W
    npair = NA_HEADS // 2
    rows_tok = NA_ROWS_PER_STEP * GRID_W
    return pl.pallas_call(
        functools.partial(_natten_kernel, grid_rows),
        grid=(bsz, grid_rows // NA_ROWS_PER_STEP),
        in_specs=[
            pl.BlockSpec((None, npair, rows_tok, LANES), lambda b, r: (b, 0, r, 0)),
            pl.BlockSpec((None, npair, seq, LANES), lambda b, r: (b, 0, 0, 0)),
            pl.BlockSpec((None, seq, d), lambda b, r: (b, 0, 0)),
            _resident((None, npair, n_ctx, LANES), lambda b, r: (b, 0, 0, 0)),
            _resident((None, n_ctx, d), lambda b, r: (b, 0, 0)),
            _resident(bias2.shape, lambda b, r: (0, 0, 0, 0)),
        ],
        out_specs=pl.BlockSpec((None, rows_tok, d), lambda b, r: (b, r, 0)),
        out_shape=jax.ShapeDtypeStruct((bsz, seq, d), BF16),
        compiler_params=_cparams(("arbitrary", "arbitrary")),
        name="natten",
    )(q, k, v, kc, vc, bias2)


def natten_bias_table(rpb):
    w = GRID_W
    hg = NA_GROUP_HEADS
    qc = np.arange(w)[:, None]
    kc = np.arange(w)[None, :]
    win_start = np.clip(qc - NA_WIN_C // 2, 0, w - NA_WIN_C)
    col_ok = (kc >= win_start) & (kc < win_start + NA_WIN_C)
    dc_idx = np.clip(kc - qc, -(NA_WIN_C - 1), NA_WIN_C - 1) + NA_WIN_C - 1
    ndc = 2 * NA_WIN_C - 1
    onehot = (np.arange(ndc)[:, None, None] == dc_idx[None]).astype(np.float32)
    t = jnp.einsum("hrc,cqk->hrqk", rpb.astype(F32), jnp.asarray(onehot), precision=lax.Precision.HIGHEST)
    t = jnp.where(col_ok[None, None], t, MASK_VALUE)
    ndr = 2 * NA_WIN_R - 2
    t2 = jnp.concatenate([t[:, :ndr], t[:, 1:ndr + 1]], axis=-1)
    t2 = t2.reshape(NA_HEADS // hg, hg, ndr, w, 2 * w).transpose(0, 2, 1, 3, 4)
    return t2.reshape(NA_HEADS // hg, ndr, hg * w, 2 * w)


def _mlp_kernel(ff_chunk, out_proj, final_norm, *refs):
    refs = list(refs)
    x = refs.pop(0)[...]
    if out_proj:
        ys_ref, z_ref, gnw_ref, yb_ref, wo_ref, gate_a_ref = refs[:6]
        del refs[:6]
        ka = ys_ref.shape[-1]
        ys = ys_ref[...] * _silu(z_ref[...])
        ys = (ys * lax.rsqrt(jnp.mean(ys * ys, axis=-1, keepdims=True) + RMS_EPS)) * gnw_ref[...]
        x = x + gate_a_ref[...] * (_dot(ys.astype(BF16), wo_ref[:ka, :]) + _dot(yb_ref[...], wo_ref[ka:, :]))
    nw_ref, sh_ref, sc_ref, gate_ref, w1_ref, w2_ref = refs[:6]
    fnw_ref = refs[6] if final_norm else None
    o_ref = refs[-1]
    h = _rms_mod(x, nw_ref[...], sh_ref[...], sc_ref[...]).astype(BF16)
    dff = w1_ref.shape[1]
    acc = None
    for c0 in range(0, dff, ff_chunk):
        a = jnp.maximum(_dot(h, w1_ref[:, c0:c0 + ff_chunk]), 0.0)
        part = _dot((a * a).astype(BF16), w2_ref[c0:c0 + ff_chunk, :])
        acc = part if acc is None else acc + part
    y = x + gate_ref[...] * acc
    if final_norm:
        ms = jnp.mean(y * y, axis=-1, keepdims=True)
        y = (y * lax.rsqrt(ms + RMS_EPS)) * fnw_ref[...]
    o_ref[...] = y


def mlp(x, nw, shift, scale, gate, w1, w2, layer, tm, ff_chunk, final_nw=None, out_proj=None):
    bsz, seq, d = x.shape
    tok = lambda width: pl.BlockSpec((None, tm, width), lambda b, i: (b, i, 0))
    vec = pl.BlockSpec((None, 1, d), lambda b, i: (b, 0, 0))
    row = pl.BlockSpec((1, d), lambda b, i: (0, 0))
    layer_w = lambda w: _resident((None,) + w.shape[1:], lambda b, i: (layer, 0, 0))
    in_specs = [tok(d)]
    args = [x]
    if out_proj is not None:
        ys, z, gnw, yb, wo, gate_a = out_proj
        in_specs += [tok(ys.shape[-1]), tok(z.shape[-1]), pl.BlockSpec(gnw.shape, lambda b, i: (0, 0)),
                     tok(yb.shape[-1]), _resident(wo.shape, lambda b, i: (0, 0)), vec]
        args += [ys, z, gnw, yb, wo, gate_a]
    in_specs += [row, vec, vec, vec, layer_w(w1), layer_w(w2)]
    args += [nw, shift, scale, gate, w1, w2]
    if final_nw is not None:
        in_specs.append(row)
        args.append(final_nw)
    return pl.pallas_call(
        functools.partial(_mlp_kernel, ff_chunk, out_proj is not None, final_nw is not None),
        grid=(bsz, seq // tm),
        in_specs=in_specs,
        out_specs=tok(d),
        out_shape=jax.ShapeDtypeStruct((bsz, seq, d), F32),
        compiler_params=_cparams(("arbitrary", "arbitrary")),
        name="mlp_final" if final_nw is not None else "mlp",
    )(*args)


def _shortconv_kernel(n_tiles, x_ref, xp_ref, xn_ref, nw_ref, sh_ref, sc_ref, gate_ref,
                      win_ref, cw_ref, wout_ref, o_ref):
    i = pl.program_id(1)
    tm, d = x_ref.shape
    inner = wout_ref.shape[0]
    hb = SUBLANES
    x = x_ref[...]
    x_ext = jnp.concatenate([xp_ref[...], x, xn_ref[...]], axis=0)
    h = _rms_mod(x_ext, nw_ref[...], sh_ref[...], sc_ref[...]).astype(BF16)
    gate_c = _dot(h, win_ref[:, inner:2 * inner])
    val = _dot(h, win_ref[:, 2 * inner:])
    u = gate_c * val
    rows = lax.broadcasted_iota(jnp.int32, u.shape, 0)
    outside = ((rows < hb) & (i == 0)) | ((rows >= tm + hb) & (i == n_tiles - 1))
    u = jnp.where(outside, 0.0, u)
    ext = tm + 2 * hb
    conv = (pltpu.roll(u, 1, 0) * cw_ref[0:1, :] + u * cw_ref[1:2, :]
            + pltpu.roll(u, ext - 1, 0) * cw_ref[2:3, :])[hb:hb + tm, :]
    gate_b = _dot(h[hb:hb + tm, :], win_ref[:, :inner])
    y = _dot((gate_b * conv).astype(BF16), wout_ref[...])
    o_ref[...] = x + gate_ref[...] * y


def shortconv(x, nw, shift, scale, gate, w_in, conv_w, w_out, tm):
    bsz, seq, d = x.shape
    n_tiles = seq // tm
    per_tile = tm // SUBLANES
    n_halo = seq // SUBLANES
    tok = pl.BlockSpec((None, tm, d), lambda b, i: (b, i, 0))
    vec = pl.BlockSpec((None, 1, d), lambda b, i: (b, 0, 0))
    return pl.pallas_call(
        functools.partial(_shortconv_kernel, n_tiles),
        grid=(bsz, n_tiles),
        in_specs=[
            tok,
            pl.BlockSpec((None, SUBLANES, d), lambda b, i: (b, jnp.maximum(i * per_tile - 1, 0), 0)),
            pl.BlockSpec((None, SUBLANES, d), lambda b, i: (b, jnp.minimum((i + 1) * per_tile, n_halo - 1), 0)),
            pl.BlockSpec((1, d), lambda b, i: (0, 0)), vec, vec, vec,
            _resident(w_in.shape, lambda b, i: (0, 0)),
            pl.BlockSpec(conv_w.shape, lambda b, i: (0, 0)),
            _resident(w_out.shape, lambda b, i: (0, 0)),
        ],
        out_specs=tok,
        out_shape=jax.ShapeDtypeStruct((bsz, seq, d), F32),
        compiler_params=_cparams(("arbitrary", "arbitrary")),
        name="shortconv",
    )(x, x, x, nw, shift, scale, gate, w_in, conv_w, w_out)


def _pad_lanes(row):
    return jnp.zeros((1, LANES), F32).at[0, :row.shape[0]].set(row.astype(F32))


def kernel(x, c, ctx, c_ctx, mod_w, mod_b, norm_mix_w, norm_mlp_w, mlp_w1, mlp_w2, ssdna_in_w, ssdna_conv_w,
           ssdna_conv_b, ssd_dt_bias, ssd_a_log, ssd_d, ssd_norm_w, na_rpb, ssdna_out_w, sc_in_w, sc_conv_w,
           sc_out_w, final_norm_w):
    bsz, seq, d = x.shape
    n_ctx = ctx.shape[1]
    tm = min(512, seq)
    inner = SSD_HEADS * SSD_HEAD_DIM
    gn2 = SSD_GROUPS * SSD_STATE

    mrows = -(-(bsz + 1) // SUBLANES) * SUBLANES
    cc = jnp.zeros((mrows, d), F32).at[:bsz].set(c).at[bsz].set(c_ctx)
    mod = modvec(cc, mod_w, mod_b).reshape(mod_w.shape[0], mrows, 6, d)
    vecs = lambda i: [mod[i, :bsz, j].reshape(bsz, 1, d) for j in range(6)]
    row = lambda v: v.reshape(1, -1).astype(F32)

    shift_a, scale_a, gate_a, shift_f, scale_f, gate_f = vecs(0)
    shift_c = jnp.broadcast_to(mod[0, bsz, 0].reshape(1, 1, d), (bsz, 1, d))
    scale_c = jnp.broadcast_to(mod[0, bsz, 1].reshape(1, 1, d), (bsz, 1, d))

    w = ssdna_in_w[0]
    o_b, o_dt, o_k, o_v = inner, inner + gn2, inner + gn2 + 2 * SSD_HEADS, inner + gn2 + 2 * SSD_HEADS + d
    o_c = o_v + d
    o_z, o_q = o_c + gn2, o_c + gn2 + inner
    dt_pad = jnp.zeros((d, LANES - SSD_HEADS), F32)
    w0 = jnp.concatenate([
        w[:, :o_dt], w[:, o_c:o_z],
        w[:, o_z:o_q],
        w[:, o_q:] * (NA_HEAD_DIM ** -0.5),
        w[:, o_k:o_v], w[:, o_v:o_c],
        w[:, o_dt:o_dt + SSD_HEADS], dt_pad, w[:, o_dt + SSD_HEADS:o_k], dt_pad,
    ], axis=1).astype(BF16)
    nw0 = row(norm_mix_w[0])
    conv_w = ssdna_conv_w[0]
    conv_b = row(ssdna_conv_b[0])
    xs, bc, z, q_l, k_l, v_l, dt_l = inproj0(x, nw0, shift_a, scale_a, w0, conv_w, conv_b, tm)
    xs_c, bc_c, _, _, k_c, v_c, dt_c = inproj0(ctx, nw0, shift_c, scale_c, w0, conv_w, conv_b, n_ctx)

    d_skip_row = row(jnp.repeat(ssd_d[0], SSD_HEAD_DIM))
    expand = (np.arange(LANES)[:, None] == (np.arange(inner)[None, :] // SSD_HEAD_DIM)).astype(np.float32)
    expand2 = jnp.asarray(np.concatenate([expand] * SPLIT_EXPAND, axis=0), BF16)
    pad2 = lambda p: jnp.concatenate([_pad_lanes(p[0]), _pad_lanes(p[1])], axis=0)
    common = (xs, bc, dt_l, xs_c, bc_c, dt_c, pad2(ssd_dt_bias[0]), pad2(ssd_a_log[0]), expand2)
    y_fwd = ssd_pass(False, *common, d_skip_row=d_skip_row)
    y_ssd = ssd_pass(True, *common, y_prev=y_fwd)

    y_na = natten(q_l, k_l, v_l, k_c, v_c, natten_bias_table(na_rpb[0]))
    w1_all, w2_all = mlp_w1.astype(BF16), mlp_w2.astype(BF16)
    x = mlp(x, row(norm_mlp_w[0]), shift_f, scale_f, gate_f, w1_all, w2_all, 0, tm, 512,
            out_proj=(y_ssd, z, row(ssd_norm_w[0]), y_na, ssdna_out_w[0].astype(BF16), gate_a))

    shift_a, scale_a, gate_a, shift_f, scale_f, gate_f = vecs(1)
    x = shortconv(x, row(norm_mix_w[1]), shift_a, scale_a, gate_a, sc_in_w[0].astype(BF16), sc_conv_w[0],
                  sc_out_w[0].astype(BF16), tm)
    x = mlp(x, row(norm_mlp_w[1]), shift_f, scale_f, gate_f, w1_all, w2_all, 1, tm, 512,
            final_nw=row(final_norm_w))
    return x
```

```python
import functools

import numpy as np
import jax
import jax.numpy as jnp
from jax import lax
from jax.experimental import pallas as pl
from jax.experimental.pallas import tpu as pltpu

F32 = jnp.float32
BF16 = jnp.bfloat16

RMS_EPS = 1e-6
MASK_VALUE = -1e30
LOG2_E = 1.4426950408889634

GRID_W = 64
SSD_HEADS = 16
SSD_HEAD_DIM = 64
SSD_GROUPS = 2
SSD_STATE = 128
SSD_CHUNK = 128
NA_HEADS = 16
NA_HEAD_DIM = 64
NA_WIN_R = 8
NA_WIN_C = 16

LANES = 128
SUBLANES = 8
VMEM_LIMIT = 56 * 1024 * 1024


def _cparams(semantics):
    return pltpu.CompilerParams(dimension_semantics=semantics, vmem_limit_bytes=VMEM_LIMIT)


def _resident(block_shape, index_map):
    return pl.BlockSpec(block_shape, index_map, pipeline_mode=pl.Buffered(1))


def _rms_mod(x, nw, shift, scale):
    ms = jnp.mean(x * x, axis=-1, keepdims=True)
    return (x * lax.rsqrt(ms + RMS_EPS)) * nw * (1.0 + scale) + shift


def _silu(x):
    return x * jax.nn.sigmoid(x)


def _softplus(x):
    return jnp.maximum(x, 0.0) + jnp.log1p(jnp.exp(-jnp.abs(x)))


def _dot(a, b):
    return jnp.dot(a, b, preferred_element_type=F32)


def _dot_nt(a, b):
    return lax.dot_general(a, b, (((1,), (1,)), ((), ())), preferred_element_type=F32)


def _split_bf16(a, parts):
    out = []
    r = a
    for _ in range(parts):
        h = r.astype(BF16)
        out.append(h)
        r = r - h.astype(F32)
    return out


def _modvec_kernel(c_ref, w_ref, b_ref, o_ref):
    s = _silu(c_ref[...]).astype(BF16)
    o_ref[...] = _dot(s, w_ref[...].astype(BF16)) + b_ref[...]


def modvec(cc, mod_w, mod_b, tn=1536):
    depth, d, n = mod_w.shape
    rows = cc.shape[0]
    return pl.pallas_call(
        _modvec_kernel,
        grid=(depth, n // tn),
        in_specs=[
            pl.BlockSpec((rows, d), lambda i, j: (0, 0)),
            pl.BlockSpec((None, d, tn), lambda i, j: (i, 0, j)),
            pl.BlockSpec((None, 1, tn), lambda i, j: (i, 0, j)),
        ],
        out_specs=pl.BlockSpec((None, rows, tn), lambda i, j: (i, 0, j)),
        out_shape=jax.ShapeDtypeStruct((depth, rows, n), F32),
        compiler_params=_cparams(("arbitrary", "arbitrary")),
        name="modvec",
    )(cc, mod_w, mod_b.reshape(depth, 1, n))


def _reorder_columns_kernel(segments, w_ref, o_ref):
    dst = 0
    for src, width, scale in segments:
        pad = -width % LANES
        v = w_ref[:, src:src + width]
        if scale != 1.0:
            v = v * scale
        o_ref[:, dst:dst + width] = v.astype(o_ref.dtype)
        if pad:
            o_ref[:, dst + width:dst + width + pad] = jnp.zeros((o_ref.shape[0], pad), o_ref.dtype)
        dst += width + pad


def reorder_columns(w, segments, rows_per_step=128):
    k, n = w.shape
    n_out = sum(width + (-width % LANES) for _, width, _ in segments)
    return pl.pallas_call(
        functools.partial(_reorder_columns_kernel, tuple(segments)),
        grid=(k // rows_per_step,),
        in_specs=[pl.BlockSpec((rows_per_step, n), lambda i: (i, 0))],
        out_specs=pl.BlockSpec((rows_per_step, n_out), lambda i: (i, 0)),
        out_shape=jax.ShapeDtypeStruct((k, n_out), BF16),
        compiler_params=_cparams(("arbitrary",)),
        name="reorder_columns",
    )(w)


SSD_INNER = SSD_HEADS * SSD_HEAD_DIM
SSD_BC = 2 * SSD_GROUPS * SSD_STATE
XBC_COLS = SSD_INNER + SSD_BC
DT_COLS = 2 * LANES


def _inproj0_kernel(n_tiles, x_ref, xp_ref, xn_ref, nw_ref, sh_ref, sc_ref, w_ref, cw_ref, cb_ref,
                    xs_ref, bc_ref, z_ref, q_ref, k_ref, v_ref, dt_ref):
    i = pl.program_id(1)
    tm, d = x_ref.shape
    hb = SUBLANES
    ext = tm + 2 * hb
    x_ext = jnp.concatenate([xp_ref[...], x_ref[...], xn_ref[...]], axis=0)
    h_ext = _rms_mod(x_ext, nw_ref[...], sh_ref[...], sc_ref[...]).astype(BF16)
    u = _dot(h_ext, w_ref[:, :XBC_COLS])
    rows = lax.broadcasted_iota(jnp.int32, u.shape, 0)
    outside = ((rows < hb) & (i == 0)) | ((rows >= tm + hb) & (i == n_tiles - 1))
    u = jnp.where(outside, 0.0, u)
    xc = (pltpu.roll(u, 1, 0) * cw_ref[0:1, :] + u * cw_ref[1:2, :]
          + pltpu.roll(u, ext - 1, 0) * cw_ref[2:3, :])[hb:hb + tm, :] + cb_ref[...]
    xc = _silu(xc)
    xs_ref[...] = xc[:, :SSD_INNER]
    bc_ref[...] = xc[:, SSD_INNER:].astype(bc_ref.dtype)

    h = h_ext[hb:hb + tm, :]
    c0 = XBC_COLS
    for ref, width in ((z_ref, d), (q_ref, d), (k_ref, d), (v_ref, d), (dt_ref, DT_COLS)):
        y = _dot(h, w_ref[:, c0:c0 + width]).astype(ref.dtype)
        if len(ref.shape) == 3:
            for p in range(ref.shape[0]):
                ref[p] = y[:, p * LANES:(p + 1) * LANES]
        else:
            ref[...] = y
        c0 += width


def inproj0(x, nw, shift, scale, w, conv_w, conv_b, tm):
    bsz, seq, d = x.shape
    ncols = w.shape[1]
    npair = d // LANES
    n_tiles = seq // tm
    per_tile = tm // SUBLANES
    n_halo = seq // SUBLANES
    tok = lambda width: pl.BlockSpec((None, tm, width), lambda b, i: (b, i, 0))
    pair_major = pl.BlockSpec((None, npair, tm, LANES), lambda b, i: (b, 0, i, 0))
    vec = pl.BlockSpec((None, 1, d), lambda b, i: (b, 0, 0))
    const = lambda shape: pl.BlockSpec(shape, lambda b, i: (0,) * len(shape))
    out_shapes = (
        jax.ShapeDtypeStruct((bsz, seq, SSD_INNER), F32),
        jax.ShapeDtypeStruct((bsz, seq, SSD_BC), BF16),
        jax.ShapeDtypeStruct((bsz, seq, d), F32),
        jax.ShapeDtypeStruct((bsz, npair, seq, LANES), BF16),
        jax.ShapeDtypeStruct((bsz, npair, seq, LANES), BF16),
        jax.ShapeDtypeStruct((bsz, seq, d), BF16),
        jax.ShapeDtypeStruct((bsz, seq, DT_COLS), F32),
    )
    return pl.pallas_call(
        functools.partial(_inproj0_kernel, n_tiles),
        grid=(bsz, n_tiles),
        in_specs=[tok(d),
                  pl.BlockSpec((None, SUBLANES, d), lambda b, i: (b, jnp.maximum(i * per_tile - 1, 0), 0)),
                  pl.BlockSpec((None, SUBLANES, d),
                               lambda b, i: (b, jnp.minimum((i + 1) * per_tile, n_halo - 1), 0)),
                  const((1, d)), vec, vec,
                  _resident((d, ncols), lambda b, i: (0, 0)),
                  const((3, XBC_COLS)), const((1, XBC_COLS))],
        out_specs=(tok(SSD_INNER), tok(SSD_BC), tok(d), pair_major, pair_major, tok(d), tok(DT_COLS)),
        out_shape=out_shapes,
        compiler_params=_cparams(("arbitrary", "arbitrary")),
        name="inproj0",
    )(x, x, x, nw, shift, scale, w, conv_w, conv_b)


SSD_CHUNKS_PER_STEP = 2
SPLIT_CUMSUM = 3
SPLIT_EXPAND = 2


def _ssd_kernel(backward, n_ctx_steps, n_lat_steps, *refs):
    if backward:
        (xs_ref, bc_ref, dt_ref, xsc_ref, bcc_ref, dtc_ref, dtb_ref, alog_ref, e_ref,
         yprev_ref, y_ref, state_ref) = refs
    else:
        (xs_ref, bc_ref, dt_ref, xsc_ref, bcc_ref, dtc_ref, dtb_ref, alog_ref, e_ref,
         dsk_ref, y_ref, state_ref) = refs
    q = SSD_CHUNK
    nch = SSD_CHUNKS_PER_STEP
    inner = SSD_INNER
    gn = SSD_STATE
    gcols = inner // SSD_GROUPS
    heads_per_group = SSD_HEADS // SSD_GROUPS
    c = pl.program_id(1)
    scan_dir = 1 if backward else 0

    ri = lax.broadcasted_iota(jnp.int32, (q, q), 0)
    ci = lax.broadcasted_iota(jnp.int32, (q, q), 1)
    contributes = (ri >= ci, ri <= ci)
    tri = [jnp.concatenate([jnp.where(m, 1.0, 0.0).astype(BF16)] * SPLIT_CUMSUM, axis=1) for m in contributes]
    lane = lax.broadcasted_iota(jnp.int32, (q, LANES), 1)
    low_half = lane < SSD_HEAD_DIM

    def decay_terms(dtraw, direction):
        cols = slice(direction * LANES, (direction + 1) * LANES)
        dt = _softplus(dtraw[:, cols] + dtb_ref[direction:direction + 1, :])
        a = dt * (-jnp.exp(alog_ref[direction:direction + 1, :]))
        a_cum = _dot(tri[direction], jnp.concatenate(_split_bf16(a, SPLIT_CUMSUM), axis=0))
        return dt, a_cum

    def expand(v):
        return _dot(jnp.concatenate(_split_bf16(v, SPLIT_EXPAND), axis=1), e_ref[...])

    def process(x, bc16, dtraw, out_rows):
        want_y = out_rows is not None
        bm16 = bc16[:, :SSD_GROUPS * gn]
        cm16 = bc16[:, SSD_GROUPS * gn:]

        dt, a_cum = decay_terms(dtraw, scan_dir)
        edge = a_cum[0:1, :] if backward else a_cum[q - 1:q, :]
        ea_full = expand(jnp.exp(a_cum))
        dtw_full = expand(dt * jnp.exp(edge - a_cum))
        chunk_decay = ea_full[0:1, :] if backward else ea_full[q - 1:q, :]
        xw = (x * dtw_full).astype(BF16)
        bts = [bm16[:, g * gn:(g + 1) * gn].astype(F32).T.astype(BF16) for g in range(SSD_GROUPS)]

        y_part = None
        if want_y and backward:
            y_part = yprev_ref[out_rows, :]
        elif want_y:
            x16 = x.astype(BF16)
            dirs = [(d_, ac * LOG2_E) for d_, ac in ((dt, a_cum), decay_terms(dtraw, 1))]
            tr = [(d_.T, ac.T) for d_, ac in dirs]
            ys = []
            for g in range(SSD_GROUPS):
                cb = _dot_nt(cm16[:, g * gn:(g + 1) * gn], bm16[:, g * gn:(g + 1) * gn])
                for pair in range(heads_per_group // 2):
                    ms = []
                    for hh in range(2):
                        h = g * heads_per_group + pair * 2 + hh
                        w = None
                        for direction in range(2):
                            ac, (dt_t, ac_t) = dirs[direction][1], tr[direction]
                            seg = ac[:, h:h + 1] - ac_t[h:h + 1, :]
                            term = jnp.exp2(jnp.where(contributes[direction], seg, -jnp.inf)) * dt_t[h:h + 1, :]
                            w = term if w is None else w + term
                        ms.append((cb * w).astype(BF16))
                    m_pair = jnp.concatenate(ms, axis=1)
                    col0 = (g * heads_per_group + pair * 2) * SSD_HEAD_DIM
                    xp = x16[:, col0:col0 + LANES]
                    zero = jnp.zeros_like(xp)
                    x_bd = jnp.concatenate([jnp.where(low_half, xp, zero),
                                            jnp.where(low_half, zero, xp)], axis=0)
                    ys.append(_dot(m_pair, x_bd))
            y_part = jnp.concatenate(ys, axis=1) + x * dsk_ref[...]

        def finish():
            if want_y:
                y_off = jnp.concatenate(
                    [_dot(cm16[:, g * gn:(g + 1) * gn], state_ref[:, g * gcols:(g + 1) * gcols].astype(BF16))
                     for g in range(SSD_GROUPS)], axis=1) * ea_full
                y_ref[out_rows, :] = y_part + y_off
            for g in range(SSD_GROUPS):
                sl = slice(g * gcols, (g + 1) * gcols)
                state_ref[:, sl] = state_ref[:, sl] * chunk_decay[:, sl] + _dot(bts[g], xw[:, sl])

        return finish

    def run_block(x_ref_, bc_ref_, dt_ref_, r0, is_latent):
        order = range(nch - 1, -1, -1) if backward else range(nch)
        finishers = []
        for k in order:
            rows = slice(r0 + k * q, r0 + (k + 1) * q)
            out_rows = slice(k * q, (k + 1) * q) if is_latent else None
            finishers.append(process(x_ref_[rows, :], bc_ref_[rows, :], dt_ref_[rows, :], out_rows))
        for fin in finishers:
            fin()

    @pl.when(c == 0)
    def _():
        state_ref[...] = jnp.zeros_like(state_ref)

    blk = nch * q
    for step in range(n_ctx_steps):
        cs = (n_ctx_steps - 1 - step) if backward else step

        @pl.when(c == step)
        def _(cs=cs):
            run_block(xsc_ref, bcc_ref, dtc_ref, cs * blk, False)

    @pl.when(c >= n_ctx_steps)
    def _():
        run_block(xs_ref, bc_ref, dt_ref, 0, True)


def ssd_pass(backward, xs, bc, dt, xs_c, bc_c, dt_c, dt_bias_rows, a_log_rows, expand2,
             d_skip_row=None, y_prev=None):
    bsz, seq, inner = xs.shape
    n_ctx = xs_c.shape[1]
    blk = SSD_CHUNK * SSD_CHUNKS_PER_STEP
    assert n_ctx % blk == 0 and seq % blk == 0
    ncs, nls = n_ctx // blk, seq // blk

    def lat(c):
        ls = jnp.maximum(c - ncs, 0)
        return (nls - 1 - ls) if backward else ls

    block = lambda width: pl.BlockSpec((None, blk, width), lambda b, c: (b, lat(c), 0))
    whole = lambda width: pl.BlockSpec((None, n_ctx, width), lambda b, c: (b, 0, 0))
    const = lambda shape: pl.BlockSpec(shape, lambda b, c: (0,) * len(shape))
    in_specs = [
        block(inner), block(SSD_BC), block(DT_COLS),
        whole(inner), whole(SSD_BC), whole(DT_COLS),
        const((2, LANES)), const((2, LANES)), const(expand2.shape),
    ]
    args = [xs, bc, dt, xs_c, bc_c, dt_c, dt_bias_rows, a_log_rows, expand2]
    if backward:
        in_specs += [block(inner)]
        args += [y_prev]
    else:
        in_specs += [const((1, inner))]
        args += [d_skip_row]
    return pl.pallas_call(
        functools.partial(_ssd_kernel, backward, ncs, nls),
        grid=(bsz, ncs + nls),
        in_specs=in_specs,
        out_specs=block(inner),
        out_shape=jax.ShapeDtypeStruct((bsz, seq, inner), F32),
        scratch_shapes=[pltpu.VMEM((SSD_STATE, inner), F32)],
        compiler_params=_cparams(("arbitrary", "arbitrary")),
        name="ssd_bwd" if backward else "ssd_fwd",
    )(*args)


NA_GROUP_HEADS = 4
NA_ROWS_PER_STEP = 4


def _natten_kernel(grid_rows, q_ref, k_ref, v_ref, kc_ref, vc_ref, bias_ref, o_ref):
    w = GRID_W
    nwin = NA_WIN_R * w
    hg = NA_GROUP_HEADS
    gd = hg * NA_HEAD_DIM
    ngroups = NA_HEADS // hg
    nrows = NA_ROWS_PER_STEP
    lane_head = lax.broadcasted_iota(jnp.int32, (w, gd), 1) // NA_HEAD_DIM
    low_half = lax.broadcasted_iota(jnp.int32, (w, LANES), 1) < NA_HEAD_DIM
    r_base = pl.program_id(1) * nrows
    rs = [jnp.clip(r_base + i - NA_WIN_R // 2, 0, grid_rows - NA_WIN_R) for i in range(nrows)]
    k0 = [pl.multiple_of(rs[i] * w, w) for i in range(nrows)]
    dr0 = [rs[i] - (r_base + i) + NA_WIN_R - 1 for i in range(nrows)]

    def scores(g):
        s_win = [[] for _ in range(nrows)]
        s_ctx = [[] for _ in range(nrows)]
        for pp in range(hg // 2):
            pair = g * (hg // 2) + pp
            wqs = []
            for i in range(nrows):
                qp = q_ref[pair, i * w:(i + 1) * w, :]
                zero = jnp.zeros_like(qp)
                wqs.append(jnp.concatenate([jnp.where(low_half, qp, zero), jnp.where(low_half, zero, qp)], axis=0))
            sc = _dot_nt(jnp.concatenate(wqs, axis=0), kc_ref[pair])
            for i in range(nrows):
                s_win[i].append(_dot_nt(wqs[i], k_ref[pair, pl.ds(k0[i], nwin), :]))
                s_ctx[i].append(sc[i * 2 * w:(i + 1) * 2 * w, :])
        return [(jnp.concatenate(s_win[i], axis=0), jnp.concatenate(s_ctx[i], axis=0)) for i in range(nrows)]

    def softmax(g, i, s):
        s_win, s_ctx = s
        s_win = jnp.concatenate(
            [s_win[:, t * LANES:(t + 1) * LANES] + bias_ref[g, dr0[i] + 2 * t]
             for t in range(nwin // LANES)], axis=1)
        m = jnp.maximum(jnp.max(s_win, axis=-1, keepdims=True), jnp.max(s_ctx, axis=-1, keepdims=True))
        p_win = jnp.exp(s_win - m)
        p_ctx = jnp.exp(s_ctx - m)
        denom = jnp.sum(p_win, axis=-1, keepdims=True) + jnp.sum(p_ctx, axis=-1, keepdims=True)
        return p_win.astype(BF16), p_ctx.astype(BF16), denom

    def values(g, ps):
        cols = slice(g * gd, (g + 1) * gd)
        o_ctx = _dot(jnp.concatenate([p[1] for p in ps], axis=0), vc_ref[:, cols])
        for i in range(nrows):
            p_win, _, denom = ps[i]
            o = (_dot(p_win, v_ref[pl.ds(k0[i], nwin), cols]) + o_ctx[i * hg * w:(i + 1) * hg * w, :]) / denom
            acc = o[:w, :]
            for hh in range(1, hg):
                acc = jnp.where(lane_head == hh, o[hh * w:(hh + 1) * w, :], acc)
            o_ref[i * w:(i + 1) * w, cols] = acc.astype(o_ref.dtype)

    s, p = {}, {}
    for step in range(ngroups + 2):
        if step < ngroups:
            s[step] = scores(step)
        if 0 <= step - 1 < ngroups:
            sg = s.pop(step - 1)
            p[step - 1] = [softmax(step - 1, i, sg[i]) for i in range(nrows)]
        if 0 <= step - 2 < ngroups:
            values(step - 2, p.pop(step - 2))


def natten(q, k, v, kc, vc, bias2):
    bsz, seq, d = v.shape
    n_ctx = vc.shape[1]
    grid_rows = seq // GRID_W
    npair = NA_HEADS // 2
    rows_tok = NA_ROWS_PER_STEP * GRID_W
    return pl.pallas_call(
        functools.partial(_natten_kernel, grid_rows),
        grid=(bsz, grid_rows // NA_ROWS_PER_STEP),
        in_specs=[
            pl.BlockSpec((None, npair, rows_tok, LANES), lambda b, r: (b, 0, r, 0)),
            pl.BlockSpec((None, npair, seq, LANES), lambda b, r: (b, 0, 0, 0)),
            pl.BlockSpec((None, seq, d), lambda b, r: (b, 0, 0)),
            _resident((None, npair, n_ctx, LANES), lambda b, r: (b, 0, 0, 0)),
            _resident((None, n_ctx, d), lambda b, r: (b, 0, 0)),
            _resident(bias2.shape, lambda b, r: (0, 0, 0, 0)),
        ],
        out_specs=pl.BlockSpec((None, rows_tok, d), lambda b, r: (b, r, 0)),
        out_shape=jax.ShapeDtypeStruct((bsz, seq, d), BF16),
        compiler_params=_cparams(("arbitrary", "arbitrary")),
        name="natten",
    )(q, k, v, kc, vc, bias2)


def natten_bias_table(rpb):
    w = GRID_W
    hg = NA_GROUP_HEADS
    qc = np.arange(w)[:, None]
    kc = np.arange(w)[None, :]
    win_start = np.clip(qc - NA_WIN_C // 2, 0, w - NA_WIN_C)
    col_ok = (kc >= win_start) & (kc < win_start + NA_WIN_C)
    dc_idx = np.clip(kc - qc, -(NA_WIN_C - 1), NA_WIN_C - 1) + NA_WIN_C - 1
    ndc = 2 * NA_WIN_C - 1
    onehot = (np.arange(ndc)[:, None, None] == dc_idx[None]).astype(np.float32)
    ndr = 2 * NA_WIN_R - 2
    nr = 2 * NA_WIN_R - 1
    two_rows = (np.arange(nr)[:, None, None]
                == np.arange(ndr)[None, :, None] + np.arange(2)[None, None, :]).astype(np.float32)
    t = jnp.einsum("ghrc,rdp,cqk->gdhqpk", rpb.astype(F32).reshape(NA_HEADS // hg, hg, nr, ndc),
                   jnp.asarray(two_rows), jnp.asarray(onehot), precision=lax.Precision.HIGHEST)
    t = jnp.where(col_ok[None, None, None, :, None, :], t, MASK_VALUE)
    return t.reshape(NA_HEADS // hg, ndr, hg * w, 2 * w)


def _mlp_kernel(ff_chunk, out_proj, final_norm, *refs):
    refs = list(refs)
    x = refs.pop(0)[...]
    if out_proj:
        ys_ref, z_ref, gnw_ref, yb_ref, wo_ref, gate_a_ref = refs[:6]
        del refs[:6]
        ka = ys_ref.shape[-1]
        ys = ys_ref[...] * _silu(z_ref[...])
        ys = (ys * lax.rsqrt(jnp.mean(ys * ys, axis=-1, keepdims=True) + RMS_EPS)) * gnw_ref[...]
        x = x + gate_a_ref[...] * (_dot(ys.astype(BF16), wo_ref[:ka, :]) + _dot(yb_ref[...], wo_ref[ka:, :]))
    nw_ref, sh_ref, sc_ref, gate_ref, w1_ref, w2_ref = refs[:6]
    fnw_ref = refs[6] if final_norm else None
    o_ref = refs[-1]
    h = _rms_mod(x, nw_ref[...], sh_ref[...], sc_ref[...]).astype(BF16)
    dff = w1_ref.shape[1]
    acc = None
    for c0 in range(0, dff, ff_chunk):
        a = jnp.maximum(_dot(h, w1_ref[:, c0:c0 + ff_chunk]), 0.0)
        part = _dot((a * a).astype(BF16), w2_ref[c0:c0 + ff_chunk, :])
        acc = part if acc is None else acc + part
    y = x + gate_ref[...] * acc
    if final_norm:
        ms = jnp.mean(y * y, axis=-1, keepdims=True)
        y = (y * lax.rsqrt(ms + RMS_EPS)) * fnw_ref[...]
    o_ref[...] = y


def mlp(x, nw, shift, scale, gate, w1, w2, layer, tm, ff_chunk, final_nw=None, out_proj=None):
    bsz, seq, d = x.shape
    tok = lambda width: pl.BlockSpec((None, tm, width), lambda b, i: (b, i, 0))
    vec = pl.BlockSpec((None, 1, d), lambda b, i: (b, 0, 0))
    row = pl.BlockSpec((1, d), lambda b, i: (0, 0))
    layer_w = lambda w: _resident((None,) + w.shape[1:], lambda b, i: (layer, 0, 0))
    in_specs = [tok(d)]
    args = [x]
    if out_proj is not None:
        ys, z, gnw, yb, wo, gate_a = out_proj
        in_specs += [tok(ys.shape[-1]), tok(z.shape[-1]), pl.BlockSpec(gnw.shape, lambda b, i: (0, 0)),
                     tok(yb.shape[-1]), _resident(wo.shape, lambda b, i: (0, 0)), vec]
        args += [ys, z, gnw, yb, wo, gate_a]
    in_specs += [row, vec, vec, vec, layer_w(w1), layer_w(w2)]
    args += [nw, shift, scale, gate, w1, w2]
    if final_nw is not None:
        in_specs.append(row)
        args.append(final_nw)
    return pl.pallas_call(
        functools.partial(_mlp_kernel, ff_chunk, out_proj is not None, final_nw is not None),
        grid=(bsz, seq // tm),
        in_specs=in_specs,
        out_specs=tok(d),
        out_shape=jax.ShapeDtypeStruct((bsz, seq, d), F32),
        compiler_params=_cparams(("arbitrary", "arbitrary")),
        name="mlp_final" if final_nw is not None else "mlp",
    )(*args)


def _shortconv_kernel(n_tiles, x_ref, xp_ref, xn_ref, nw_ref, sh_ref, sc_ref, gate_ref,
                      win_ref, cw_ref, wout_ref, o_ref):
    i = pl.program_id(1)
    tm, d = x_ref.shape
    inner = wout_ref.shape[0]
    hb = SUBLANES
    x = x_ref[...]
    x_ext = jnp.concatenate([xp_ref[...], x, xn_ref[...]], axis=0)
    h = _rms_mod(x_ext, nw_ref[...], sh_ref[...], sc_ref[...]).astype(BF16)
    gate_c = _dot(h, win_ref[:, inner:2 * inner])
    val = _dot(h, win_ref[:, 2 * inner:])
    u = gate_c * val
    rows = lax.broadcasted_iota(jnp.int32, u.shape, 0)
    outside = ((rows < hb) & (i == 0)) | ((rows >= tm + hb) & (i == n_tiles - 1))
    u = jnp.where(outside, 0.0, u)
    ext = tm + 2 * hb
    conv = (pltpu.roll(u, 1, 0) * cw_ref[0:1, :] + u * cw_ref[1:2, :]
            + pltpu.roll(u, ext - 1, 0) * cw_ref[2:3, :])[hb:hb + tm, :]
    gate_b = _dot(h[hb:hb + tm, :], win_ref[:, :inner])
    y = _dot((gate_b * conv).astype(BF16), wout_ref[...])
    o_ref[...] = x + gate_ref[...] * y


def shortconv(x, nw, shift, scale, gate, w_in, conv_w, w_out, tm):
    bsz, seq, d = x.shape
    n_tiles = seq // tm
    per_tile = tm // SUBLANES
    n_halo = seq // SUBLANES
    tok = pl.BlockSpec((None, tm, d), lambda b, i: (b, i, 0))
    vec = pl.BlockSpec((None, 1, d), lambda b, i: (b, 0, 0))
    return pl.pallas_call(
        functools.partial(_shortconv_kernel, n_tiles),
        grid=(bsz, n_tiles),
        in_specs=[
            tok,
            pl.BlockSpec((None, SUBLANES, d), lambda b, i: (b, jnp.maximum(i * per_tile - 1, 0), 0)),
            pl.BlockSpec((None, SUBLANES, d), lambda b, i: (b, jnp.minimum((i + 1) * per_tile, n_halo - 1), 0)),
            pl.BlockSpec((1, d), lambda b, i: (0, 0)), vec, vec, vec,
            _resident(w_in.shape, lambda b, i: (0, 0)),
            pl.BlockSpec(conv_w.shape, lambda b, i: (0, 0)),
            _resident(w_out.shape, lambda b, i: (0, 0)),
        ],
        out_specs=tok,
        out_shape=jax.ShapeDtypeStruct((bsz, seq, d), F32),
        compiler_params=_cparams(("arbitrary", "arbitrary")),
        name="shortconv",
    )(x, x, x, nw, shift, scale, gate, w_in, conv_w, w_out)


def _pad_lanes(row):
    return jnp.zeros((1, LANES), F32).at[0, :row.shape[0]].set(row.astype(F32))


def kernel(x, c, ctx, c_ctx, mod_w, mod_b, norm_mix_w, norm_mlp_w, mlp_w1, mlp_w2, ssdna_in_w, ssdna_conv_w,
           ssdna_conv_b, ssd_dt_bias, ssd_a_log, ssd_d, ssd_norm_w, na_rpb, ssdna_out_w, sc_in_w, sc_conv_w,
           sc_out_w, final_norm_w):
    bsz, seq, d = x.shape
    n_ctx = ctx.shape[1]
    tm = min(512, seq)
    inner = SSD_HEADS * SSD_HEAD_DIM
    gn2 = SSD_GROUPS * SSD_STATE

    mrows = -(-(bsz + 1) // SUBLANES) * SUBLANES
    cc = jnp.zeros((mrows, d), F32).at[:bsz].set(c).at[bsz].set(c_ctx)
    mod = modvec(cc, mod_w, mod_b).reshape(mod_w.shape[0], mrows, 6, d)
    vecs = lambda i: [mod[i, :bsz, j].reshape(bsz, 1, d) for j in range(6)]
    row = lambda v: v.reshape(1, -1).astype(F32)

    shift_a, scale_a, gate_a, shift_f, scale_f, gate_f = vecs(0)
    shift_c = jnp.broadcast_to(mod[0, bsz, 0].reshape(1, 1, d), (bsz, 1, d))
    scale_c = jnp.broadcast_to(mod[0, bsz, 1].reshape(1, 1, d), (bsz, 1, d))

    o_dt, o_k = inner + gn2, inner + gn2 + 2 * SSD_HEADS
    o_v = o_k + d
    o_c = o_v + d
    o_z, o_q = o_c + gn2, o_c + gn2 + inner
    segments = [(0, o_dt, 1.0), (o_c, gn2, 1.0), (o_z, inner, 1.0),
                (o_q, d, NA_HEAD_DIM ** -0.5),
                (o_k, d, 1.0), (o_v, d, 1.0), (o_dt, SSD_HEADS, 1.0), (o_dt + SSD_HEADS, SSD_HEADS, 1.0)]
    w0 = reorder_columns(ssdna_in_w[0], segments)
    nw0 = row(norm_mix_w[0])
    conv_w = ssdna_conv_w[0]
    conv_b = row(ssdna_conv_b[0])
    xs, bc, z, q_l, k_l, v_l, dt_l = inproj0(x, nw0, shift_a, scale_a, w0, conv_w, conv_b, tm)
    xs_c, bc_c, _, _, k_c, v_c, dt_c = inproj0(ctx, nw0, shift_c, scale_c, w0, conv_w, conv_b, n_ctx)

    d_skip_row = row(jnp.repeat(ssd_d[0], SSD_HEAD_DIM))
    expand = (np.arange(LANES)[:, None] == (np.arange(inner)[None, :] // SSD_HEAD_DIM)).astype(np.float32)
    expand2 = jnp.asarray(np.concatenate([expand] * SPLIT_EXPAND, axis=0), BF16)
    pad2 = lambda p: jnp.concatenate([_pad_lanes(p[0]), _pad_lanes(p[1])], axis=0)
    common = (xs, bc, dt_l, xs_c, bc_c, dt_c, pad2(ssd_dt_bias[0]), pad2(ssd_a_log[0]), expand2)
    y_fwd = ssd_pass(False, *common, d_skip_row=d_skip_row)
    y_ssd = ssd_pass(True, *common, y_prev=y_fwd)

    y_na = natten(q_l, k_l, v_l, k_c, v_c, natten_bias_table(na_rpb[0]))
    w1_all, w2_all = mlp_w1.astype(BF16), mlp_w2.astype(BF16)
    x = mlp(x, row(norm_mlp_w[0]), shift_f, scale_f, gate_f, w1_all, w2_all, 0, tm, 512,
            out_proj=(y_ssd, z, row(ssd_norm_w[0]), y_na, ssdna_out_w[0].astype(BF16), gate_a))

    shift_a, scale_a, gate_a, shift_f, scale_f, gate_f = vecs(1)
    x = shortconv(x, row(norm_mix_w[1]), shift_a, scale_a, gate_a, sc_in_w[0].astype(BF16), sc_conv_w[0],
                  sc_out_w[0].astype(BF16), tm)
    x = mlp(x, row(norm_mlp_w[1]), shift_f, scale_f, gate_f, w1_all, w2_all, 1, tm, 512,
            final_nw=row(final_norm_w))
    return x
```

```python
import functools

import numpy as np
import jax
import jax.numpy as jnp
from jax import lax
from jax.experimental import pallas as pl
from jax.experimental.pallas import tpu as pltpu

F32 = jnp.float32
BF16 = jnp.bfloat16

RMS_EPS = 1e-6
MASK_VALUE = -1e30
LOG2_E = 1.4426950408889634

GRID_W = 64
SSD_HEADS = 16
SSD_HEAD_DIM = 64
SSD_GROUPS = 2
SSD_STATE = 128
SSD_CHUNK = 128
NA_HEADS = 16
NA_HEAD_DIM = 64
NA_WIN_R = 8
NA_WIN_C = 16

LANES = 128
SUBLANES = 8
VMEM_LIMIT = 56 * 1024 * 1024


def _cparams(semantics):
    return pltpu.CompilerParams(dimension_semantics=semantics, vmem_limit_bytes=VMEM_LIMIT)


def _resident(block_shape, index_map):
    return pl.BlockSpec(block_shape, index_map, pipeline_mode=pl.Buffered(1))


def _rms_mod(x, nw, shift, scale):
    ms = jnp.mean(x * x, axis=-1, keepdims=True)
    return (x * lax.rsqrt(ms + RMS_EPS)) * nw * (1.0 + scale) + shift


def _silu(x):
    return x * jax.nn.sigmoid(x)


def _softplus(x):
    return jnp.maximum(x, 0.0) + jnp.log1p(jnp.exp(-jnp.abs(x)))


def _dot(a, b):
    return jnp.dot(a, b, preferred_element_type=F32)


def _dot_nt(a, b):
    return lax.dot_general(a, b, (((1,), (1,)), ((), ())), preferred_element_type=F32)


def _split_bf16(a, parts):
    out = []
    r = a
    for _ in range(parts):
        h = r.astype(BF16)
        out.append(h)
        r = r - h.astype(F32)
    return out


def _modvec_kernel(c_ref, w_ref, b_ref, o_ref):
    s = _silu(c_ref[...]).astype(BF16)
    o_ref[...] = _dot(s, w_ref[...].astype(BF16)) + b_ref[...]


def modvec(cc, mod_w, mod_b, tn=1536):
    depth, d, n = mod_w.shape
    rows = cc.shape[0]
    return pl.pallas_call(
        _modvec_kernel,
        grid=(depth, n // tn),
        in_specs=[
            pl.BlockSpec((rows, d), lambda i, j: (0, 0)),
            pl.BlockSpec((None, d, tn), lambda i, j: (i, 0, j)),
            pl.BlockSpec((None, 1, tn), lambda i, j: (i, 0, j)),
        ],
        out_specs=pl.BlockSpec((None, rows, tn), lambda i, j: (i, 0, j)),
        out_shape=jax.ShapeDtypeStruct((depth, rows, n), F32),
        compiler_params=_cparams(("arbitrary", "arbitrary")),
        name="modvec",
    )(cc, mod_w, mod_b.reshape(depth, 1, n))


def _reorder_columns_kernel(segments, w_ref, o_ref):
    dst = 0
    for src, width, scale in segments:
        pad = -width % LANES
        v = w_ref[:, src:src + width]
        if scale != 1.0:
            v = v * scale
        o_ref[:, dst:dst + width] = v.astype(o_ref.dtype)
        if pad:
            o_ref[:, dst + width:dst + width + pad] = jnp.zeros((o_ref.shape[0], pad), o_ref.dtype)
        dst += width + pad


def reorder_columns(w, segments, rows_per_step=128):
    k, n = w.shape
    n_out = sum(width + (-width % LANES) for _, width, _ in segments)
    return pl.pallas_call(
        functools.partial(_reorder_columns_kernel, tuple(segments)),
        grid=(k // rows_per_step,),
        in_specs=[pl.BlockSpec((rows_per_step, n), lambda i: (i, 0))],
        out_specs=pl.BlockSpec((rows_per_step, n_out), lambda i: (i, 0)),
        out_shape=jax.ShapeDtypeStruct((k, n_out), BF16),
        compiler_params=_cparams(("arbitrary",)),
        name="reorder_columns",
    )(w)


SSD_INNER = SSD_HEADS * SSD_HEAD_DIM
SSD_BC = 2 * SSD_GROUPS * SSD_STATE
XBC_COLS = SSD_INNER + SSD_BC
DT_COLS = 2 * LANES


def _inproj0_kernel(n_tiles, x_ref, xp_ref, xn_ref, nw_ref, sh_ref, sc_ref, w_ref, cw_ref, cb_ref,
                    xs_ref, bc_ref, z_ref, q_ref, k_ref, v_ref, dt_ref):
    i = pl.program_id(1)
    tm, d = x_ref.shape
    hb = SUBLANES
    ext = tm + 2 * hb
    x_ext = jnp.concatenate([xp_ref[...], x_ref[...], xn_ref[...]], axis=0)
    h_ext = _rms_mod(x_ext, nw_ref[...], sh_ref[...], sc_ref[...]).astype(BF16)
    u = _dot(h_ext, w_ref[:, :XBC_COLS])
    rows = lax.broadcasted_iota(jnp.int32, u.shape, 0)
    outside = ((rows < hb) & (i == 0)) | ((rows >= tm + hb) & (i == n_tiles - 1))
    u = jnp.where(outside, 0.0, u)
    xc = (pltpu.roll(u, 1, 0) * cw_ref[0:1, :] + u * cw_ref[1:2, :]
          + pltpu.roll(u, ext - 1, 0) * cw_ref[2:3, :])[hb:hb + tm, :] + cb_ref[...]
    xc = _silu(xc)
    xs_ref[...] = xc[:, :SSD_INNER]
    bc_ref[...] = xc[:, SSD_INNER:].astype(bc_ref.dtype)

    h = h_ext[hb:hb + tm, :]
    c0 = XBC_COLS
    for ref, width in ((z_ref, d), (q_ref, d), (k_ref, d), (v_ref, d), (dt_ref, DT_COLS)):
        y = _dot(h, w_ref[:, c0:c0 + width]).astype(ref.dtype)
        if len(ref.shape) == 3:
            for p in range(ref.shape[0]):
                ref[p] = y[:, p * LANES:(p + 1) * LANES]
        else:
            ref[...] = y
        c0 += width


def inproj0(x, nw, shift, scale, w, conv_w, conv_b, tm):
    bsz, seq, d = x.shape
    ncols = w.shape[1]
    npair = d // LANES
    n_tiles = seq // tm
    per_tile = tm // SUBLANES
    n_halo = seq // SUBLANES
    tok = lambda width: pl.BlockSpec((None, tm, width), lambda b, i: (b, i, 0))
    pair_major = pl.BlockSpec((None, npair, tm, LANES), lambda b, i: (b, 0, i, 0))
    vec = pl.BlockSpec((None, 1, d), lambda b, i: (b, 0, 0))
    const = lambda shape: pl.BlockSpec(shape, lambda b, i: (0,) * len(shape))
    out_shapes = (
        jax.ShapeDtypeStruct((bsz, seq, SSD_INNER), F32),
        jax.ShapeDtypeStruct((bsz, seq, SSD_BC), BF16),
        jax.ShapeDtypeStruct((bsz, seq, d), F32),
        jax.ShapeDtypeStruct((bsz, npair, seq, LANES), BF16),
        jax.ShapeDtypeStruct((bsz, npair, seq, LANES), BF16),
        jax.ShapeDtypeStruct((bsz, seq, d), BF16),
        jax.ShapeDtypeStruct((bsz, seq, DT_COLS), F32),
    )
    return pl.pallas_call(
        functools.partial(_inproj0_kernel, n_tiles),
        grid=(bsz, n_tiles),
        in_specs=[tok(d),
                  pl.BlockSpec((None, SUBLANES, d), lambda b, i: (b, jnp.maximum(i * per_tile - 1, 0), 0)),
                  pl.BlockSpec((None, SUBLANES, d),
                               lambda b, i: (b, jnp.minimum((i + 1) * per_tile, n_halo - 1), 0)),
                  const((1, d)), vec, vec,
                  _resident((d, ncols), lambda b, i: (0, 0)),
                  const((3, XBC_COLS)), const((1, XBC_COLS))],
        out_specs=(tok(SSD_INNER), tok(SSD_BC), tok(d), pair_major, pair_major, tok(d), tok(DT_COLS)),
        out_shape=out_shapes,
        compiler_params=_cparams(("arbitrary", "arbitrary")),
        name="inproj0",
    )(x, x, x, nw, shift, scale, w, conv_w, conv_b)


SSD_CHUNKS_PER_STEP = 2
SPLIT_CUMSUM = 3
SPLIT_EXPAND = 2


def _ssd_kernel(backward, n_ctx_steps, n_lat_steps, *refs):
    if backward:
        (xs_ref, bc_ref, dt_ref, xsc_ref, bcc_ref, dtc_ref, dtb_ref, alog_ref, e_ref,
         yprev_ref, y_ref, state_ref) = refs
    else:
        (xs_ref, bc_ref, dt_ref, xsc_ref, bcc_ref, dtc_ref, dtb_ref, alog_ref, e_ref,
         dsk_ref, y_ref, state_ref) = refs
    q = SSD_CHUNK
    nch = SSD_CHUNKS_PER_STEP
    inner = SSD_INNER
    gn = SSD_STATE
    gcols = inner // SSD_GROUPS
    heads_per_group = SSD_HEADS // SSD_GROUPS
    c = pl.program_id(1)
    scan_dir = 1 if backward else 0

    ri = lax.broadcasted_iota(jnp.int32, (q, q), 0)
    ci = lax.broadcasted_iota(jnp.int32, (q, q), 1)
    contributes = (ri >= ci, ri <= ci)
    tri = [jnp.concatenate([jnp.where(m, 1.0, 0.0).astype(BF16)] * SPLIT_CUMSUM, axis=1) for m in contributes]
    lane = lax.broadcasted_iota(jnp.int32, (q, LANES), 1)
    low_half = lane < SSD_HEAD_DIM

    def decay_terms(dtraw, direction):
        cols = slice(direction * LANES, (direction + 1) * LANES)
        dt = _softplus(dtraw[:, cols] + dtb_ref[direction:direction + 1, :])
        a = dt * (-jnp.exp(alog_ref[direction:direction + 1, :]))
        a_cum = _dot(tri[direction], jnp.concatenate(_split_bf16(a, SPLIT_CUMSUM), axis=0))
        return dt, a_cum

    def expand(*vs):
        lhs = jnp.concatenate([jnp.concatenate(_split_bf16(v, SPLIT_EXPAND), axis=1) for v in vs], axis=0)
        full = _dot(lhs, e_ref[...])
        return [full[i * q:(i + 1) * q, :] for i in range(len(vs))]

    def process(x, bc16, dtraw, out_rows):
        want_y = out_rows is not None
        bm16 = bc16[:, :SSD_GROUPS * gn]
        cm16 = bc16[:, SSD_GROUPS * gn:]

        dt, a_cum = decay_terms(dtraw, scan_dir)
        edge = a_cum[0:1, :] if backward else a_cum[q - 1:q, :]
        ea_full, dtw_full = expand(jnp.exp(a_cum), dt * jnp.exp(edge - a_cum))
        chunk_decay = ea_full[0:1, :] if backward else ea_full[q - 1:q, :]
        xw = (x * dtw_full).astype(BF16)
        bts = [bm16[:, g * gn:(g + 1) * gn].astype(F32).T.astype(BF16) for g in range(SSD_GROUPS)]

        y_part = None
        if want_y and backward:
            y_part = yprev_ref[out_rows, :]
        elif want_y:
            x16 = x.astype(BF16)
            dirs = [(d_, ac * LOG2_E) for d_, ac in ((dt, a_cum), decay_terms(dtraw, 1))]
            tr = [(d_.T, ac.T) for d_, ac in dirs]
            ys = []
            for g in range(SSD_GROUPS):
                cb = _dot_nt(cm16[:, g * gn:(g + 1) * gn], bm16[:, g * gn:(g + 1) * gn])
                for pair in range(heads_per_group // 2):
                    ms = []
                    for hh in range(2):
                        h = g * heads_per_group + pair * 2 + hh
                        w = None
                        for direction in range(2):
                            ac, (dt_t, ac_t) = dirs[direction][1], tr[direction]
                            seg = ac[:, h:h + 1] - ac_t[h:h + 1, :]
                            term = jnp.exp2(jnp.where(contributes[direction], seg, -jnp.inf)) * dt_t[h:h + 1, :]
                            w = term if w is None else w + term
                        ms.append((cb * w).astype(BF16))
                    m_pair = jnp.concatenate(ms, axis=1)
                    col0 = (g * heads_per_group + pair * 2) * SSD_HEAD_DIM
                    xp = x16[:, col0:col0 + LANES]
                    zero = jnp.zeros_like(xp)
                    x_bd = jnp.concatenate([jnp.where(low_half, xp, zero),
                                            jnp.where(low_half, zero, xp)], axis=0)
                    ys.append(_dot(m_pair, x_bd))
            y_part = jnp.concatenate(ys, axis=1) + x * dsk_ref[...]

        def finish():
            if want_y:
                y_off = jnp.concatenate(
                    [_dot(cm16[:, g * gn:(g + 1) * gn], state_ref[:, g * gcols:(g + 1) * gcols].astype(BF16))
                     for g in range(SSD_GROUPS)], axis=1) * ea_full
                y_ref[out_rows, :] = y_part + y_off
            for g in range(SSD_GROUPS):
                sl = slice(g * gcols, (g + 1) * gcols)
                state_ref[:, sl] = state_ref[:, sl] * chunk_decay[:, sl] + _dot(bts[g], xw[:, sl])

        return finish

    def run_block(x_ref_, bc_ref_, dt_ref_, r0, is_latent):
        order = range(nch - 1, -1, -1) if backward else range(nch)
        finishers = []
        for k in order:
            rows = slice(r0 + k * q, r0 + (k + 1) * q)
            out_rows = slice(k * q, (k + 1) * q) if is_latent else None
            finishers.append(process(x_ref_[rows, :], bc_ref_[rows, :], dt_ref_[rows, :], out_rows))
        for fin in finishers:
            fin()

    @pl.when(c == 0)
    def _():
        state_ref[...] = jnp.zeros_like(state_ref)

    blk = nch * q
    for step in range(n_ctx_steps):
        cs = (n_ctx_steps - 1 - step) if backward else step

        @pl.when(c == step)
        def _(cs=cs):
            run_block(xsc_ref, bcc_ref, dtc_ref, cs * blk, False)

    @pl.when(c >= n_ctx_steps)
    def _():
        run_block(xs_ref, bc_ref, dt_ref, 0, True)


def ssd_pass(backward, xs, bc, dt, xs_c, bc_c, dt_c, dt_bias_rows, a_log_rows, expand2,
             d_skip_row=None, y_prev=None):
    bsz, seq, inner = xs.shape
    n_ctx = xs_c.shape[1]
    blk = SSD_CHUNK * SSD_CHUNKS_PER_STEP
    assert n_ctx % blk == 0 and seq % blk == 0
    ncs, nls = n_ctx // blk, seq // blk

    def lat(c):
        ls = jnp.maximum(c - ncs, 0)
        return (nls - 1 - ls) if backward else ls

    block = lambda width: pl.BlockSpec((None, blk, width), lambda b, c: (b, lat(c), 0))
    whole = lambda width: pl.BlockSpec((None, n_ctx, width), lambda b, c: (b, 0, 0))
    const = lambda shape: pl.BlockSpec(shape, lambda b, c: (0,) * len(shape))
    in_specs = [
        block(inner), block(SSD_BC), block(DT_COLS),
        whole(inner), whole(SSD_BC), whole(DT_COLS),
        const((2, LANES)), const((2, LANES)), const(expand2.shape),
    ]
    args = [xs, bc, dt, xs_c, bc_c, dt_c, dt_bias_rows, a_log_rows, expand2]
    if backward:
        in_specs += [block(inner)]
        args += [y_prev]
    else:
        in_specs += [const((1, inner))]
        args += [d_skip_row]
    return pl.pallas_call(
        functools.partial(_ssd_kernel, backward, ncs, nls),
        grid=(bsz, ncs + nls),
        in_specs=in_specs,
        out_specs=block(inner),
        out_shape=jax.ShapeDtypeStruct((bsz, seq, inner), F32),
        scratch_shapes=[pltpu.VMEM((SSD_STATE, inner), F32)],
        compiler_params=_cparams(("arbitrary", "arbitrary")),
        name="ssd_bwd" if backward else "ssd_fwd",
    )(*args)


NA_GROUP_HEADS = 4
NA_ROWS_PER_STEP = 4


def _natten_kernel(grid_rows, q_ref, k_ref, v_ref, kc_ref, vc_ref, bias_ref, o_ref):
    w = GRID_W
    nwin = NA_WIN_R * w
    hg = NA_GROUP_HEADS
    gd = hg * NA_HEAD_DIM
    ngroups = NA_HEADS // hg
    nrows = NA_ROWS_PER_STEP
    lane_head = lax.broadcasted_iota(jnp.int32, (w, gd), 1) // NA_HEAD_DIM
    low_half = lax.broadcasted_iota(jnp.int32, (w, LANES), 1) < NA_HEAD_DIM
    r_base = pl.program_id(1) * nrows
    rs = [jnp.clip(r_base + i - NA_WIN_R // 2, 0, grid_rows - NA_WIN_R) for i in range(nrows)]
    k0 = [pl.multiple_of(rs[i] * w, w) for i in range(nrows)]
    dr0 = [rs[i] - (r_base + i) + NA_WIN_R - 1 for i in range(nrows)]

    def scores(g):
        s_win = [[] for _ in range(nrows)]
        s_ctx = [[] for _ in range(nrows)]
        for pp in range(hg // 2):
            pair = g * (hg // 2) + pp
            wqs = []
            for i in range(nrows):
                qp = q_ref[pair, i * w:(i + 1) * w, :]
                zero = jnp.zeros_like(qp)
                wqs.append(jnp.concatenate([jnp.where(low_half, qp, zero), jnp.where(low_half, zero, qp)], axis=0))
            sc = _dot_nt(jnp.concatenate(wqs, axis=0), kc_ref[pair])
            for i in range(nrows):
                s_win[i].append(_dot_nt(wqs[i], k_ref[pair, pl.ds(k0[i], nwin), :]))
                s_ctx[i].append(sc[i * 2 * w:(i + 1) * 2 * w, :])
        return [(jnp.concatenate(s_win[i], axis=0), jnp.concatenate(s_ctx[i], axis=0)) for i in range(nrows)]

    def softmax(g, i, s):
        s_win, s_ctx = s
        s_win = jnp.concatenate(
            [s_win[:, t * LANES:(t + 1) * LANES] + bias_ref[g, dr0[i] + 2 * t]
             for t in range(nwin // LANES)], axis=1)
        m = jnp.maximum(jnp.max(s_win, axis=-1, keepdims=True), jnp.max(s_ctx, axis=-1, keepdims=True))
        p_win = jnp.exp(s_win - m)
        p_ctx = jnp.exp(s_ctx - m)
        denom = jnp.sum(p_win, axis=-1, keepdims=True) + jnp.sum(p_ctx, axis=-1, keepdims=True)
        return p_win.astype(BF16), p_ctx.astype(BF16), denom

    def values(g, ps):
        cols = slice(g * gd, (g + 1) * gd)
        o_ctx = _dot(jnp.concatenate([p[1] for p in ps], axis=0), vc_ref[:, cols])
        for i in range(nrows):
            p_win, _, denom = ps[i]
            o = (_dot(p_win, v_ref[pl.ds(k0[i], nwin), cols]) + o_ctx[i * hg * w:(i + 1) * hg * w, :]) / denom
            acc = o[:w, :]
            for hh in range(1, hg):
                acc = jnp.where(lane_head == hh, o[hh * w:(hh + 1) * w, :], acc)
            o_ref[i * w:(i + 1) * w, cols] = acc.astype(o_ref.dtype)

    s, p = {}, {}
    for step in range(ngroups + 2):
        if step < ngroups:
            s[step] = scores(step)
        if 0 <= step - 1 < ngroups:
            sg = s.pop(step - 1)
            p[step - 1] = [softmax(step - 1, i, sg[i]) for i in range(nrows)]
        if 0 <= step - 2 < ngroups:
            values(step - 2, p.pop(step - 2))


def natten(q, k, v, kc, vc, bias2):
    bsz, seq, d = v.shape
    n_ctx = vc.shape[1]
    grid_rows = seq // GRID_W
    npair = NA_HEADS // 2
    rows_tok = NA_ROWS_PER_STEP * GRID_W
    return pl.pallas_call(
        functools.partial(_natten_kernel, grid_rows),
        grid=(bsz, grid_rows // NA_ROWS_PER_STEP),
        in_specs=[
            pl.BlockSpec((None, npair, rows_tok, LANES), lambda b, r: (b, 0, r, 0)),
            pl.BlockSpec((None, npair, seq, LANES), lambda b, r: (b, 0, 0, 0)),
            pl.BlockSpec((None, seq, d), lambda b, r: (b, 0, 0)),
            _resident((None, npair, n_ctx, LANES), lambda b, r: (b, 0, 0, 0)),
            _resident((None, n_ctx, d), lambda b, r: (b, 0, 0)),
            _resident(bias2.shape, lambda b, r: (0, 0, 0, 0)),
        ],
        out_specs=pl.BlockSpec((None, rows_tok, d), lambda b, r: (b, r, 0)),
        out_shape=jax.ShapeDtypeStruct((bsz, seq, d), BF16),
        compiler_params=_cparams(("arbitrary", "arbitrary")),
        name="natten",
    )(q, k, v, kc, vc, bias2)


def natten_bias_table(rpb):
    w = GRID_W
    hg = NA_GROUP_HEADS
    qc = np.arange(w)[:, None]
    kc = np.arange(w)[None, :]
    win_start = np.clip(qc - NA_WIN_C // 2, 0, w - NA_WIN_C)
    col_ok = (kc >= win_start) & (kc < win_start + NA_WIN_C)
    dc_idx = np.clip(kc - qc, -(NA_WIN_C - 1), NA_WIN_C - 1) + NA_WIN_C - 1
    ndc = 2 * NA_WIN_C - 1
    onehot = (np.arange(ndc)[:, None, None] == dc_idx[None]).astype(np.float32)
    ndr = 2 * NA_WIN_R - 2
    nr = 2 * NA_WIN_R - 1
    t = jnp.einsum("hrc,cqk->hrqk", rpb.astype(F32), jnp.asarray(onehot), precision=lax.Precision.HIGHEST)
    t = jnp.where(col_ok[None, None], t, MASK_VALUE)

    def pair_rows_kernel(t_ref, o_ref):
        for dr in range(ndr):
            for hh in range(hg):
                o_ref[dr, hh * w:(hh + 1) * w, :] = jnp.concatenate([t_ref[hh, dr], t_ref[hh, dr + 1]], axis=1)

    return pl.pallas_call(
        pair_rows_kernel,
        grid=(NA_HEADS // hg,),
        in_specs=[pl.BlockSpec((hg, nr, w, w), lambda g: (g, 0, 0, 0))],
        out_specs=pl.BlockSpec((None, ndr, hg * w, 2 * w), lambda g: (g, 0, 0, 0)),
        out_shape=jax.ShapeDtypeStruct((NA_HEADS // hg, ndr, hg * w, 2 * w), F32),
        compiler_params=_cparams(("arbitrary",)),
        name="natten_bias_pairs",
    )(t)


def _mlp_kernel(ff_chunk, out_proj, final_norm, *refs):
    refs = list(refs)
    x = refs.pop(0)[...]
    if out_proj:
        ys_ref, z_ref, gnw_ref, yb_ref, wo_ref, gate_a_ref = refs[:6]
        del refs[:6]
        ka = ys_ref.shape[-1]
        ys = ys_ref[...] * _silu(z_ref[...])
        ys = (ys * lax.rsqrt(jnp.mean(ys * ys, axis=-1, keepdims=True) + RMS_EPS)) * gnw_ref[...]
        x = x + gate_a_ref[...] * (_dot(ys.astype(BF16), wo_ref[:ka, :]) + _dot(yb_ref[...], wo_ref[ka:, :]))
    nw_ref, sh_ref, sc_ref, gate_ref, w1_ref, w2_ref = refs[:6]
    fnw_ref = refs[6] if final_norm else None
    o_ref = refs[-1]
    h = _rms_mod(x, nw_ref[...], sh_ref[...], sc_ref[...]).astype(BF16)
    dff = w1_ref.shape[1]
    acc = None
    for c0 in range(0, dff, ff_chunk):
        a = jnp.maximum(_dot(h, w1_ref[:, c0:c0 + ff_chunk]), 0.0)
        part = _dot((a * a).astype(BF16), w2_ref[c0:c0 + ff_chunk, :])
        acc = part if acc is None else acc + part
    y = x + gate_ref[...] * acc
    if final_norm:
        ms = jnp.mean(y * y, axis=-1, keepdims=True)
        y = (y * lax.rsqrt(ms + RMS_EPS)) * fnw_ref[...]
    o_ref[...] = y


def mlp(x, nw, shift, scale, gate, w1, w2, layer, tm, ff_chunk, final_nw=None, out_proj=None):
    bsz, seq, d = x.shape
    tok = lambda width: pl.BlockSpec((None, tm, width), lambda b, i: (b, i, 0))
    vec = pl.BlockSpec((None, 1, d), lambda b, i: (b, 0, 0))
    row = pl.BlockSpec((1, d), lambda b, i: (0, 0))
    layer_w = lambda w: _resident((None,) + w.shape[1:], lambda b, i: (layer, 0, 0))
    in_specs = [tok(d)]
    args = [x]
    if out_proj is not None:
        ys, z, gnw, yb, wo, gate_a = out_proj
        in_specs += [tok(ys.shape[-1]), tok(z.shape[-1]), pl.BlockSpec(gnw.shape, lambda b, i: (0, 0)),
                     tok(yb.shape[-1]), _resident(wo.shape, lambda b, i: (0, 0)), vec]
        args += [ys, z, gnw, yb, wo, gate_a]
    in_specs += [row, vec, vec, vec, layer_w(w1), layer_w(w2)]
    args += [nw, shift, scale, gate, w1, w2]
    if final_nw is not None:
        in_specs.append(row)
        args.append(final_nw)
    return pl.pallas_call(
        functools.partial(_mlp_kernel, ff_chunk, out_proj is not None, final_nw is not None),
        grid=(bsz, seq // tm),
        in_specs=in_specs,
        out_specs=tok(d),
        out_shape=jax.ShapeDtypeStruct((bsz, seq, d), F32),
        compiler_params=_cparams(("arbitrary", "arbitrary")),
        name="mlp_final" if final_nw is not None else "mlp",
    )(*args)


def _shortconv_kernel(n_tiles, x_ref, xp_ref, xn_ref, nw_ref, sh_ref, sc_ref, gate_ref,
                      win_ref, cw_ref, wout_ref, o_ref):
    i = pl.program_id(1)
    tm, d = x_ref.shape
    inner = wout_ref.shape[0]
    hb = SUBLANES
    x = x_ref[...]
    x_ext = jnp.concatenate([xp_ref[...], x, xn_ref[...]], axis=0)
    h = _rms_mod(x_ext, nw_ref[...], sh_ref[...], sc_ref[...]).astype(BF16)
    gate_c = _dot(h, win_ref[:, inner:2 * inner])
    val = _dot(h, win_ref[:, 2 * inner:])
    u = gate_c * val
    rows = lax.broadcasted_iota(jnp.int32, u.shape, 0)
    outside = ((rows < hb) & (i == 0)) | ((rows >= tm + hb) & (i == n_tiles - 1))
    u = jnp.where(outside, 0.0, u)
    ext = tm + 2 * hb
    conv = (pltpu.roll(u, 1, 0) * cw_ref[0:1, :] + u * cw_ref[1:2, :]
            + pltpu.roll(u, ext - 1, 0) * cw_ref[2:3, :])[hb:hb + tm, :]
    gate_b = _dot(h[hb:hb + tm, :], win_ref[:, :inner])
    y = _dot((gate_b * conv).astype(BF16), wout_ref[...])
    o_ref[...] = x + gate_ref[...] * y


def shortconv(x, nw, shift, scale, gate, w_in, conv_w, w_out, tm):
    bsz, seq, d = x.shape
    n_tiles = seq // tm
    per_tile = tm // SUBLANES
    n_halo = seq // SUBLANES
    tok = pl.BlockSpec((None, tm, d), lambda b, i: (b, i, 0))
    vec = pl.BlockSpec((None, 1, d), lambda b, i: (b, 0, 0))
    return pl.pallas_call(
        functools.partial(_shortconv_kernel, n_tiles),
        grid=(bsz, n_tiles),
        in_specs=[
            tok,
            pl.BlockSpec((None, SUBLANES, d), lambda b, i: (b, jnp.maximum(i * per_tile - 1, 0), 0)),
            pl.BlockSpec((None, SUBLANES, d), lambda b, i: (b, jnp.minimum((i + 1) * per_tile, n_halo - 1), 0)),
            pl.BlockSpec((1, d), lambda b, i: (0, 0)), vec, vec, vec,
            _resident(w_in.shape, lambda b, i: (0, 0)),
            pl.BlockSpec(conv_w.shape, lambda b, i: (0, 0)),
            _resident(w_out.shape, lambda b, i: (0, 0)),
        ],
        out_specs=tok,
        out_shape=jax.ShapeDtypeStruct((bsz, seq, d), F32),
        compiler_params=_cparams(("arbitrary", "arbitrary")),
        name="shortconv",
    )(x, x, x, nw, shift, scale, gate, w_in, conv_w, w_out)


def _pad_lanes(row):
    return jnp.zeros((1, LANES), F32).at[0, :row.shape[0]].set(row.astype(F32))


def kernel(x, c, ctx, c_ctx, mod_w, mod_b, norm_mix_w, norm_mlp_w, mlp_w1, mlp_w2, ssdna_in_w, ssdna_conv_w,
           ssdna_conv_b, ssd_dt_bias, ssd_a_log, ssd_d, ssd_norm_w, na_rpb, ssdna_out_w, sc_in_w, sc_conv_w,
           sc_out_w, final_norm_w):
    bsz, seq, d = x.shape
    n_ctx = ctx.shape[1]
    tm = min(512, seq)
    inner = SSD_HEADS * SSD_HEAD_DIM
    gn2 = SSD_GROUPS * SSD_STATE

    mrows = -(-(bsz + 1) // SUBLANES) * SUBLANES
    cc = jnp.zeros((mrows, d), F32).at[:bsz].set(c).at[bsz].set(c_ctx)
    mod = modvec(cc, mod_w, mod_b).reshape(mod_w.shape[0], mrows, 6, d)
    vecs = lambda i: [mod[i, :bsz, j].reshape(bsz, 1, d) for j in range(6)]
    row = lambda v: v.reshape(1, -1).astype(F32)

    shift_a, scale_a, gate_a, shift_f, scale_f, gate_f = vecs(0)
    shift_c = jnp.broadcast_to(mod[0, bsz, 0].reshape(1, 1, d), (bsz, 1, d))
    scale_c = jnp.broadcast_to(mod[0, bsz, 1].reshape(1, 1, d), (bsz, 1, d))

    o_dt, o_k = inner + gn2, inner + gn2 + 2 * SSD_HEADS
    o_v = o_k + d
    o_c = o_v + d
    o_z, o_q = o_c + gn2, o_c + gn2 + inner
    segments = [(0, o_dt, 1.0), (o_c, gn2, 1.0), (o_z, inner, 1.0),
                (o_q, d, NA_HEAD_DIM ** -0.5),
                (o_k, d, 1.0), (o_v, d, 1.0), (o_dt, SSD_HEADS, 1.0), (o_dt + SSD_HEADS, SSD_HEADS, 1.0)]
    w0 = reorder_columns(ssdna_in_w[0], segments)
    nw0 = row(norm_mix_w[0])
    conv_w = ssdna_conv_w[0]
    conv_b = row(ssdna_conv_b[0])
    xs, bc, z, q_l, k_l, v_l, dt_l = inproj0(x, nw0, shift_a, scale_a, w0, conv_w, conv_b, tm)
    xs_c, bc_c, _, _, k_c, v_c, dt_c = inproj0(ctx, nw0, shift_c, scale_c, w0, conv_w, conv_b, n_ctx)

    d_skip_row = row(jnp.repeat(ssd_d[0], SSD_HEAD_DIM))
    expand = (np.arange(LANES)[:, None] == (np.arange(inner)[None, :] // SSD_HEAD_DIM)).astype(np.float32)
    expand2 = jnp.asarray(np.concatenate([expand] * SPLIT_EXPAND, axis=0), BF16)
    pad2 = lambda p: jnp.concatenate([_pad_lanes(p[0]), _pad_lanes(p[1])], axis=0)
    common = (xs, bc, dt_l, xs_c, bc_c, dt_c, pad2(ssd_dt_bias[0]), pad2(ssd_a_log[0]), expand2)
    y_fwd = ssd_pass(False, *common, d_skip_row=d_skip_row)
    y_ssd = ssd_pass(True, *common, y_prev=y_fwd)

    y_na = natten(q_l, k_l, v_l, k_c, v_c, natten_bias_table(na_rpb[0]))
    w1_all, w2_all = mlp_w1.astype(BF16), mlp_w2.astype(BF16)
    x = mlp(x, row(norm_mlp_w[0]), shift_f, scale_f, gate_f, w1_all, w2_all, 0, tm, 512,
            out_proj=(y_ssd, z, row(ssd_norm_w[0]), y_na, ssdna_out_w[0].astype(BF16), gate_a))

    shift_a, scale_a, gate_a, shift_f, scale_f, gate_f = vecs(1)
    x = shortconv(x, row(norm_mix_w[1]), shift_a, scale_a, gate_a, sc_in_w[0].astype(BF16), sc_conv_w[0],
                  sc_out_w[0].astype(BF16), tm)
    x = mlp(x, row(norm_mlp_w[1]), shift_f, scale_f, gate_f, w1_all, w2_all, 1, tm, 512,
            final_nw=row(final_norm_w))
    return x
```

```python
import functools

import numpy as np
import jax
import jax.numpy as jnp
from jax import lax
from jax.experimental import pallas as pl
from jax.experimental.pallas import tpu as pltpu

F32 = jnp.float32
BF16 = jnp.bfloat16

RMS_EPS = 1e-6
MASK_VALUE = -1e30
LOG2_E = 1.4426950408889634

GRID_W = 64
SSD_HEADS = 16
SSD_HEAD_DIM = 64
SSD_GROUPS = 2
SSD_STATE = 128
SSD_CHUNK = 128
NA_HEADS = 16
NA_HEAD_DIM = 64
NA_WIN_R = 8
NA_WIN_C = 16

LANES = 128
SUBLANES = 8
VMEM_LIMIT = 56 * 1024 * 1024


def _cparams(semantics):
    return pltpu.CompilerParams(dimension_semantics=semantics, vmem_limit_bytes=VMEM_LIMIT)


def _resident(block_shape, index_map):
    return pl.BlockSpec(block_shape, index_map, pipeline_mode=pl.Buffered(1))


def _rms_mod(x, nw, shift, scale):
    ms = jnp.mean(x * x, axis=-1, keepdims=True)
    return (x * lax.rsqrt(ms + RMS_EPS)) * nw * (1.0 + scale) + shift


def _silu(x):
    return x * jax.nn.sigmoid(x)


def _softplus(x):
    return jnp.maximum(x, 0.0) + jnp.log1p(jnp.exp(-jnp.abs(x)))


def _dot(a, b):
    return jnp.dot(a, b, preferred_element_type=F32)


def _dot_nt(a, b):
    return lax.dot_general(a, b, (((1,), (1,)), ((), ())), preferred_element_type=F32)


def _split_bf16(a, parts):
    out = []
    r = a
    for _ in range(parts):
        h = r.astype(BF16)
        out.append(h)
        r = r - h.astype(F32)
    return out


def _modvec_kernel(c_ref, w_ref, b_ref, o_ref):
    s = _silu(c_ref[...]).astype(BF16)
    o_ref[...] = _dot(s, w_ref[...].astype(BF16)) + b_ref[...]


def modvec(cc, mod_w, mod_b, tn=1536):
    depth, d, n = mod_w.shape
    rows = cc.shape[0]
    return pl.pallas_call(
        _modvec_kernel,
        grid=(depth, n // tn),
        in_specs=[
            pl.BlockSpec((rows, d), lambda i, j: (0, 0)),
            pl.BlockSpec((None, d, tn), lambda i, j: (i, 0, j)),
            pl.BlockSpec((None, 1, tn), lambda i, j: (i, 0, j)),
        ],
        out_specs=pl.BlockSpec((None, rows, tn), lambda i, j: (i, 0, j)),
        out_shape=jax.ShapeDtypeStruct((depth, rows, n), F32),
        compiler_params=_cparams(("arbitrary", "arbitrary")),
        name="modvec",
    )(cc, mod_w, mod_b.reshape(depth, 1, n))


def _reorder_columns_kernel(segments, w_ref, o_ref):
    dst = 0
    for src, width, scale in segments:
        pad = -width % LANES
        v = w_ref[:, src:src + width]
        if scale != 1.0:
            v = v * scale
        o_ref[:, dst:dst + width] = v.astype(o_ref.dtype)
        if pad:
            o_ref[:, dst + width:dst + width + pad] = jnp.zeros((o_ref.shape[0], pad), o_ref.dtype)
        dst += width + pad


def reorder_columns(w, segments, rows_per_step=128):
    k, n = w.shape
    n_out = sum(width + (-width % LANES) for _, width, _ in segments)
    return pl.pallas_call(
        functools.partial(_reorder_columns_kernel, tuple(segments)),
        grid=(k // rows_per_step,),
        in_specs=[pl.BlockSpec((rows_per_step, n), lambda i: (i, 0))],
        out_specs=pl.BlockSpec((rows_per_step, n_out), lambda i: (i, 0)),
        out_shape=jax.ShapeDtypeStruct((k, n_out), BF16),
        compiler_params=_cparams(("arbitrary",)),
        name="reorder_columns",
    )(w)


SSD_INNER = SSD_HEADS * SSD_HEAD_DIM
SSD_BC = 2 * SSD_GROUPS * SSD_STATE
XBC_COLS = SSD_INNER + SSD_BC
DT_COLS = 2 * LANES


def _inproj0_kernel(n_tiles, x_ref, xp_ref, xn_ref, nw_ref, sh_ref, sc_ref, w_ref, cw_ref, cb_ref,
                    xs_ref, bc_ref, z_ref, q_ref, k_ref, v_ref, dt_ref):
    i = pl.program_id(1)
    tm, d = x_ref.shape
    hb = SUBLANES
    ext = tm + 2 * hb
    x_ext = jnp.concatenate([xp_ref[...], x_ref[...], xn_ref[...]], axis=0)
    h_ext = _rms_mod(x_ext, nw_ref[...], sh_ref[...], sc_ref[...]).astype(BF16)
    u = _dot(h_ext, w_ref[:, :XBC_COLS])
    rows = lax.broadcasted_iota(jnp.int32, u.shape, 0)
    outside = ((rows < hb) & (i == 0)) | ((rows >= tm + hb) & (i == n_tiles - 1))
    u = jnp.where(outside, 0.0, u)
    xc = (pltpu.roll(u, 1, 0) * cw_ref[0:1, :] + u * cw_ref[1:2, :]
          + pltpu.roll(u, ext - 1, 0) * cw_ref[2:3, :])[hb:hb + tm, :] + cb_ref[...]
    xc = _silu(xc)
    xs_ref[...] = xc[:, :SSD_INNER]
    bc_ref[...] = xc[:, SSD_INNER:].astype(bc_ref.dtype)

    h = h_ext[hb:hb + tm, :]
    c0 = XBC_COLS
    for ref, width in ((z_ref, d), (q_ref, d), (k_ref, d), (v_ref, d), (dt_ref, DT_COLS)):
        y = _dot(h, w_ref[:, c0:c0 + width]).astype(ref.dtype)
        if len(ref.shape) == 3:
            for p in range(ref.shape[0]):
                ref[p] = y[:, p * LANES:(p + 1) * LANES]
        else:
            ref[...] = y
        c0 += width


def inproj0(x, nw, shift, scale, w, conv_w, conv_b, tm):
    bsz, seq, d = x.shape
    ncols = w.shape[1]
    npair = d // LANES
    n_tiles = seq // tm
    per_tile = tm // SUBLANES
    n_halo = seq // SUBLANES
    tok = lambda width: pl.BlockSpec((None, tm, width), lambda b, i: (b, i, 0))
    pair_major = pl.BlockSpec((None, npair, tm, LANES), lambda b, i: (b, 0, i, 0))
    vec = pl.BlockSpec((None, 1, d), lambda b, i: (b, 0, 0))
    const = lambda shape: pl.BlockSpec(shape, lambda b, i: (0,) * len(shape))
    out_shapes = (
        jax.ShapeDtypeStruct((bsz, seq, SSD_INNER), F32),
        jax.ShapeDtypeStruct((bsz, seq, SSD_BC), BF16),
        jax.ShapeDtypeStruct((bsz, seq, d), F32),
        jax.ShapeDtypeStruct((bsz, npair, seq, LANES), BF16),
        jax.ShapeDtypeStruct((bsz, npair, seq, LANES), BF16),
        jax.ShapeDtypeStruct((bsz, seq, d), BF16),
        jax.ShapeDtypeStruct((bsz, seq, DT_COLS), F32),
    )
    return pl.pallas_call(
        functools.partial(_inproj0_kernel, n_tiles),
        grid=(bsz, n_tiles),
        in_specs=[tok(d),
                  pl.BlockSpec((None, SUBLANES, d), lambda b, i: (b, jnp.maximum(i * per_tile - 1, 0), 0)),
                  pl.BlockSpec((None, SUBLANES, d),
                               lambda b, i: (b, jnp.minimum((i + 1) * per_tile, n_halo - 1), 0)),
                  const((1, d)), vec, vec,
                  _resident((d, ncols), lambda b, i: (0, 0)),
                  const((3, XBC_COLS)), const((1, XBC_COLS))],
        out_specs=(tok(SSD_INNER), tok(SSD_BC), tok(d), pair_major, pair_major, tok(d), tok(DT_COLS)),
        out_shape=out_shapes,
        compiler_params=_cparams(("arbitrary", "arbitrary")),
        name="inproj0",
    )(x, x, x, nw, shift, scale, w, conv_w, conv_b)


SSD_CHUNKS_PER_STEP = 2
SPLIT_CUMSUM = 3
SPLIT_EXPAND = 2


def _ssd_kernel(backward, n_ctx_steps, n_lat_steps, *refs):
    if backward:
        (xs_ref, bc_ref, dt_ref, xsc_ref, bcc_ref, dtc_ref, dtb_ref, alog_ref, e_ref,
         y_ref, state_ref) = refs
    else:
        (xs_ref, bc_ref, dt_ref, xsc_ref, bcc_ref, dtc_ref, dtb_ref, alog_ref, e_ref,
         dsk_ref, y_ref, state_ref) = refs
    q = SSD_CHUNK
    nch = SSD_CHUNKS_PER_STEP
    inner = SSD_INNER
    gn = SSD_STATE
    gcols = inner // SSD_GROUPS
    heads_per_group = SSD_HEADS // SSD_GROUPS
    c = pl.program_id(1)
    scan_dir = 1 if backward else 0

    ri = lax.broadcasted_iota(jnp.int32, (q, q), 0)
    ci = lax.broadcasted_iota(jnp.int32, (q, q), 1)
    contributes = (ri >= ci, ri <= ci)
    tri = [jnp.concatenate([jnp.where(m, 1.0, 0.0).astype(BF16)] * SPLIT_CUMSUM, axis=1) for m in contributes]
    lane = lax.broadcasted_iota(jnp.int32, (q, LANES), 1)
    low_half = lane < SSD_HEAD_DIM

    def decay_terms(dtraw, direction):
        cols = slice(direction * LANES, (direction + 1) * LANES)
        dt = _softplus(dtraw[:, cols] + dtb_ref[direction:direction + 1, :])
        a = dt * (-jnp.exp(alog_ref[direction:direction + 1, :]))
        a_cum = _dot(tri[direction], jnp.concatenate(_split_bf16(a, SPLIT_CUMSUM), axis=0))
        return dt, a_cum

    def expand(*vs):
        lhs = jnp.concatenate([jnp.concatenate(_split_bf16(v, SPLIT_EXPAND), axis=1) for v in vs], axis=0)
        full = _dot(lhs, e_ref[...])
        return [full[i * q:(i + 1) * q, :] for i in range(len(vs))]

    def process(x, bc16, dtraw, out_rows):
        want_y = out_rows is not None
        bm16 = bc16[:, :SSD_GROUPS * gn]
        cm16 = bc16[:, SSD_GROUPS * gn:]

        dt, a_cum = decay_terms(dtraw, scan_dir)
        edge = a_cum[0:1, :] if backward else a_cum[q - 1:q, :]
        ea_full, dtw_full = expand(jnp.exp(a_cum), dt * jnp.exp(edge - a_cum))
        chunk_decay = ea_full[0:1, :] if backward else ea_full[q - 1:q, :]
        xw = (x * dtw_full).astype(BF16)
        bts = [bm16[:, g * gn:(g + 1) * gn].astype(F32).T.astype(BF16) for g in range(SSD_GROUPS)]

        y_part = None
        if want_y and not backward:
            x16 = x.astype(BF16)
            dirs = [(d_, ac * LOG2_E) for d_, ac in ((dt, a_cum), decay_terms(dtraw, 1))]
            tr = [(d_.T, ac.T) for d_, ac in dirs]
            ys = []
            for g in range(SSD_GROUPS):
                cb = _dot_nt(cm16[:, g * gn:(g + 1) * gn], bm16[:, g * gn:(g + 1) * gn])
                for pair in range(heads_per_group // 2):
                    ms = []
                    for hh in range(2):
                        h = g * heads_per_group + pair * 2 + hh
                        w = None
                        for direction in range(2):
                            ac, (dt_t, ac_t) = dirs[direction][1], tr[direction]
                            seg = ac[:, h:h + 1] - ac_t[h:h + 1, :]
                            term = jnp.exp2(jnp.where(contributes[direction], seg, -jnp.inf)) * dt_t[h:h + 1, :]
                            w = term if w is None else w + term
                        ms.append((cb * w).astype(BF16))
                    m_pair = jnp.concatenate(ms, axis=1)
                    col0 = (g * heads_per_group + pair * 2) * SSD_HEAD_DIM
                    xp = x16[:, col0:col0 + LANES]
                    zero = jnp.zeros_like(xp)
                    x_bd = jnp.concatenate([jnp.where(low_half, xp, zero),
                                            jnp.where(low_half, zero, xp)], axis=0)
                    ys.append(_dot(m_pair, x_bd))
            y_part = jnp.concatenate(ys, axis=1) + x * dsk_ref[...]

        def finish():
            if want_y:
                y_off = jnp.concatenate(
                    [_dot(cm16[:, g * gn:(g + 1) * gn], state_ref[:, g * gcols:(g + 1) * gcols].astype(BF16))
                     for g in range(SSD_GROUPS)], axis=1) * ea_full
                y_ref[out_rows, :] = y_off if y_part is None else y_part + y_off
            for g in range(SSD_GROUPS):
                sl = slice(g * gcols, (g + 1) * gcols)
                state_ref[:, sl] = state_ref[:, sl] * chunk_decay[:, sl] + _dot(bts[g], xw[:, sl])

        return finish

    def run_block(x_ref_, bc_ref_, dt_ref_, r0, is_latent):
        order = range(nch - 1, -1, -1) if backward else range(nch)
        finishers = []
        for k in order:
            rows = slice(r0 + k * q, r0 + (k + 1) * q)
            out_rows = slice(k * q, (k + 1) * q) if is_latent else None
            finishers.append(process(x_ref_[rows, :], bc_ref_[rows, :], dt_ref_[rows, :], out_rows))
        for fin in finishers:
            fin()

    @pl.when(c == 0)
    def _():
        state_ref[...] = jnp.zeros_like(state_ref)

    blk = nch * q
    for step in range(n_ctx_steps):
        cs = (n_ctx_steps - 1 - step) if backward else step

        @pl.when(c == step)
        def _(cs=cs):
            run_block(xsc_ref, bcc_ref, dtc_ref, cs * blk, False)

    @pl.when(c >= n_ctx_steps)
    def _():
        run_block(xs_ref, bc_ref, dt_ref, 0, True)


def ssd_pass(backward, xs, bc, dt, xs_c, bc_c, dt_c, dt_bias_rows, a_log_rows, expand2,
             d_skip_row=None):
    bsz, seq, inner = xs.shape
    n_ctx = xs_c.shape[1]
    blk = SSD_CHUNK * SSD_CHUNKS_PER_STEP
    assert n_ctx % blk == 0 and seq % blk == 0
    ncs, nls = n_ctx // blk, seq // blk

    def lat(c):
        ls = jnp.maximum(c - ncs, 0)
        return (nls - 1 - ls) if backward else ls

    block = lambda width: pl.BlockSpec((None, blk, width), lambda b, c: (b, lat(c), 0))
    whole = lambda width: pl.BlockSpec((None, n_ctx, width), lambda b, c: (b, 0, 0))
    const = lambda shape: pl.BlockSpec(shape, lambda b, c: (0,) * len(shape))
    in_specs = [
        block(inner), block(SSD_BC), block(DT_COLS),
        whole(inner), whole(SSD_BC), whole(DT_COLS),
        const((2, LANES)), const((2, LANES)), const(expand2.shape),
    ]
    args = [xs, bc, dt, xs_c, bc_c, dt_c, dt_bias_rows, a_log_rows, expand2]
    if not backward:
        in_specs += [const((1, inner))]
        args += [d_skip_row]
    return pl.pallas_call(
        functools.partial(_ssd_kernel, backward, ncs, nls),
        grid=(bsz, ncs + nls),
        in_specs=in_specs,
        out_specs=block(inner),
        out_shape=jax.ShapeDtypeStruct((bsz, seq, inner), F32),
        scratch_shapes=[pltpu.VMEM((SSD_STATE, inner), F32)],
        compiler_params=_cparams(("arbitrary", "arbitrary")),
        name="ssd_bwd" if backward else "ssd_fwd",
    )(*args)


NA_GROUP_HEADS = 4
NA_ROWS_PER_STEP = 4


def _natten_kernel(grid_rows, q_ref, k_ref, v_ref, kc_ref, vc_ref, bias_ref, o_ref):
    w = GRID_W
    nwin = NA_WIN_R * w
    hg = NA_GROUP_HEADS
    gd = hg * NA_HEAD_DIM
    ngroups = NA_HEADS // hg
    nrows = NA_ROWS_PER_STEP
    lane_head = lax.broadcasted_iota(jnp.int32, (w, gd), 1) // NA_HEAD_DIM
    low_half = lax.broadcasted_iota(jnp.int32, (w, LANES), 1) < NA_HEAD_DIM
    r_base = pl.program_id(1) * nrows
    rs = [jnp.clip(r_base + i - NA_WIN_R // 2, 0, grid_rows - NA_WIN_R) for i in range(nrows)]
    k0 = [pl.multiple_of(rs[i] * w, w) for i in range(nrows)]
    dr0 = [rs[i] - (r_base + i) + NA_WIN_R - 1 for i in range(nrows)]

    def scores(g):
        s_win = [[] for _ in range(nrows)]
        s_ctx = [[] for _ in range(nrows)]
        for pp in range(hg // 2):
            pair = g * (hg // 2) + pp
            wqs = []
            for i in range(nrows):
                qp = q_ref[pair, i * w:(i + 1) * w, :]
                zero = jnp.zeros_like(qp)
                wqs.append(jnp.concatenate([jnp.where(low_half, qp, zero), jnp.where(low_half, zero, qp)], axis=0))
            sc = _dot_nt(jnp.concatenate(wqs, axis=0), kc_ref[pair])
            for i in range(nrows):
                s_win[i].append(_dot_nt(wqs[i], k_ref[pair, pl.ds(k0[i], nwin), :]))
                s_ctx[i].append(sc[i * 2 * w:(i + 1) * 2 * w, :])
        return [(jnp.concatenate(s_win[i], axis=0), jnp.concatenate(s_ctx[i], axis=0)) for i in range(nrows)]

    def softmax(g, i, s):
        s_win, s_ctx = s
        s_win = jnp.concatenate(
            [s_win[:, t * LANES:(t + 1) * LANES] + bias_ref[g, dr0[i] + 2 * t]
             for t in range(nwin // LANES)], axis=1)
        m = jnp.maximum(jnp.max(s_win, axis=-1, keepdims=True), jnp.max(s_ctx, axis=-1, keepdims=True))
        p_win = jnp.exp(s_win - m)
        p_ctx = jnp.exp(s_ctx - m)
        denom = jnp.sum(p_win, axis=-1, keepdims=True) + jnp.sum(p_ctx, axis=-1, keepdims=True)
        return p_win.astype(BF16), p_ctx.astype(BF16), denom

    def values(g, ps):
        cols = slice(g * gd, (g + 1) * gd)
        o_ctx = _dot(jnp.concatenate([p[1] for p in ps], axis=0), vc_ref[:, cols])
        for i in range(nrows):
            p_win, _, denom = ps[i]
            o = (_dot(p_win, v_ref[pl.ds(k0[i], nwin), cols]) + o_ctx[i * hg * w:(i + 1) * hg * w, :]) / denom
            acc = o[:w, :]
            for hh in range(1, hg):
                acc = jnp.where(lane_head == hh, o[hh * w:(hh + 1) * w, :], acc)
            o_ref[i * w:(i + 1) * w, cols] = acc.astype(o_ref.dtype)

    s, p = {}, {}
    for step in range(ngroups + 2):
        if step < ngroups:
            s[step] = scores(step)
        if 0 <= step - 1 < ngroups:
            sg = s.pop(step - 1)
            p[step - 1] = [softmax(step - 1, i, sg[i]) for i in range(nrows)]
        if 0 <= step - 2 < ngroups:
            values(step - 2, p.pop(step - 2))


def natten(q, k, v, kc, vc, bias2):
    bsz, seq, d = v.shape
    n_ctx = vc.shape[1]
    grid_rows = seq // GRID_W
    npair = NA_HEADS // 2
    rows_tok = NA_ROWS_PER_STEP * GRID_W
    return pl.pallas_call(
        functools.partial(_natten_kernel, grid_rows),
        grid=(bsz, grid_rows // NA_ROWS_PER_STEP),
        in_specs=[
            pl.BlockSpec((None, npair, rows_tok, LANES), lambda b, r: (b, 0, r, 0)),
            pl.BlockSpec((None, npair, seq, LANES), lambda b, r: (b, 0, 0, 0)),
            pl.BlockSpec((None, seq, d), lambda b, r: (b, 0, 0)),
            _resident((None, npair, n_ctx, LANES), lambda b, r: (b, 0, 0, 0)),
            _resident((None, n_ctx, d), lambda b, r: (b, 0, 0)),
            _resident(bias2.shape, lambda b, r: (0, 0, 0, 0)),
        ],
        out_specs=pl.BlockSpec((None, rows_tok, d), lambda b, r: (b, r, 0)),
        out_shape=jax.ShapeDtypeStruct((bsz, seq, d), BF16),
        compiler_params=_cparams(("arbitrary", "arbitrary")),
        name="natten",
    )(q, k, v, kc, vc, bias2)


def natten_bias_table(rpb):
    w = GRID_W
    hg = NA_GROUP_HEADS
    qc = np.arange(w)[:, None]
    kc = np.arange(w)[None, :]
    win_start = np.clip(qc - NA_WIN_C // 2, 0, w - NA_WIN_C)
    col_ok = (kc >= win_start) & (kc < win_start + NA_WIN_C)
    dc_idx = np.clip(kc - qc, -(NA_WIN_C - 1), NA_WIN_C - 1) + NA_WIN_C - 1
    ndc = 2 * NA_WIN_C - 1
    onehot = (np.arange(ndc)[:, None, None] == dc_idx[None]).astype(np.float32)
    ndr = 2 * NA_WIN_R - 2
    nr = 2 * NA_WIN_R - 1
    t = jnp.einsum("hrc,cqk->hrqk", rpb.astype(F32), jnp.asarray(onehot), precision=lax.Precision.HIGHEST)
    t = jnp.where(col_ok[None, None], t, MASK_VALUE)

    def pair_rows_kernel(t_ref, o_ref):
        for dr in range(ndr):
            for hh in range(hg):
                o_ref[dr, hh * w:(hh + 1) * w, :] = jnp.concatenate([t_ref[hh, dr], t_ref[hh, dr + 1]], axis=1)

    return pl.pallas_call(
        pair_rows_kernel,
        grid=(NA_HEADS // hg,),
        in_specs=[pl.BlockSpec((hg, nr, w, w), lambda g: (g, 0, 0, 0))],
        out_specs=pl.BlockSpec((None, ndr, hg * w, 2 * w), lambda g: (g, 0, 0, 0)),
        out_shape=jax.ShapeDtypeStruct((NA_HEADS // hg, ndr, hg * w, 2 * w), F32),
        compiler_params=_cparams(("arbitrary",)),
        name="natten_bias_pairs",
    )(t)


def _mlp_kernel(ff_chunk, out_proj, final_norm, *refs):
    refs = list(refs)
    x = refs.pop(0)[...]
    if out_proj:
        ysf_ref, ysb_ref, z_ref, gnw_ref, yb_ref, wo_ref, gate_a_ref = refs[:7]
        del refs[:7]
        ka = ysf_ref.shape[-1]
        ys = (ysf_ref[...] + ysb_ref[...]) * _silu(z_ref[...])
        ys = (ys * lax.rsqrt(jnp.mean(ys * ys, axis=-1, keepdims=True) + RMS_EPS)) * gnw_ref[...]
        x = x + gate_a_ref[...] * (_dot(ys.astype(BF16), wo_ref[:ka, :]) + _dot(yb_ref[...], wo_ref[ka:, :]))
    nw_ref, sh_ref, sc_ref, gate_ref, w1_ref, w2_ref = refs[:6]
    fnw_ref = refs[6] if final_norm else None
    o_ref = refs[-1]
    h = _rms_mod(x, nw_ref[...], sh_ref[...], sc_ref[...]).astype(BF16)
    dff = w1_ref.shape[1]
    acc = None
    for c0 in range(0, dff, ff_chunk):
        a = jnp.maximum(_dot(h, w1_ref[:, c0:c0 + ff_chunk]), 0.0)
        part = _dot((a * a).astype(BF16), w2_ref[c0:c0 + ff_chunk, :])
        acc = part if acc is None else acc + part
    y = x + gate_ref[...] * acc
    if final_norm:
        ms = jnp.mean(y * y, axis=-1, keepdims=True)
        y = (y * lax.rsqrt(ms + RMS_EPS)) * fnw_ref[...]
    o_ref[...] = y


def mlp(x, nw, shift, scale, gate, w1, w2, layer, tm, ff_chunk, final_nw=None, out_proj=None):
    bsz, seq, d = x.shape
    tok = lambda width: pl.BlockSpec((None, tm, width), lambda b, i: (b, i, 0))
    vec = pl.BlockSpec((None, 1, d), lambda b, i: (b, 0, 0))
    row = pl.BlockSpec((1, d), lambda b, i: (0, 0))
    layer_w = lambda w: _resident((None,) + w.shape[1:], lambda b, i: (layer, 0, 0))
    in_specs = [tok(d)]
    args = [x]
    if out_proj is not None:
        ysf, ysb, z, gnw, yb, wo, gate_a = out_proj
        in_specs += [tok(ysf.shape[-1]), tok(ysb.shape[-1]), tok(z.shape[-1]),
                     pl.BlockSpec(gnw.shape, lambda b, i: (0, 0)),
                     tok(yb.shape[-1]), _resident(wo.shape, lambda b, i: (0, 0)), vec]
        args += [ysf, ysb, z, gnw, yb, wo, gate_a]
    in_specs += [row, vec, vec, vec, layer_w(w1), layer_w(w2)]
    args += [nw, shift, scale, gate, w1, w2]
    if final_nw is not None:
        in_specs.append(row)
        args.append(final_nw)
    return pl.pallas_call(
        functools.partial(_mlp_kernel, ff_chunk, out_proj is not None, final_nw is not None),
        grid=(bsz, seq // tm),
        in_specs=in_specs,
        out_specs=tok(d),
        out_shape=jax.ShapeDtypeStruct((bsz, seq, d), F32),
        compiler_params=_cparams(("arbitrary", "arbitrary")),
        name="mlp_final" if final_nw is not None else "mlp",
    )(*args)


def _shortconv_kernel(n_tiles, x_ref, xp_ref, xn_ref, nw_ref, sh_ref, sc_ref, gate_ref,
                      win_ref, cw_ref, wout_ref, o_ref):
    i = pl.program_id(1)
    tm, d = x_ref.shape
    inner = wout_ref.shape[0]
    hb = SUBLANES
    x = x_ref[...]
    x_ext = jnp.concatenate([xp_ref[...], x, xn_ref[...]], axis=0)
    h = _rms_mod(x_ext, nw_ref[...], sh_ref[...], sc_ref[...]).astype(BF16)
    gate_c = _dot(h, win_ref[:, inner:2 * inner])
    val = _dot(h, win_ref[:, 2 * inner:])
    u = gate_c * val
    rows = lax.broadcasted_iota(jnp.int32, u.shape, 0)
    outside = ((rows < hb) & (i == 0)) | ((rows >= tm + hb) & (i == n_tiles - 1))
    u = jnp.where(outside, 0.0, u)
    ext = tm + 2 * hb
    conv = (pltpu.roll(u, 1, 0) * cw_ref[0:1, :] + u * cw_ref[1:2, :]
            + pltpu.roll(u, ext - 1, 0) * cw_ref[2:3, :])[hb:hb + tm, :]
    gate_b = _dot(h[hb:hb + tm, :], win_ref[:, :inner])
    y = _dot((gate_b * conv).astype(BF16), wout_ref[...])
    o_ref[...] = x + gate_ref[...] * y


def shortconv(x, nw, shift, scale, gate, w_in, conv_w, w_out, tm):
    bsz, seq, d = x.shape
    n_tiles = seq // tm
    per_tile = tm // SUBLANES
    n_halo = seq // SUBLANES
    tok = pl.BlockSpec((None, tm, d), lambda b, i: (b, i, 0))
    vec = pl.BlockSpec((None, 1, d), lambda b, i: (b, 0, 0))
    return pl.pallas_call(
        functools.partial(_shortconv_kernel, n_tiles),
        grid=(bsz, n_tiles),
        in_specs=[
            tok,
            pl.BlockSpec((None, SUBLANES, d), lambda b, i: (b, jnp.maximum(i * per_tile - 1, 0), 0)),
            pl.BlockSpec((None, SUBLANES, d), lambda b, i: (b, jnp.minimum((i + 1) * per_tile, n_halo - 1), 0)),
            pl.BlockSpec((1, d), lambda b, i: (0, 0)), vec, vec, vec,
            _resident(w_in.shape, lambda b, i: (0, 0)),
            pl.BlockSpec(conv_w.shape, lambda b, i: (0, 0)),
            _resident(w_out.shape, lambda b, i: (0, 0)),
        ],
        out_specs=tok,
        out_shape=jax.ShapeDtypeStruct((bsz, seq, d), F32),
        compiler_params=_cparams(("arbitrary", "arbitrary")),
        name="shortconv",
    )(x, x, x, nw, shift, scale, gate, w_in, conv_w, w_out)


def _pad_lanes(row):
    return jnp.zeros((1, LANES), F32).at[0, :row.shape[0]].set(row.astype(F32))


def kernel(x, c, ctx, c_ctx, mod_w, mod_b, norm_mix_w, norm_mlp_w, mlp_w1, mlp_w2, ssdna_in_w, ssdna_conv_w,
           ssdna_conv_b, ssd_dt_bias, ssd_a_log, ssd_d, ssd_norm_w, na_rpb, ssdna_out_w, sc_in_w, sc_conv_w,
           sc_out_w, final_norm_w):
    bsz, seq, d = x.shape
    n_ctx = ctx.shape[1]
    tm = min(512, seq)
    inner = SSD_HEADS * SSD_HEAD_DIM
    gn2 = SSD_GROUPS * SSD_STATE

    mrows = -(-(bsz + 1) // SUBLANES) * SUBLANES
    cc = jnp.zeros((mrows, d), F32).at[:bsz].set(c).at[bsz].set(c_ctx)
    mod = modvec(cc, mod_w, mod_b).reshape(mod_w.shape[0], mrows, 6, d)
    vecs = lambda i: [mod[i, :bsz, j].reshape(bsz, 1, d) for j in range(6)]
    row = lambda v: v.reshape(1, -1).astype(F32)

    shift_a, scale_a, gate_a, shift_f, scale_f, gate_f = vecs(0)
    shift_c = jnp.broadcast_to(mod[0, bsz, 0].reshape(1, 1, d), (bsz, 1, d))
    scale_c = jnp.broadcast_to(mod[0, bsz, 1].reshape(1, 1, d), (bsz, 1, d))

    o_dt, o_k = inner + gn2, inner + gn2 + 2 * SSD_HEADS
    o_v = o_k + d
    o_c = o_v + d
    o_z, o_q = o_c + gn2, o_c + gn2 + inner
    segments = [(0, o_dt, 1.0), (o_c, gn2, 1.0), (o_z, inner, 1.0),
                (o_q, d, NA_HEAD_DIM ** -0.5),
                (o_k, d, 1.0), (o_v, d, 1.0), (o_dt, SSD_HEADS, 1.0), (o_dt + SSD_HEADS, SSD_HEADS, 1.0)]
    w0 = reorder_columns(ssdna_in_w[0], segments)
    nw0 = row(norm_mix_w[0])
    conv_w = ssdna_conv_w[0]
    conv_b = row(ssdna_conv_b[0])
    xs, bc, z, q_l, k_l, v_l, dt_l = inproj0(x, nw0, shift_a, scale_a, w0, conv_w, conv_b, tm)
    xs_c, bc_c, _, _, k_c, v_c, dt_c = inproj0(ctx, nw0, shift_c, scale_c, w0, conv_w, conv_b, n_ctx)

    d_skip_row = row(jnp.repeat(ssd_d[0], SSD_HEAD_DIM))
    expand = (np.arange(LANES)[:, None] == (np.arange(inner)[None, :] // SSD_HEAD_DIM)).astype(np.float32)
    expand2 = jnp.asarray(np.concatenate([expand] * SPLIT_EXPAND, axis=0), BF16)
    pad2 = lambda p: jnp.concatenate([_pad_lanes(p[0]), _pad_lanes(p[1])], axis=0)
    common = (xs, bc, dt_l, xs_c, bc_c, dt_c, pad2(ssd_dt_bias[0]), pad2(ssd_a_log[0]), expand2)
    y_fwd = ssd_pass(False, *common, d_skip_row=d_skip_row)
    y_bwd = ssd_pass(True, *common)

    y_na = natten(q_l, k_l, v_l, k_c, v_c, natten_bias_table(na_rpb[0]))
    w1_all, w2_all = mlp_w1.astype(BF16), mlp_w2.astype(BF16)
    x = mlp(x, row(norm_mlp_w[0]), shift_f, scale_f, gate_f, w1_all, w2_all, 0, tm, 512,
            out_proj=(y_fwd, y_bwd, z, row(ssd_norm_w[0]), y_na, ssdna_out_w[0].astype(BF16), gate_a))

    shift_a, scale_a, gate_a, shift_f, scale_f, gate_f = vecs(1)
    x = shortconv(x, row(norm_mix_w[1]), shift_a, scale_a, gate_a, sc_in_w[0].astype(BF16), sc_conv_w[0],
                  sc_out_w[0].astype(BF16), tm)
    x = mlp(x, row(norm_mlp_w[1]), shift_f, scale_f, gate_f, w1_all, w2_all, 1, tm, 512,
            final_nw=row(final_norm_w))
    return x
```

```python
import functools

import numpy as np
import jax
import jax.numpy as jnp
from jax import lax
from jax.experimental import pallas as pl
from jax.experimental.pallas import tpu as pltpu

F32 = jnp.float32
BF16 = jnp.bfloat16

RMS_EPS = 1e-6
MASK_VALUE = -1e30
LOG2_E = 1.4426950408889634

GRID_W = 64
SSD_HEADS = 16
SSD_HEAD_DIM = 64
SSD_GROUPS = 2
SSD_STATE = 128
SSD_CHUNK = 128
NA_HEADS = 16
NA_HEAD_DIM = 64
NA_WIN_R = 8
NA_WIN_C = 16

LANES = 128
SUBLANES = 8
VMEM_LIMIT = 56 * 1024 * 1024


def _cparams(semantics):
    return pltpu.CompilerParams(dimension_semantics=semantics, vmem_limit_bytes=VMEM_LIMIT)


def _resident(block_shape, index_map):
    return pl.BlockSpec(block_shape, index_map, pipeline_mode=pl.Buffered(1))


def _rms_mod(x, nw, shift, scale):
    ms = jnp.mean(x * x, axis=-1, keepdims=True)
    return (x * lax.rsqrt(ms + RMS_EPS)) * nw * (1.0 + scale) + shift


def _silu(x):
    return x * jax.nn.sigmoid(x)


def _softplus(x):
    return jnp.maximum(x, 0.0) + jnp.log1p(jnp.exp(-jnp.abs(x)))


def _dot(a, b):
    return jnp.dot(a, b, preferred_element_type=F32)


def _dot_nt(a, b):
    return lax.dot_general(a, b, (((1,), (1,)), ((), ())), preferred_element_type=F32)


def _split_bf16(a, parts):
    out = []
    r = a
    for _ in range(parts):
        h = r.astype(BF16)
        out.append(h)
        r = r - h.astype(F32)
    return out


def _modvec_kernel(c_ref, w_ref, b_ref, o_ref):
    s = _silu(c_ref[...]).astype(BF16)
    o_ref[...] = _dot(s, w_ref[...].astype(BF16)) + b_ref[...]


def modvec(cc, mod_w, mod_b, tn=1536):
    depth, d, n = mod_w.shape
    rows = cc.shape[0]
    return pl.pallas_call(
        _modvec_kernel,
        grid=(depth, n // tn),
        in_specs=[
            pl.BlockSpec((rows, d), lambda i, j: (0, 0)),
            pl.BlockSpec((None, d, tn), lambda i, j: (i, 0, j)),
            pl.BlockSpec((None, 1, tn), lambda i, j: (i, 0, j)),
        ],
        out_specs=pl.BlockSpec((None, rows, tn), lambda i, j: (i, 0, j)),
        out_shape=jax.ShapeDtypeStruct((depth, rows, n), F32),
        compiler_params=_cparams(("arbitrary", "arbitrary")),
        name="modvec",
    )(cc, mod_w, mod_b.reshape(depth, 1, n))


def _reorder_columns_kernel(segments, w_ref, o_ref):
    dst = 0
    for src, width, scale in segments:
        pad = -width % LANES
        v = w_ref[:, src:src + width]
        if scale != 1.0:
            v = v * scale
        o_ref[:, dst:dst + width] = v.astype(o_ref.dtype)
        if pad:
            o_ref[:, dst + width:dst + width + pad] = jnp.zeros((o_ref.shape[0], pad), o_ref.dtype)
        dst += width + pad


def reorder_columns(w, layer, segments, rows_per_step=128):
    _, k, n = w.shape
    n_out = sum(width + (-width % LANES) for _, width, _ in segments)
    return pl.pallas_call(
        functools.partial(_reorder_columns_kernel, tuple(segments)),
        grid=(k // rows_per_step,),
        in_specs=[pl.BlockSpec((None, rows_per_step, n), lambda i: (layer, i, 0))],
        out_specs=pl.BlockSpec((rows_per_step, n_out), lambda i: (i, 0)),
        out_shape=jax.ShapeDtypeStruct((k, n_out), BF16),
        compiler_params=_cparams(("arbitrary",)),
        name="reorder_columns",
    )(w)


SSD_INNER = SSD_HEADS * SSD_HEAD_DIM
SSD_BC = 2 * SSD_GROUPS * SSD_STATE
XBC_COLS = SSD_INNER + SSD_BC
DT_COLS = 2 * LANES


def _inproj0_kernel(n_tiles, latent, x_ref, xp_ref, xn_ref, nw_ref, sh_ref, sc_ref, w_ref, cw_ref, cb_ref,
                    xs_ref, bc_ref, *out_refs):
    i = pl.program_id(1)
    tm, d = x_ref.shape
    hb = SUBLANES
    ext = tm + 2 * hb
    x_ext = jnp.concatenate([xp_ref[...], x_ref[...], xn_ref[...]], axis=0)
    h_ext = _rms_mod(x_ext, nw_ref[...], sh_ref[...], sc_ref[...]).astype(BF16)
    u = _dot(h_ext, w_ref[:, :XBC_COLS])
    rows = lax.broadcasted_iota(jnp.int32, u.shape, 0)
    outside = ((rows < hb) & (i == 0)) | ((rows >= tm + hb) & (i == n_tiles - 1))
    u = jnp.where(outside, 0.0, u)
    xc = (pltpu.roll(u, 1, 0) * cw_ref[0:1, :] + u * cw_ref[1:2, :]
          + pltpu.roll(u, ext - 1, 0) * cw_ref[2:3, :])[hb:hb + tm, :] + cb_ref[...]
    xc = _silu(xc)
    xs_ref[...] = xc[:, :SSD_INNER]
    bc_ref[...] = xc[:, SSD_INNER:].astype(bc_ref.dtype)

    h = h_ext[hb:hb + tm, :]
    c0 = XBC_COLS if latent else XBC_COLS + 2 * d
    for ref in out_refs:
        width = ref.shape[-1] if len(ref.shape) == 2 else ref.shape[0] * ref.shape[-1]
        y = _dot(h, w_ref[:, c0:c0 + width]).astype(ref.dtype)
        if len(ref.shape) == 3:
            for p in range(ref.shape[0]):
                ref[p] = y[:, p * LANES:(p + 1) * LANES]
        else:
            ref[...] = y
        c0 += width


def inproj0(x, nw, shift, scale, w, conv_w, conv_b, tm, latent=True):
    bsz, seq, d = x.shape
    ncols = w.shape[1]
    npair = d // LANES
    n_tiles = seq // tm
    per_tile = tm // SUBLANES
    n_halo = seq // SUBLANES
    tok = lambda width: pl.BlockSpec((None, tm, width), lambda b, i: (b, i, 0))
    pair_major = pl.BlockSpec((None, npair, tm, LANES), lambda b, i: (b, 0, i, 0))
    vec = pl.BlockSpec((None, 1, d), lambda b, i: (b, 0, 0))
    const = lambda shape: pl.BlockSpec(shape, lambda b, i: (0,) * len(shape))
    flat = lambda width, dtype: (jax.ShapeDtypeStruct((bsz, seq, width), dtype), tok(width))
    pairs = (jax.ShapeDtypeStruct((bsz, npair, seq, LANES), BF16), pair_major)
    outs = [flat(SSD_INNER, F32), flat(SSD_BC, BF16)]
    if latent:
        outs += [flat(d, F32), pairs]
    outs += [pairs, flat(d, BF16), flat(DT_COLS, F32)]
    return pl.pallas_call(
        functools.partial(_inproj0_kernel, n_tiles, latent),
        grid=(bsz, n_tiles),
        in_specs=[tok(d),
                  pl.BlockSpec((None, SUBLANES, d), lambda b, i: (b, jnp.maximum(i * per_tile - 1, 0), 0)),
                  pl.BlockSpec((None, SUBLANES, d),
                               lambda b, i: (b, jnp.minimum((i + 1) * per_tile, n_halo - 1), 0)),
                  const((1, d)), vec, vec,
                  _resident((d, ncols), lambda b, i: (0, 0)),
                  const((3, XBC_COLS)), const((1, XBC_COLS))],
        out_specs=tuple(spec for _, spec in outs),
        out_shape=tuple(shape for shape, _ in outs),
        compiler_params=_cparams(("arbitrary", "arbitrary")),
        name="inproj0",
    )(x, x, x, nw, shift, scale, w, conv_w, conv_b)


SSD_CHUNKS_PER_STEP = 2
SPLIT_CUMSUM = 3
SPLIT_EXPAND = 2


def _ssd_kernel(backward, n_ctx_steps, n_lat_steps, *refs):
    if backward:
        (xs_ref, bc_ref, dt_ref, xsc_ref, bcc_ref, dtc_ref, dtb_ref, alog_ref, e_ref,
         y_ref, state_ref) = refs
    else:
        (xs_ref, bc_ref, dt_ref, xsc_ref, bcc_ref, dtc_ref, dtb_ref, alog_ref, e_ref,
         dsk_ref, y_ref, state_ref) = refs
    q = SSD_CHUNK
    nch = SSD_CHUNKS_PER_STEP
    inner = SSD_INNER
    gn = SSD_STATE
    gcols = inner // SSD_GROUPS
    heads_per_group = SSD_HEADS // SSD_GROUPS
    c = pl.program_id(1)
    scan_dir = 1 if backward else 0

    ri = lax.broadcasted_iota(jnp.int32, (q, q), 0)
    ci = lax.broadcasted_iota(jnp.int32, (q, q), 1)
    contributes = (ri >= ci, ri <= ci)
    tri = [jnp.concatenate([jnp.where(m, 1.0, 0.0).astype(BF16)] * SPLIT_CUMSUM, axis=1) for m in contributes]
    lane = lax.broadcasted_iota(jnp.int32, (q, LANES), 1)
    low_half = lane < SSD_HEAD_DIM

    def decay_terms(dtraw, direction):
        cols = slice(direction * LANES, (direction + 1) * LANES)
        dt = _softplus(dtraw[:, cols] + dtb_ref[direction:direction + 1, :])
        a = dt * (-jnp.exp(alog_ref[direction:direction + 1, :]))
        a_cum = _dot(tri[direction], jnp.concatenate(_split_bf16(a, SPLIT_CUMSUM), axis=0))
        return dt, a_cum

    def expand(*vs):
        lhs = jnp.concatenate([jnp.concatenate(_split_bf16(v, SPLIT_EXPAND), axis=1) for v in vs], axis=0)
        full = _dot(lhs, e_ref[...])
        return [full[i * q:(i + 1) * q, :] for i in range(len(vs))]

    def process(x, bc16, dtraw, out_rows):
        want_y = out_rows is not None
        bm16 = bc16[:, :SSD_GROUPS * gn]
        cm16 = bc16[:, SSD_GROUPS * gn:]

        dt, a_cum = decay_terms(dtraw, scan_dir)
        edge = a_cum[0:1, :] if backward else a_cum[q - 1:q, :]
        ea_full, dtw_full = expand(jnp.exp(a_cum), dt * jnp.exp(edge - a_cum))
        chunk_decay = ea_full[0:1, :] if backward else ea_full[q - 1:q, :]
        xw = (x * dtw_full).astype(BF16)
        bts = [bm16[:, g * gn:(g + 1) * gn].astype(F32).T.astype(BF16) for g in range(SSD_GROUPS)]

        y_part = None
        if want_y and not backward:
            x16 = x.astype(BF16)
            dirs = [(d_, ac * LOG2_E) for d_, ac in ((dt, a_cum), decay_terms(dtraw, 1))]
            tr = [(d_.T, ac.T) for d_, ac in dirs]
            ys = []
            for g in range(SSD_GROUPS):
                cb = _dot_nt(cm16[:, g * gn:(g + 1) * gn], bm16[:, g * gn:(g + 1) * gn])
                for pair in range(heads_per_group // 2):
                    ms = []
                    for hh in range(2):
                        h = g * heads_per_group + pair * 2 + hh
                        w = None
                        for direction in range(2):
                            ac, (dt_t, ac_t) = dirs[direction][1], tr[direction]
                            seg = ac[:, h:h + 1] - ac_t[h:h + 1, :]
                            term = jnp.exp2(jnp.where(contributes[direction], seg, -jnp.inf)) * dt_t[h:h + 1, :]
                            w = term if w is None else w + term
                        ms.append((cb * w).astype(BF16))
                    m_pair = jnp.concatenate(ms, axis=1)
                    col0 = (g * heads_per_group + pair * 2) * SSD_HEAD_DIM
                    xp = x16[:, col0:col0 + LANES]
                    zero = jnp.zeros_like(xp)
                    x_bd = jnp.concatenate([jnp.where(low_half, xp, zero),
                                            jnp.where(low_half, zero, xp)], axis=0)
                    ys.append(_dot(m_pair, x_bd))
            y_part = jnp.concatenate(ys, axis=1) + x * dsk_ref[...]

        def finish():
            if want_y:
                y_off = jnp.concatenate(
                    [_dot(cm16[:, g * gn:(g + 1) * gn], state_ref[:, g * gcols:(g + 1) * gcols].astype(BF16))
                     for g in range(SSD_GROUPS)], axis=1) * ea_full
                y_ref[out_rows, :] = y_off if y_part is None else y_part + y_off
            for g in range(SSD_GROUPS):
                sl = slice(g * gcols, (g + 1) * gcols)
                state_ref[:, sl] = state_ref[:, sl] * chunk_decay[:, sl] + _dot(bts[g], xw[:, sl])

        return finish

    def run_block(x_ref_, bc_ref_, dt_ref_, r0, is_latent):
        order = range(nch - 1, -1, -1) if backward else range(nch)
        finishers = []
        for k in order:
            rows = slice(r0 + k * q, r0 + (k + 1) * q)
            out_rows = slice(k * q, (k + 1) * q) if is_latent else None
            finishers.append(process(x_ref_[rows, :], bc_ref_[rows, :], dt_ref_[rows, :], out_rows))
        for fin in finishers:
            fin()

    @pl.when(c == 0)
    def _():
        state_ref[...] = jnp.zeros_like(state_ref)

    blk = nch * q
    for step in range(n_ctx_steps):
        cs = (n_ctx_steps - 1 - step) if backward else step

        @pl.when(c == step)
        def _(cs=cs):
            run_block(xsc_ref, bcc_ref, dtc_ref, cs * blk, False)

    @pl.when(c >= n_ctx_steps)
    def _():
        run_block(xs_ref, bc_ref, dt_ref, 0, True)


def ssd_pass(backward, xs, bc, dt, xs_c, bc_c, dt_c, dt_bias_rows, a_log_rows, expand2,
             d_skip_row=None):
    bsz, seq, inner = xs.shape
    n_ctx = xs_c.shape[1]
    blk = SSD_CHUNK * SSD_CHUNKS_PER_STEP
    assert n_ctx % blk == 0 and seq % blk == 0
    ncs, nls = n_ctx // blk, seq // blk

    def lat(c):
        ls = jnp.maximum(c - ncs, 0)
        return (nls - 1 - ls) if backward else ls

    block = lambda width: pl.BlockSpec((None, blk, width), lambda b, c: (b, lat(c), 0))
    whole = lambda width: pl.BlockSpec((None, n_ctx, width), lambda b, c: (b, 0, 0))
    const = lambda shape: pl.BlockSpec(shape, lambda b, c: (0,) * len(shape))
    in_specs = [
        block(inner), block(SSD_BC), block(DT_COLS),
        whole(inner), whole(SSD_BC), whole(DT_COLS),
        const((2, LANES)), const((2, LANES)), const(expand2.shape),
    ]
    args = [xs, bc, dt, xs_c, bc_c, dt_c, dt_bias_rows, a_log_rows, expand2]
    if not backward:
        in_specs += [const((1, inner))]
        args += [d_skip_row]
    return pl.pallas_call(
        functools.partial(_ssd_kernel, backward, ncs, nls),
        grid=(bsz, ncs + nls),
        in_specs=in_specs,
        out_specs=block(inner),
        out_shape=jax.ShapeDtypeStruct((bsz, seq, inner), F32),
        scratch_shapes=[pltpu.VMEM((SSD_STATE, inner), F32)],
        compiler_params=_cparams(("arbitrary", "arbitrary")),
        name="ssd_bwd" if backward else "ssd_fwd",
    )(*args)


NA_GROUP_HEADS = 4
NA_ROWS_PER_STEP = 4


def _natten_kernel(grid_rows, q_ref, k_ref, v_ref, kc_ref, vc_ref, bias_ref, o_ref):
    w = GRID_W
    nwin = NA_WIN_R * w
    hg = NA_GROUP_HEADS
    gd = hg * NA_HEAD_DIM
    ngroups = NA_HEADS // hg
    nrows = NA_ROWS_PER_STEP
    lane_head = lax.broadcasted_iota(jnp.int32, (w, gd), 1) // NA_HEAD_DIM
    low_half = lax.broadcasted_iota(jnp.int32, (w, LANES), 1) < NA_HEAD_DIM
    r_base = pl.program_id(1) * nrows
    rs = [jnp.clip(r_base + i - NA_WIN_R // 2, 0, grid_rows - NA_WIN_R) for i in range(nrows)]
    k0 = [pl.multiple_of(rs[i] * w, w) for i in range(nrows)]
    dr0 = [rs[i] - (r_base + i) + NA_WIN_R - 1 for i in range(nrows)]

    def scores(g):
        s_win = [[] for _ in range(nrows)]
        s_ctx = [[] for _ in range(nrows)]
        for pp in range(hg // 2):
            pair = g * (hg // 2) + pp
            wqs = []
            for i in range(nrows):
                qp = q_ref[pair, i * w:(i + 1) * w, :]
                zero = jnp.zeros_like(qp)
                wqs.append(jnp.concatenate([jnp.where(low_half, qp, zero), jnp.where(low_half, zero, qp)], axis=0))
            sc = _dot_nt(jnp.concatenate(wqs, axis=0), kc_ref[pair])
            for i in range(nrows):
                s_win[i].append(_dot_nt(wqs[i], k_ref[pair, pl.ds(k0[i], nwin), :]))
                s_ctx[i].append(sc[i * 2 * w:(i + 1) * 2 * w, :])
        return [(jnp.concatenate(s_win[i], axis=0), jnp.concatenate(s_ctx[i], axis=0)) for i in range(nrows)]

    def softmax(g, i, s):
        s_win, s_ctx = s
        s_win = jnp.concatenate(
            [s_win[:, t * LANES:(t + 1) * LANES] + bias_ref[g, dr0[i] + 2 * t]
             for t in range(nwin // LANES)], axis=1)
        m = jnp.maximum(jnp.max(s_win, axis=-1, keepdims=True), jnp.max(s_ctx, axis=-1, keepdims=True))
        p_win = jnp.exp(s_win - m)
        p_ctx = jnp.exp(s_ctx - m)
        denom = jnp.sum(p_win, axis=-1, keepdims=True) + jnp.sum(p_ctx, axis=-1, keepdims=True)
        return p_win.astype(BF16), p_ctx.astype(BF16), denom

    def values(g, ps):
        cols = slice(g * gd, (g + 1) * gd)
        o_ctx = _dot(jnp.concatenate([p[1] for p in ps], axis=0), vc_ref[:, cols])
        for i in range(nrows):
            p_win, _, denom = ps[i]
            o = (_dot(p_win, v_ref[pl.ds(k0[i], nwin), cols]) + o_ctx[i * hg * w:(i + 1) * hg * w, :]) / denom
            acc = o[:w, :]
            for hh in range(1, hg):
                acc = jnp.where(lane_head == hh, o[hh * w:(hh + 1) * w, :], acc)
            o_ref[i * w:(i + 1) * w, cols] = acc.astype(o_ref.dtype)

    s, p = {}, {}
    for step in range(ngroups + 2):
        if step < ngroups:
            s[step] = scores(step)
        if 0 <= step - 1 < ngroups:
            sg = s.pop(step - 1)
            p[step - 1] = [softmax(step - 1, i, sg[i]) for i in range(nrows)]
        if 0 <= step - 2 < ngroups:
            values(step - 2, p.pop(step - 2))


def natten(q, k, v, kc, vc, bias2):
    bsz, seq, d = v.shape
    n_ctx = vc.shape[1]
    grid_rows = seq // GRID_W
    npair = NA_HEADS // 2
    rows_tok = NA_ROWS_PER_STEP * GRID_W
    return pl.pallas_call(
        functools.partial(_natten_kernel, grid_rows),
        grid=(bsz, grid_rows // NA_ROWS_PER_STEP),
        in_specs=[
            pl.BlockSpec((None, npair, rows_tok, LANES), lambda b, r: (b, 0, r, 0)),
            pl.BlockSpec((None, npair, seq, LANES), lambda b, r: (b, 0, 0, 0)),
            pl.BlockSpec((None, seq, d), lambda b, r: (b, 0, 0)),
            _resident((None, npair, n_ctx, LANES), lambda b, r: (b, 0, 0, 0)),
            _resident((None, n_ctx, d), lambda b, r: (b, 0, 0)),
            _resident(bias2.shape, lambda b, r: (0, 0, 0, 0)),
        ],
        out_specs=pl.BlockSpec((None, rows_tok, d), lambda b, r: (b, r, 0)),
        out_shape=jax.ShapeDtypeStruct((bsz, seq, d), BF16),
        compiler_params=_cparams(("arbitrary", "arbitrary")),
        name="natten",
    )(q, k, v, kc, vc, bias2)


def natten_bias_table(rpb):
    w = GRID_W
    hg = NA_GROUP_HEADS
    qc = np.arange(w)[:, None]
    kc = np.arange(w)[None, :]
    win_start = np.clip(qc - NA_WIN_C // 2, 0, w - NA_WIN_C)
    col_ok = (kc >= win_start) & (kc < win_start + NA_WIN_C)
    dc_idx = np.clip(kc - qc, -(NA_WIN_C - 1), NA_WIN_C - 1) + NA_WIN_C - 1
    ndc = 2 * NA_WIN_C - 1
    onehot = (np.arange(ndc)[:, None, None] == dc_idx[None]).astype(np.float32)
    ndr = 2 * NA_WIN_R - 2
    nr = 2 * NA_WIN_R - 1
    t = jnp.einsum("hrc,cqk->hrqk", rpb.astype(F32), jnp.asarray(onehot), precision=lax.Precision.HIGHEST)
    t = jnp.where(col_ok[None, None], t, MASK_VALUE)

    def pair_rows_kernel(t_ref, o_ref):
        for dr in range(ndr):
            for hh in range(hg):
                o_ref[dr, hh * w:(hh + 1) * w, :] = jnp.concatenate([t_ref[hh, dr], t_ref[hh, dr + 1]], axis=1)

    return pl.pallas_call(
        pair_rows_kernel,
        grid=(NA_HEADS // hg,),
        in_specs=[pl.BlockSpec((hg, nr, w, w), lambda g: (g, 0, 0, 0))],
        out_specs=pl.BlockSpec((None, ndr, hg * w, 2 * w), lambda g: (g, 0, 0, 0)),
        out_shape=jax.ShapeDtypeStruct((NA_HEADS // hg, ndr, hg * w, 2 * w), F32),
        compiler_params=_cparams(("arbitrary",)),
        name="natten_bias_pairs",
    )(t)


def _mlp_kernel(ff_chunk, out_proj, final_norm, *refs):
    refs = list(refs)
    x = refs.pop(0)[...]
    if out_proj:
        ysf_ref, ysb_ref, z_ref, gnw_ref, yb_ref, wo_ref, gate_a_ref = refs[:7]
        del refs[:7]
        ka = ysf_ref.shape[-1]
        ys = (ysf_ref[...] + ysb_ref[...]) * _silu(z_ref[...])
        ys = (ys * lax.rsqrt(jnp.mean(ys * ys, axis=-1, keepdims=True) + RMS_EPS)) * gnw_ref[...]
        x = x + gate_a_ref[...] * (_dot(ys.astype(BF16), wo_ref[:ka, :]) + _dot(yb_ref[...], wo_ref[ka:, :]))
    nw_ref, sh_ref, sc_ref, gate_ref, w1_ref, w2_ref = refs[:6]
    fnw_ref = refs[6] if final_norm else None
    o_ref = refs[-1]
    h = _rms_mod(x, nw_ref[...], sh_ref[...], sc_ref[...]).astype(BF16)
    dff = w1_ref.shape[1]
    acc = None
    for c0 in range(0, dff, ff_chunk):
        a = jnp.maximum(_dot(h, w1_ref[:, c0:c0 + ff_chunk]), 0.0)
        part = _dot((a * a).astype(BF16), w2_ref[c0:c0 + ff_chunk, :])
        acc = part if acc is None else acc + part
    y = x + gate_ref[...] * acc
    if final_norm:
        ms = jnp.mean(y * y, axis=-1, keepdims=True)
        y = (y * lax.rsqrt(ms + RMS_EPS)) * fnw_ref[...]
    o_ref[...] = y


def mlp(x, nw, shift, scale, gate, w1, w2, layer, tm, ff_chunk, final_nw=None, out_proj=None):
    bsz, seq, d = x.shape
    tok = lambda width: pl.BlockSpec((None, tm, width), lambda b, i: (b, i, 0))
    vec = pl.BlockSpec((None, 1, d), lambda b, i: (b, 0, 0))
    row = pl.BlockSpec((1, d), lambda b, i: (0, 0))
    layer_w = lambda w: _resident((None,) + w.shape[1:], lambda b, i: (layer, 0, 0))
    in_specs = [tok(d)]
    args = [x]
    if out_proj is not None:
        ysf, ysb, z, gnw, yb, wo, gate_a = out_proj
        in_specs += [tok(ysf.shape[-1]), tok(ysb.shape[-1]), tok(z.shape[-1]),
                     pl.BlockSpec(gnw.shape, lambda b, i: (0, 0)),
                     tok(yb.shape[-1]), _resident(wo.shape, lambda b, i: (0, 0)), vec]
        args += [ysf, ysb, z, gnw, yb, wo, gate_a]
    in_specs += [row, vec, vec, vec, layer_w(w1), layer_w(w2)]
    args += [nw, shift, scale, gate, w1, w2]
    if final_nw is not None:
        in_specs.append(row)
        args.append(final_nw)
    return pl.pallas_call(
        functools.partial(_mlp_kernel, ff_chunk, out_proj is not None, final_nw is not None),
        grid=(bsz, seq // tm),
        in_specs=in_specs,
        out_specs=tok(d),
        out_shape=jax.ShapeDtypeStruct((bsz, seq, d), F32),
        compiler_params=_cparams(("arbitrary", "arbitrary")),
        name="mlp_final" if final_nw is not None else "mlp",
    )(*args)


def _shortconv_kernel(n_tiles, x_ref, xp_ref, xn_ref, nw_ref, sh_ref, sc_ref, gate_ref,
                      win_ref, cw_ref, wout_ref, o_ref):
    i = pl.program_id(1)
    tm, d = x_ref.shape
    inner = wout_ref.shape[0]
    hb = SUBLANES
    x = x_ref[...]
    x_ext = jnp.concatenate([xp_ref[...], x, xn_ref[...]], axis=0)
    h = _rms_mod(x_ext, nw_ref[...], sh_ref[...], sc_ref[...]).astype(BF16)
    gate_c = _dot(h, win_ref[:, inner:2 * inner])
    val = _dot(h, win_ref[:, 2 * inner:])
    u = gate_c * val
    rows = lax.broadcasted_iota(jnp.int32, u.shape, 0)
    outside = ((rows < hb) & (i == 0)) | ((rows >= tm + hb) & (i == n_tiles - 1))
    u = jnp.where(outside, 0.0, u)
    ext = tm + 2 * hb
    conv = (pltpu.roll(u, 1, 0) * cw_ref[0:1, :] + u * cw_ref[1:2, :]
            + pltpu.roll(u, ext - 1, 0) * cw_ref[2:3, :])[hb:hb + tm, :]
    gate_b = _dot(h[hb:hb + tm, :], win_ref[:, :inner])
    y = _dot((gate_b * conv).astype(BF16), wout_ref[...])
    o_ref[...] = x + gate_ref[...] * y


def shortconv(x, nw, shift, scale, gate, w_in, conv_w, w_out, tm):
    bsz, seq, d = x.shape
    n_tiles = seq // tm
    per_tile = tm // SUBLANES
    n_halo = seq // SUBLANES
    tok = pl.BlockSpec((None, tm, d), lambda b, i: (b, i, 0))
    vec = pl.BlockSpec((None, 1, d), lambda b, i: (b, 0, 0))
    return pl.pallas_call(
        functools.partial(_shortconv_kernel, n_tiles),
        grid=(bsz, n_tiles),
        in_specs=[
            tok,
            pl.BlockSpec((None, SUBLANES, d), lambda b, i: (b, jnp.maximum(i * per_tile - 1, 0), 0)),
            pl.BlockSpec((None, SUBLANES, d), lambda b, i: (b, jnp.minimum((i + 1) * per_tile, n_halo - 1), 0)),
            pl.BlockSpec((1, d), lambda b, i: (0, 0)), vec, vec, vec,
            _resident(w_in.shape, lambda b, i: (0, 0)),
            pl.BlockSpec(conv_w.shape, lambda b, i: (0, 0)),
            _resident(w_out.shape, lambda b, i: (0, 0)),
        ],
        out_specs=tok,
        out_shape=jax.ShapeDtypeStruct((bsz, seq, d), F32),
        compiler_params=_cparams(("arbitrary", "arbitrary")),
        name="shortconv",
    )(x, x, x, nw, shift, scale, gate, w_in, conv_w, w_out)


def _pad_lanes(row):
    return jnp.zeros((1, LANES), F32).at[0, :row.shape[0]].set(row.astype(F32))


def kernel(x, c, ctx, c_ctx, mod_w, mod_b, norm_mix_w, norm_mlp_w, mlp_w1, mlp_w2, ssdna_in_w, ssdna_conv_w,
           ssdna_conv_b, ssd_dt_bias, ssd_a_log, ssd_d, ssd_norm_w, na_rpb, ssdna_out_w, sc_in_w, sc_conv_w,
           sc_out_w, final_norm_w):
    bsz, seq, d = x.shape
    n_ctx = ctx.shape[1]
    tm = min(512, seq)
    inner = SSD_HEADS * SSD_HEAD_DIM
    gn2 = SSD_GROUPS * SSD_STATE

    mrows = -(-(bsz + 1) // SUBLANES) * SUBLANES
    cc = jnp.zeros((mrows, d), F32).at[:bsz].set(c).at[bsz].set(c_ctx)
    mod = modvec(cc, mod_w, mod_b).reshape(mod_w.shape[0], mrows, 6, d)
    vecs = lambda i: [mod[i, :bsz, j].reshape(bsz, 1, d) for j in range(6)]
    row = lambda v: v.reshape(1, -1).astype(F32)

    shift_a, scale_a, gate_a, shift_f, scale_f, gate_f = vecs(0)
    shift_c = jnp.broadcast_to(mod[0, bsz, 0].reshape(1, 1, d), (bsz, 1, d))
    scale_c = jnp.broadcast_to(mod[0, bsz, 1].reshape(1, 1, d), (bsz, 1, d))

    o_dt, o_k = inner + gn2, inner + gn2 + 2 * SSD_HEADS
    o_v = o_k + d
    o_c = o_v + d
    o_z, o_q = o_c + gn2, o_c + gn2 + inner
    segments = [(0, o_dt, 1.0), (o_c, gn2, 1.0), (o_z, inner, 1.0),
                (o_q, d, NA_HEAD_DIM ** -0.5),
                (o_k, d, 1.0), (o_v, d, 1.0), (o_dt, SSD_HEADS, 1.0), (o_dt + SSD_HEADS, SSD_HEADS, 1.0)]
    w0 = reorder_columns(ssdna_in_w, 0, segments)
    nw0 = row(norm_mix_w[0])
    conv_w = ssdna_conv_w[0]
    conv_b = row(ssdna_conv_b[0])
    xs, bc, z, q_l, k_l, v_l, dt_l = inproj0(x, nw0, shift_a, scale_a, w0, conv_w, conv_b, tm)
    xs_c, bc_c, k_c, v_c, dt_c = inproj0(ctx, nw0, shift_c, scale_c, w0, conv_w, conv_b, n_ctx, latent=False)

    d_skip_row = row(jnp.repeat(ssd_d[0], SSD_HEAD_DIM))
    expand = (np.arange(LANES)[:, None] == (np.arange(inner)[None, :] // SSD_HEAD_DIM)).astype(np.float32)
    expand2 = jnp.asarray(np.concatenate([expand] * SPLIT_EXPAND, axis=0), BF16)
    pad2 = lambda p: jnp.concatenate([_pad_lanes(p[0]), _pad_lanes(p[1])], axis=0)
    common = (xs, bc, dt_l, xs_c, bc_c, dt_c, pad2(ssd_dt_bias[0]), pad2(ssd_a_log[0]), expand2)
    y_fwd = ssd_pass(False, *common, d_skip_row=d_skip_row)
    y_bwd = ssd_pass(True, *common)

    y_na = natten(q_l, k_l, v_l, k_c, v_c, natten_bias_table(na_rpb[0]))
    w1_all, w2_all = mlp_w1.astype(BF16), mlp_w2.astype(BF16)
    x = mlp(x, row(norm_mlp_w[0]), shift_f, scale_f, gate_f, w1_all, w2_all, 0, tm, 512,
            out_proj=(y_fwd, y_bwd, z, row(ssd_norm_w[0]), y_na, ssdna_out_w[0].astype(BF16), gate_a))

    shift_a, scale_a, gate_a, shift_f, scale_f, gate_f = vecs(1)
    x = shortconv(x, row(norm_mix_w[1]), shift_a, scale_a, gate_a, sc_in_w[0].astype(BF16), sc_conv_w[0],
                  sc_out_w[0].astype(BF16), tm)
    x = mlp(x, row(norm_mlp_w[1]), shift_f, scale_f, gate_f, w1_all, w2_all, 1, tm, 512,
            final_nw=row(final_norm_w))
    return x
```

```python
import functools

import numpy as np
import jax
import jax.numpy as jnp
from jax import lax
from jax.experimental import pallas as pl
from jax.experimental.pallas import tpu as pltpu

F32 = jnp.float32
BF16 = jnp.bfloat16

RMS_EPS = 1e-6
MASK_VALUE = -1e30
LOG2_E = 1.4426950408889634

GRID_W = 64
SSD_HEADS = 16
SSD_HEAD_DIM = 64
SSD_GROUPS = 2
SSD_STATE = 128
SSD_CHUNK = 128
NA_HEADS = 16
NA_HEAD_DIM = 64
NA_WIN_R = 8
NA_WIN_C = 16

LANES = 128
SUBLANES = 8
VMEM_LIMIT = 56 * 1024 * 1024

TOKEN_TILE = 512
FF_CHUNK = 512


def _cparams(semantics):
    return pltpu.CompilerParams(dimension_semantics=semantics, vmem_limit_bytes=VMEM_LIMIT)


def _resident(block_shape, index_map):
    return pl.BlockSpec(block_shape, index_map, pipeline_mode=pl.Buffered(1))


def _rms_mod(x, nw, shift, scale):
    ms = jnp.mean(x * x, axis=-1, keepdims=True)
    return (x * lax.rsqrt(ms + RMS_EPS)) * nw * (1.0 + scale) + shift


def _silu(x):
    return x * jax.nn.sigmoid(x)


def _softplus(x):
    return jnp.maximum(x, 0.0) + jnp.log1p(jnp.exp(-jnp.abs(x)))


def _dot(a, b):
    return jnp.dot(a, b, preferred_element_type=F32)


def _dot_nt(a, b):
    return lax.dot_general(a, b, (((1,), (1,)), ((), ())), preferred_element_type=F32)


def _split_bf16(a, parts):
    out = []
    r = a
    for _ in range(parts):
        h = r.astype(BF16)
        out.append(h)
        r = r - h.astype(F32)
    return out


def _modvec_kernel(c_ref, w_ref, b_ref, o_ref):
    s = _silu(c_ref[...]).astype(BF16)
    o_ref[...] = _dot(s, w_ref[...].astype(BF16)) + b_ref[...]


def modvec(cc, mod_w, mod_b, tn=1536):
    depth, d, n = mod_w.shape
    rows = cc.shape[0]
    return pl.pallas_call(
        _modvec_kernel,
        grid=(depth, n // tn),
        in_specs=[
            pl.BlockSpec((rows, d), lambda i, j: (0, 0)),
            pl.BlockSpec((None, d, tn), lambda i, j: (i, 0, j)),
            pl.BlockSpec((None, 1, tn), lambda i, j: (i, 0, j)),
        ],
        out_specs=pl.BlockSpec((None, rows, tn), lambda i, j: (i, 0, j)),
        out_shape=jax.ShapeDtypeStruct((depth, rows, n), F32),
        compiler_params=_cparams(("arbitrary", "arbitrary")),
        name="modvec",
    )(cc, mod_w, mod_b.reshape(depth, 1, n))


def _reorder_columns_kernel(segments, w_ref, o_ref):
    dst = 0
    for src, width, scale in segments:
        pad = -width % LANES
        v = w_ref[:, src:src + width]
        if scale != 1.0:
            v = v * scale
        o_ref[:, dst:dst + width] = v.astype(o_ref.dtype)
        if pad:
            o_ref[:, dst + width:dst + width + pad] = jnp.zeros((o_ref.shape[0], pad), o_ref.dtype)
        dst += width + pad


def reorder_columns(w, layer, segments, rows_per_step=128):
    _, k, n = w.shape
    n_out = sum(width + (-width % LANES) for _, width, _ in segments)
    return pl.pallas_call(
        functools.partial(_reorder_columns_kernel, tuple(segments)),
        grid=(k // rows_per_step,),
        in_specs=[pl.BlockSpec((None, rows_per_step, n), lambda i: (layer, i, 0))],
        out_specs=pl.BlockSpec((rows_per_step, n_out), lambda i: (i, 0)),
        out_shape=jax.ShapeDtypeStruct((k, n_out), BF16),
        compiler_params=_cparams(("arbitrary",)),
        name="reorder_columns",
    )(w)


SSD_INNER = SSD_HEADS * SSD_HEAD_DIM
SSD_BC = 2 * SSD_GROUPS * SSD_STATE
XBC_COLS = SSD_INNER + SSD_BC
DT_COLS = 2 * LANES


def _inproj0_kernel(n_tiles, latent, x_ref, xp_ref, xn_ref, nw_ref, sh_ref, sc_ref, w_ref, cw_ref, cb_ref,
                    xs_ref, bc_ref, *out_refs):
    i = pl.program_id(1)
    tm, d = x_ref.shape
    hb = SUBLANES
    ext = tm + 2 * hb
    x_ext = jnp.concatenate([xp_ref[...], x_ref[...], xn_ref[...]], axis=0)
    h_ext = _rms_mod(x_ext, nw_ref[...], sh_ref[...], sc_ref[...]).astype(BF16)
    u = _dot(h_ext, w_ref[:, :XBC_COLS])
    rows = lax.broadcasted_iota(jnp.int32, u.shape, 0)
    outside = ((rows < hb) & (i == 0)) | ((rows >= tm + hb) & (i == n_tiles - 1))
    u = jnp.where(outside, 0.0, u)
    xc = (pltpu.roll(u, 1, 0) * cw_ref[0:1, :] + u * cw_ref[1:2, :]
          + pltpu.roll(u, ext - 1, 0) * cw_ref[2:3, :])[hb:hb + tm, :] + cb_ref[...]
    xc = _silu(xc)
    xs_ref[...] = xc[:, :SSD_INNER]
    bc_ref[...] = xc[:, SSD_INNER:].astype(bc_ref.dtype)

    h = h_ext[hb:hb + tm, :]
    c0 = XBC_COLS if latent else XBC_COLS + 2 * d
    for ref in out_refs:
        width = ref.shape[-1] if len(ref.shape) == 2 else ref.shape[0] * ref.shape[-1]
        y = _dot(h, w_ref[:, c0:c0 + width]).astype(ref.dtype)
        if len(ref.shape) == 3:
            for p in range(ref.shape[0]):
                ref[p] = y[:, p * LANES:(p + 1) * LANES]
        else:
            ref[...] = y
        c0 += width


def inproj0(x, nw, shift, scale, w, conv_w, conv_b, tm, latent=True):
    bsz, seq, d = x.shape
    ncols = w.shape[1]
    npair = d // LANES
    n_tiles = seq // tm
    per_tile = tm // SUBLANES
    n_halo = seq // SUBLANES
    tok = lambda width: pl.BlockSpec((None, tm, width), lambda b, i: (b, i, 0))
    pair_major = pl.BlockSpec((None, npair, tm, LANES), lambda b, i: (b, 0, i, 0))
    vec = pl.BlockSpec((None, 1, d), lambda b, i: (b, 0, 0))
    const = lambda shape: pl.BlockSpec(shape, lambda b, i: (0,) * len(shape))
    flat = lambda width, dtype: (jax.ShapeDtypeStruct((bsz, seq, width), dtype), tok(width))
    pairs = (jax.ShapeDtypeStruct((bsz, npair, seq, LANES), BF16), pair_major)
    outs = [flat(SSD_INNER, F32), flat(SSD_BC, BF16)]
    if latent:
        outs += [flat(d, F32), pairs]
    outs += [pairs, flat(d, BF16), flat(DT_COLS, F32)]
    return pl.pallas_call(
        functools.partial(_inproj0_kernel, n_tiles, latent),
        grid=(bsz, n_tiles),
        in_specs=[tok(d),
                  pl.BlockSpec((None, SUBLANES, d), lambda b, i: (b, jnp.maximum(i * per_tile - 1, 0), 0)),
                  pl.BlockSpec((None, SUBLANES, d),
                               lambda b, i: (b, jnp.minimum((i + 1) * per_tile, n_halo - 1), 0)),
                  const((1, d)), vec, vec,
                  _resident((d, ncols), lambda b, i: (0, 0)),
                  const((3, XBC_COLS)), const((1, XBC_COLS))],
        out_specs=tuple(spec for _, spec in outs),
        out_shape=tuple(shape for shape, _ in outs),
        compiler_params=_cparams(("arbitrary", "arbitrary")),
        name="inproj0",
    )(x, x, x, nw, shift, scale, w, conv_w, conv_b)


SSD_CHUNKS_PER_STEP = 2
SPLIT_CUMSUM = 3
SPLIT_EXPAND = 2


def _ssd_kernel(backward, n_ctx_steps, n_lat_steps, *refs):
    if backward:
        (xs_ref, bc_ref, dt_ref, xsc_ref, bcc_ref, dtc_ref, dtb_ref, alog_ref, e_ref,
         y_ref, state_ref) = refs
    else:
        (xs_ref, bc_ref, dt_ref, xsc_ref, bcc_ref, dtc_ref, dtb_ref, alog_ref, e_ref,
         dsk_ref, y_ref, state_ref) = refs
    q = SSD_CHUNK
    nch = SSD_CHUNKS_PER_STEP
    inner = SSD_INNER
    gn = SSD_STATE
    gcols = inner // SSD_GROUPS
    heads_per_group = SSD_HEADS // SSD_GROUPS
    c = pl.program_id(1)
    scan_dir = 1 if backward else 0

    ri = lax.broadcasted_iota(jnp.int32, (q, q), 0)
    ci = lax.broadcasted_iota(jnp.int32, (q, q), 1)
    contributes = (ri >= ci, ri <= ci)
    tri = [jnp.concatenate([jnp.where(m, 1.0, 0.0).astype(BF16)] * SPLIT_CUMSUM, axis=1) for m in contributes]
    lane = lax.broadcasted_iota(jnp.int32, (q, LANES), 1)
    low_half = lane < SSD_HEAD_DIM

    def decay_terms(dtraw, direction):
        cols = slice(direction * LANES, (direction + 1) * LANES)
        dt = _softplus(dtraw[:, cols] + dtb_ref[direction:direction + 1, :])
        a = dt * (-jnp.exp(alog_ref[direction:direction + 1, :]))
        a_cum = _dot(tri[direction], jnp.concatenate(_split_bf16(a, SPLIT_CUMSUM), axis=0))
        return dt, a_cum

    def expand(*vs):
        lhs = jnp.concatenate([jnp.concatenate(_split_bf16(v, SPLIT_EXPAND), axis=1) for v in vs], axis=0)
        full = _dot(lhs, e_ref[...])
        return [full[i * q:(i + 1) * q, :] for i in range(len(vs))]

    def process(x, bc16, dtraw, out_rows):
        want_y = out_rows is not None
        bm16 = bc16[:, :SSD_GROUPS * gn]
        cm16 = bc16[:, SSD_GROUPS * gn:]

        dt, a_cum = decay_terms(dtraw, scan_dir)
        edge = a_cum[0:1, :] if backward else a_cum[q - 1:q, :]
        ea_full, dtw_full = expand(jnp.exp(a_cum), dt * jnp.exp(edge - a_cum))
        chunk_decay = ea_full[0:1, :] if backward else ea_full[q - 1:q, :]
        xw = (x * dtw_full).astype(BF16)
        bts = [bm16[:, g * gn:(g + 1) * gn].astype(F32).T.astype(BF16) for g in range(SSD_GROUPS)]

        y_part = None
        if want_y and not backward:
            x16 = x.astype(BF16)
            dirs = [(d_, ac * LOG2_E) for d_, ac in ((dt, a_cum), decay_terms(dtraw, 1))]
            tr = [(d_.T, ac.T) for d_, ac in dirs]
            ys = []
            for g in range(SSD_GROUPS):
                cb = _dot_nt(cm16[:, g * gn:(g + 1) * gn], bm16[:, g * gn:(g + 1) * gn])
                for pair in range(heads_per_group // 2):
                    ms = []
                    for hh in range(2):
                        h = g * heads_per_group + pair * 2 + hh
                        w = None
                        for direction in range(2):
                            ac, (dt_t, ac_t) = dirs[direction][1], tr[direction]
                            seg = ac[:, h:h + 1] - ac_t[h:h + 1, :]
                            term = jnp.exp2(jnp.where(contributes[direction], seg, -jnp.inf)) * dt_t[h:h + 1, :]
                            w = term if w is None else w + term
                        ms.append((cb * w).astype(BF16))
                    m_pair = jnp.concatenate(ms, axis=1)
                    col0 = (g * heads_per_group + pair * 2) * SSD_HEAD_DIM
                    xp = x16[:, col0:col0 + LANES]
                    zero = jnp.zeros_like(xp)
                    x_bd = jnp.concatenate([jnp.where(low_half, xp, zero),
                                            jnp.where(low_half, zero, xp)], axis=0)
                    ys.append(_dot(m_pair, x_bd))
            y_part = jnp.concatenate(ys, axis=1) + x * dsk_ref[...]

        def finish():
            if want_y:
                y_off = jnp.concatenate(
                    [_dot(cm16[:, g * gn:(g + 1) * gn], state_ref[:, g * gcols:(g + 1) * gcols].astype(BF16))
                     for g in range(SSD_GROUPS)], axis=1) * ea_full
                y_ref[out_rows, :] = y_off if y_part is None else y_part + y_off
            for g in range(SSD_GROUPS):
                sl = slice(g * gcols, (g + 1) * gcols)
                state_ref[:, sl] = state_ref[:, sl] * chunk_decay[:, sl] + _dot(bts[g], xw[:, sl])

        return finish

    def run_block(x_ref_, bc_ref_, dt_ref_, r0, is_latent):
        order = range(nch - 1, -1, -1) if backward else range(nch)
        finishers = []
        for k in order:
            rows = slice(r0 + k * q, r0 + (k + 1) * q)
            out_rows = slice(k * q, (k + 1) * q) if is_latent else None
            finishers.append(process(x_ref_[rows, :], bc_ref_[rows, :], dt_ref_[rows, :], out_rows))
        for fin in finishers:
            fin()

    @pl.when(c == 0)
    def _():
        state_ref[...] = jnp.zeros_like(state_ref)

    blk = nch * q
    for step in range(n_ctx_steps):
        cs = (n_ctx_steps - 1 - step) if backward else step

        @pl.when(c == step)
        def _(cs=cs):
            run_block(xsc_ref, bcc_ref, dtc_ref, cs * blk, False)

    @pl.when(c >= n_ctx_steps)
    def _():
        run_block(xs_ref, bc_ref, dt_ref, 0, True)


def ssd_pass(backward, xs, bc, dt, xs_c, bc_c, dt_c, dt_bias_rows, a_log_rows, expand2,
             d_skip_row=None):
    bsz, seq, inner = xs.shape
    n_ctx = xs_c.shape[1]
    blk = SSD_CHUNK * SSD_CHUNKS_PER_STEP
    assert n_ctx % blk == 0 and seq % blk == 0
    ncs, nls = n_ctx // blk, seq // blk

    def lat(c):
        ls = jnp.maximum(c - ncs, 0)
        return (nls - 1 - ls) if backward else ls

    block = lambda width: pl.BlockSpec((None, blk, width), lambda b, c: (b, lat(c), 0))
    whole = lambda width: pl.BlockSpec((None, n_ctx, width), lambda b, c: (b, 0, 0))
    const = lambda shape: pl.BlockSpec(shape, lambda b, c: (0,) * len(shape))
    in_specs = [
        block(inner), block(SSD_BC), block(DT_COLS),
        whole(inner), whole(SSD_BC), whole(DT_COLS),
        const((2, LANES)), const((2, LANES)), const(expand2.shape),
    ]
    args = [xs, bc, dt, xs_c, bc_c, dt_c, dt_bias_rows, a_log_rows, expand2]
    if not backward:
        in_specs += [const((1, inner))]
        args += [d_skip_row]
    return pl.pallas_call(
        functools.partial(_ssd_kernel, backward, ncs, nls),
        grid=(bsz, ncs + nls),
        in_specs=in_specs,
        out_specs=block(inner),
        out_shape=jax.ShapeDtypeStruct((bsz, seq, inner), F32),
        scratch_shapes=[pltpu.VMEM((SSD_STATE, inner), F32)],
        compiler_params=_cparams(("arbitrary", "arbitrary")),
        name="ssd_bwd" if backward else "ssd_fwd",
    )(*args)


NA_GROUP_HEADS = 4
NA_ROWS_PER_STEP = 4


def _natten_kernel(grid_rows, q_ref, k_ref, v_ref, kc_ref, vc_ref, bias_ref, o_ref):
    w = GRID_W
    nwin = NA_WIN_R * w
    hg = NA_GROUP_HEADS
    gd = hg * NA_HEAD_DIM
    ngroups = NA_HEADS // hg
    nrows = NA_ROWS_PER_STEP
    lane_head = lax.broadcasted_iota(jnp.int32, (w, gd), 1) // NA_HEAD_DIM
    low_half = lax.broadcasted_iota(jnp.int32, (w, LANES), 1) < NA_HEAD_DIM
    r_base = pl.program_id(1) * nrows
    rs = [jnp.clip(r_base + i - NA_WIN_R // 2, 0, grid_rows - NA_WIN_R) for i in range(nrows)]
    k0 = [pl.multiple_of(rs[i] * w, w) for i in range(nrows)]
    dr0 = [rs[i] - (r_base + i) + NA_WIN_R - 1 for i in range(nrows)]

    def scores(g):
        s_win = [[] for _ in range(nrows)]
        s_ctx = [[] for _ in range(nrows)]
        for pp in range(hg // 2):
            pair = g * (hg // 2) + pp
            wqs = []
            for i in range(nrows):
                qp = q_ref[pair, i * w:(i + 1) * w, :]
                zero = jnp.zeros_like(qp)
                wqs.append(jnp.concatenate([jnp.where(low_half, qp, zero), jnp.where(low_half, zero, qp)], axis=0))
            sc = _dot_nt(jnp.concatenate(wqs, axis=0), kc_ref[pair])
            for i in range(nrows):
                s_win[i].append(_dot_nt(wqs[i], k_ref[pair, pl.ds(k0[i], nwin), :]))
                s_ctx[i].append(sc[i * 2 * w:(i + 1) * 2 * w, :])
        return [(jnp.concatenate(s_win[i], axis=0), jnp.concatenate(s_ctx[i], axis=0)) for i in range(nrows)]

    def softmax(g, i, s):
        s_win, s_ctx = s
        s_win = jnp.concatenate(
            [s_win[:, t * LANES:(t + 1) * LANES] + bias_ref[g, dr0[i] + 2 * t]
             for t in range(nwin // LANES)], axis=1)
        m = jnp.maximum(jnp.max(s_win, axis=-1, keepdims=True), jnp.max(s_ctx, axis=-1, keepdims=True))
        p_win = jnp.exp(s_win - m)
        p_ctx = jnp.exp(s_ctx - m)
        denom = jnp.sum(p_win, axis=-1, keepdims=True) + jnp.sum(p_ctx, axis=-1, keepdims=True)
        return p_win.astype(BF16), p_ctx.astype(BF16), denom

    def values(g, ps):
        cols = slice(g * gd, (g + 1) * gd)
        o_ctx = _dot(jnp.concatenate([p[1] for p in ps], axis=0), vc_ref[:, cols])
        for i in range(nrows):
            p_win, _, denom = ps[i]
            o = (_dot(p_win, v_ref[pl.ds(k0[i], nwin), cols]) + o_ctx[i * hg * w:(i + 1) * hg * w, :]) / denom
            acc = o[:w, :]
            for hh in range(1, hg):
                acc = jnp.where(lane_head == hh, o[hh * w:(hh + 1) * w, :], acc)
            o_ref[i * w:(i + 1) * w, cols] = acc.astype(o_ref.dtype)

    s, p = {}, {}
    for step in range(ngroups + 2):
        if step < ngroups:
            s[step] = scores(step)
        if 0 <= step - 1 < ngroups:
            sg = s.pop(step - 1)
            p[step - 1] = [softmax(step - 1, i, sg[i]) for i in range(nrows)]
        if 0 <= step - 2 < ngroups:
            values(step - 2, p.pop(step - 2))


def natten(q, k, v, kc, vc, bias2):
    bsz, seq, d = v.shape
    n_ctx = vc.shape[1]
    grid_rows = seq // GRID_W
    npair = NA_HEADS // 2
    rows_tok = NA_ROWS_PER_STEP * GRID_W
    return pl.pallas_call(
        functools.partial(_natten_kernel, grid_rows),
        grid=(bsz, grid_rows // NA_ROWS_PER_STEP),
        in_specs=[
            pl.BlockSpec((None, npair, rows_tok, LANES), lambda b, r: (b, 0, r, 0)),
            pl.BlockSpec((None, npair, seq, LANES), lambda b, r: (b, 0, 0, 0)),
            pl.BlockSpec((None, seq, d), lambda b, r: (b, 0, 0)),
            _resident((None, npair, n_ctx, LANES), lambda b, r: (b, 0, 0, 0)),
            _resident((None, n_ctx, d), lambda b, r: (b, 0, 0)),
            _resident(bias2.shape, lambda b, r: (0, 0, 0, 0)),
        ],
        out_specs=pl.BlockSpec((None, rows_tok, d), lambda b, r: (b, r, 0)),
        out_shape=jax.ShapeDtypeStruct((bsz, seq, d), BF16),
        compiler_params=_cparams(("arbitrary", "arbitrary")),
        name="natten",
    )(q, k, v, kc, vc, bias2)


def natten_bias_table(rpb):
    w = GRID_W
    hg = NA_GROUP_HEADS
    qc = np.arange(w)[:, None]
    kc = np.arange(w)[None, :]
    win_start = np.clip(qc - NA_WIN_C // 2, 0, w - NA_WIN_C)
    col_ok = (kc >= win_start) & (kc < win_start + NA_WIN_C)
    dc_idx = np.clip(kc - qc, -(NA_WIN_C - 1), NA_WIN_C - 1) + NA_WIN_C - 1
    ndc = 2 * NA_WIN_C - 1
    onehot = (np.arange(ndc)[:, None, None] == dc_idx[None]).astype(np.float32)
    ndr = 2 * NA_WIN_R - 2
    nr = 2 * NA_WIN_R - 1
    t = jnp.einsum("hrc,cqk->hrqk", rpb.astype(F32), jnp.asarray(onehot), precision=lax.Precision.HIGHEST)
    t = jnp.where(col_ok[None, None], t, MASK_VALUE)

    def pair_rows_kernel(t_ref, o_ref):
        for dr in range(ndr):
            for hh in range(hg):
                o_ref[dr, hh * w:(hh + 1) * w, :] = jnp.concatenate([t_ref[hh, dr], t_ref[hh, dr + 1]], axis=1)

    return pl.pallas_call(
        pair_rows_kernel,
        grid=(NA_HEADS // hg,),
        in_specs=[pl.BlockSpec((hg, nr, w, w), lambda g: (g, 0, 0, 0))],
        out_specs=pl.BlockSpec((None, ndr, hg * w, 2 * w), lambda g: (g, 0, 0, 0)),
        out_shape=jax.ShapeDtypeStruct((NA_HEADS // hg, ndr, hg * w, 2 * w), F32),
        compiler_params=_cparams(("arbitrary",)),
        name="natten_bias_pairs",
    )(t)


def _mlp_kernel(ff_chunk, out_proj, final_norm, *refs):
    refs = list(refs)
    x = refs.pop(0)[...]
    if out_proj:
        ysf_ref, ysb_ref, z_ref, gnw_ref, yb_ref, wo_ref, gate_a_ref = refs[:7]
        del refs[:7]
        ka = ysf_ref.shape[-1]
        ys = (ysf_ref[...] + ysb_ref[...]) * _silu(z_ref[...])
        ys = (ys * lax.rsqrt(jnp.mean(ys * ys, axis=-1, keepdims=True) + RMS_EPS)) * gnw_ref[...]
        x = x + gate_a_ref[...] * (_dot(ys.astype(BF16), wo_ref[:ka, :]) + _dot(yb_ref[...], wo_ref[ka:, :]))
    nw_ref, sh_ref, sc_ref, gate_ref, w1_ref, w2_ref = refs[:6]
    fnw_ref = refs[6] if final_norm else None
    o_ref = refs[-1]
    h = _rms_mod(x, nw_ref[...], sh_ref[...], sc_ref[...]).astype(BF16)
    dff = w1_ref.shape[1]
    acc = None
    for c0 in range(0, dff, ff_chunk):
        a = jnp.maximum(_dot(h, w1_ref[:, c0:c0 + ff_chunk]), 0.0)
        part = _dot((a * a).astype(BF16), w2_ref[c0:c0 + ff_chunk, :])
        acc = part if acc is None else acc + part
    y = x + gate_ref[...] * acc
    if final_norm:
        ms = jnp.mean(y * y, axis=-1, keepdims=True)
        y = (y * lax.rsqrt(ms + RMS_EPS)) * fnw_ref[...]
    o_ref[...] = y


def mlp(x, nw, shift, scale, gate, w1, w2, layer, tm, ff_chunk, final_nw=None, out_proj=None):
    bsz, seq, d = x.shape
    tok = lambda width: pl.BlockSpec((None, tm, width), lambda b, i: (b, i, 0))
    vec = pl.BlockSpec((None, 1, d), lambda b, i: (b, 0, 0))
    row = pl.BlockSpec((1, d), lambda b, i: (0, 0))
    layer_w = lambda w: _resident((None,) + w.shape[1:], lambda b, i: (layer, 0, 0))
    in_specs = [tok(d)]
    args = [x]
    if out_proj is not None:
        ysf, ysb, z, gnw, yb, wo, gate_a = out_proj
        in_specs += [tok(ysf.shape[-1]), tok(ysb.shape[-1]), tok(z.shape[-1]),
                     pl.BlockSpec(gnw.shape, lambda b, i: (0, 0)),
                     tok(yb.shape[-1]), _resident(wo.shape, lambda b, i: (0, 0)), vec]
        args += [ysf, ysb, z, gnw, yb, wo, gate_a]
    in_specs += [row, vec, vec, vec, layer_w(w1), layer_w(w2)]
    args += [nw, shift, scale, gate, w1, w2]
    if final_nw is not None:
        in_specs.append(row)
        args.append(final_nw)
    return pl.pallas_call(
        functools.partial(_mlp_kernel, ff_chunk, out_proj is not None, final_nw is not None),
        grid=(bsz, seq // tm),
        in_specs=in_specs,
        out_specs=tok(d),
        out_shape=jax.ShapeDtypeStruct((bsz, seq, d), F32),
        compiler_params=_cparams(("arbitrary", "arbitrary")),
        name="mlp_final" if final_nw is not None else "mlp",
    )(*args)


def _shortconv_kernel(n_tiles, x_ref, xp_ref, xn_ref, nw_ref, sh_ref, sc_ref, gate_ref,
                      win_ref, cw_ref, wout_ref, o_ref):
    i = pl.program_id(1)
    tm, d = x_ref.shape
    inner = wout_ref.shape[0]
    hb = SUBLANES
    x = x_ref[...]
    x_ext = jnp.concatenate([xp_ref[...], x, xn_ref[...]], axis=0)
    h = _rms_mod(x_ext, nw_ref[...], sh_ref[...], sc_ref[...]).astype(BF16)
    gate_c = _dot(h, win_ref[:, inner:2 * inner])
    val = _dot(h, win_ref[:, 2 * inner:])
    u = gate_c * val
    rows = lax.broadcasted_iota(jnp.int32, u.shape, 0)
    outside = ((rows < hb) & (i == 0)) | ((rows >= tm + hb) & (i == n_tiles - 1))
    u = jnp.where(outside, 0.0, u)
    ext = tm + 2 * hb
    conv = (pltpu.roll(u, 1, 0) * cw_ref[0:1, :] + u * cw_ref[1:2, :]
            + pltpu.roll(u, ext - 1, 0) * cw_ref[2:3, :])[hb:hb + tm, :]
    gate_b = _dot(h[hb:hb + tm, :], win_ref[:, :inner])
    y = _dot((gate_b * conv).astype(BF16), wout_ref[...])
    o_ref[...] = x + gate_ref[...] * y


def shortconv(x, nw, shift, scale, gate, w_in, conv_w, w_out, tm):
    bsz, seq, d = x.shape
    n_tiles = seq // tm
    per_tile = tm // SUBLANES
    n_halo = seq // SUBLANES
    tok = pl.BlockSpec((None, tm, d), lambda b, i: (b, i, 0))
    vec = pl.BlockSpec((None, 1, d), lambda b, i: (b, 0, 0))
    return pl.pallas_call(
        functools.partial(_shortconv_kernel, n_tiles),
        grid=(bsz, n_tiles),
        in_specs=[
            tok,
            pl.BlockSpec((None, SUBLANES, d), lambda b, i: (b, jnp.maximum(i * per_tile - 1, 0), 0)),
            pl.BlockSpec((None, SUBLANES, d), lambda b, i: (b, jnp.minimum((i + 1) * per_tile, n_halo - 1), 0)),
            pl.BlockSpec((1, d), lambda b, i: (0, 0)), vec, vec, vec,
            _resident(w_in.shape, lambda b, i: (0, 0)),
            pl.BlockSpec(conv_w.shape, lambda b, i: (0, 0)),
            _resident(w_out.shape, lambda b, i: (0, 0)),
        ],
        out_specs=tok,
        out_shape=jax.ShapeDtypeStruct((bsz, seq, d), F32),
        compiler_params=_cparams(("arbitrary", "arbitrary")),
        name="shortconv",
    )(x, x, x, nw, shift, scale, gate, w_in, conv_w, w_out)


def _pad_lanes(row):
    return jnp.zeros((1, LANES), F32).at[0, :row.shape[0]].set(row.astype(F32))


def kernel(x, c, ctx, c_ctx, mod_w, mod_b, norm_mix_w, norm_mlp_w, mlp_w1, mlp_w2, ssdna_in_w, ssdna_conv_w,
           ssdna_conv_b, ssd_dt_bias, ssd_a_log, ssd_d, ssd_norm_w, na_rpb, ssdna_out_w, sc_in_w, sc_conv_w,
           sc_out_w, final_norm_w):
    bsz, seq, d = x.shape
    n_ctx = ctx.shape[1]
    tm = min(TOKEN_TILE, seq)
    inner = SSD_HEADS * SSD_HEAD_DIM
    gn2 = SSD_GROUPS * SSD_STATE

    mrows = -(-(bsz + 1) // SUBLANES) * SUBLANES
    cc = jnp.zeros((mrows, d), F32).at[:bsz].set(c).at[bsz].set(c_ctx)
    mod = modvec(cc, mod_w, mod_b).reshape(mod_w.shape[0], mrows, 6, d)
    vecs = lambda i: [mod[i, :bsz, j].reshape(bsz, 1, d) for j in range(6)]
    row = lambda v: v.reshape(1, -1).astype(F32)

    shift_a, scale_a, gate_a, shift_f, scale_f, gate_f = vecs(0)
    shift_c = jnp.broadcast_to(mod[0, bsz, 0].reshape(1, 1, d), (bsz, 1, d))
    scale_c = jnp.broadcast_to(mod[0, bsz, 1].reshape(1, 1, d), (bsz, 1, d))

    o_dt, o_k = inner + gn2, inner + gn2 + 2 * SSD_HEADS
    o_v = o_k + d
    o_c = o_v + d
    o_z, o_q = o_c + gn2, o_c + gn2 + inner
    segments = [(0, o_dt, 1.0), (o_c, gn2, 1.0), (o_z, inner, 1.0),
                (o_q, d, NA_HEAD_DIM ** -0.5),
                (o_k, d, 1.0), (o_v, d, 1.0), (o_dt, SSD_HEADS, 1.0), (o_dt + SSD_HEADS, SSD_HEADS, 1.0)]
    w0 = reorder_columns(ssdna_in_w, 0, segments)
    nw0 = row(norm_mix_w[0])
    conv_w = ssdna_conv_w[0]
    conv_b = row(ssdna_conv_b[0])
    xs, bc, z, q_l, k_l, v_l, dt_l = inproj0(x, nw0, shift_a, scale_a, w0, conv_w, conv_b, tm)
    xs_c, bc_c, k_c, v_c, dt_c = inproj0(ctx, nw0, shift_c, scale_c, w0, conv_w, conv_b, n_ctx, latent=False)

    d_skip_row = row(jnp.repeat(ssd_d[0], SSD_HEAD_DIM))
    expand = (np.arange(LANES)[:, None] == (np.arange(inner)[None, :] // SSD_HEAD_DIM)).astype(np.float32)
    expand2 = jnp.asarray(np.concatenate([expand] * SPLIT_EXPAND, axis=0), BF16)
    pad2 = lambda p: jnp.concatenate([_pad_lanes(p[0]), _pad_lanes(p[1])], axis=0)
    common = (xs, bc, dt_l, xs_c, bc_c, dt_c, pad2(ssd_dt_bias[0]), pad2(ssd_a_log[0]), expand2)
    y_fwd = ssd_pass(False, *common, d_skip_row=d_skip_row)
    y_bwd = ssd_pass(True, *common)

    y_na = natten(q_l, k_l, v_l, k_c, v_c, natten_bias_table(na_rpb[0]))
    w1_all, w2_all = mlp_w1.astype(BF16), mlp_w2.astype(BF16)
    x = mlp(x, row(norm_mlp_w[0]), shift_f, scale_f, gate_f, w1_all, w2_all, 0, tm, FF_CHUNK,
            out_proj=(y_fwd, y_bwd, z, row(ssd_norm_w[0]), y_na, ssdna_out_w[0].astype(BF16), gate_a))

    shift_a, scale_a, gate_a, shift_f, scale_f, gate_f = vecs(1)
    x = shortconv(x, row(norm_mix_w[1]), shift_a, scale_a, gate_a, sc_in_w[0].astype(BF16), sc_conv_w[0],
                  sc_out_w[0].astype(BF16), tm)
    x = mlp(x, row(norm_mlp_w[1]), shift_f, scale_f, gate_f, w1_all, w2_all, 1, tm, FF_CHUNK,
            final_nw=row(final_norm_w))
    return x
```

```python
import functools

import numpy as np
import jax
import jax.numpy as jnp
from jax import lax
from jax.experimental import pallas as pl
from jax.experimental.pallas import tpu as pltpu

F32 = jnp.float32
BF16 = jnp.bfloat16

RMS_EPS = 1e-6
MASK_VALUE = -1e30
LOG2_E = 1.4426950408889634

GRID_W = 64
SSD_HEADS = 16
SSD_HEAD_DIM = 64
SSD_GROUPS = 2
SSD_STATE = 128
SSD_CHUNK = 128
NA_HEADS = 16
NA_HEAD_DIM = 64
NA_WIN_R = 8
NA_WIN_C = 16

LANES = 128
SUBLANES = 8
VMEM_LIMIT = 56 * 1024 * 1024

TOKEN_TILE = 512
FF_CHUNK = 512


def _cparams(semantics):
    return pltpu.CompilerParams(dimension_semantics=semantics, vmem_limit_bytes=VMEM_LIMIT)


def _resident(block_shape, index_map):
    return pl.BlockSpec(block_shape, index_map, pipeline_mode=pl.Buffered(1))


def _rms_mod(x, nw, shift, scale):
    ms = jnp.mean(x * x, axis=-1, keepdims=True)
    return (x * lax.rsqrt(ms + RMS_EPS)) * nw * (1.0 + scale) + shift


def _silu(x):
    return x * jax.nn.sigmoid(x)


def _softplus(x):
    return jnp.maximum(x, 0.0) + jnp.log1p(jnp.exp(-jnp.abs(x)))


def _dot(a, b):
    return jnp.dot(a, b, preferred_element_type=F32)


def _dot_nt(a, b):
    return lax.dot_general(a, b, (((1,), (1,)), ((), ())), preferred_element_type=F32)


def _split_bf16(a, parts):
    out = []
    r = a
    for _ in range(parts):
        h = r.astype(BF16)
        out.append(h)
        r = r - h.astype(F32)
    return out


def _modvec_kernel(c_ref, w_ref, b_ref, o_ref):
    s = _silu(c_ref[...]).astype(BF16)
    o_ref[...] = _dot(s, w_ref[...].astype(BF16)) + b_ref[...]


def modvec(cc, mod_w, mod_b, tn=1536):
    depth, d, n = mod_w.shape
    rows = cc.shape[0]
    return pl.pallas_call(
        _modvec_kernel,
        grid=(depth, n // tn),
        in_specs=[
            pl.BlockSpec((rows, d), lambda i, j: (0, 0)),
            pl.BlockSpec((None, d, tn), lambda i, j: (i, 0, j)),
            pl.BlockSpec((None, 1, tn), lambda i, j: (i, 0, j)),
        ],
        out_specs=pl.BlockSpec((None, rows, tn), lambda i, j: (i, 0, j)),
        out_shape=jax.ShapeDtypeStruct((depth, rows, n), F32),
        compiler_params=_cparams(("arbitrary", "arbitrary")),
        name="modvec",
    )(cc, mod_w, mod_b.reshape(depth, 1, n))


def _reorder_columns_kernel(segments, w_ref, o_ref):
    dst = 0
    for src, width, scale in segments:
        pad = -width % LANES
        v = w_ref[:, src:src + width]
        if scale != 1.0:
            v = v * scale
        o_ref[:, dst:dst + width] = v.astype(o_ref.dtype)
        if pad:
            o_ref[:, dst + width:dst + width + pad] = jnp.zeros((o_ref.shape[0], pad), o_ref.dtype)
        dst += width + pad


def reorder_columns(w, layer, segments, rows_per_step=128):
    _, k, n = w.shape
    n_out = sum(width + (-width % LANES) for _, width, _ in segments)
    return pl.pallas_call(
        functools.partial(_reorder_columns_kernel, tuple(segments)),
        grid=(k // rows_per_step,),
        in_specs=[pl.BlockSpec((None, rows_per_step, n), lambda i: (layer, i, 0))],
        out_specs=pl.BlockSpec((rows_per_step, n_out), lambda i: (i, 0)),
        out_shape=jax.ShapeDtypeStruct((k, n_out), BF16),
        compiler_params=_cparams(("arbitrary",)),
        name="reorder_columns",
    )(w)


SSD_INNER = SSD_HEADS * SSD_HEAD_DIM
SSD_BC = 2 * SSD_GROUPS * SSD_STATE
XBC_COLS = SSD_INNER + SSD_BC
DT_COLS = 2 * LANES


def _inproj0_kernel(n_tiles, latent, x_ref, xp_ref, xn_ref, nw_ref, sh_ref, sc_ref, w_ref, cw_ref, cb_ref,
                    xs_ref, bc_ref, *out_refs):
    i = pl.program_id(1)
    tm, d = x_ref.shape
    hb = SUBLANES
    ext = tm + 2 * hb
    x_ext = jnp.concatenate([xp_ref[...], x_ref[...], xn_ref[...]], axis=0)
    h_ext = _rms_mod(x_ext, nw_ref[...], sh_ref[...], sc_ref[...]).astype(BF16)
    u = _dot(h_ext, w_ref[:, :XBC_COLS])
    rows = lax.broadcasted_iota(jnp.int32, u.shape, 0)
    outside = ((rows < hb) & (i == 0)) | ((rows >= tm + hb) & (i == n_tiles - 1))
    u = jnp.where(outside, 0.0, u)
    xc = (pltpu.roll(u, 1, 0) * cw_ref[0:1, :] + u * cw_ref[1:2, :]
          + pltpu.roll(u, ext - 1, 0) * cw_ref[2:3, :])[hb:hb + tm, :] + cb_ref[...]
    xc = _silu(xc)
    xs_ref[...] = xc[:, :SSD_INNER]
    bc_ref[...] = xc[:, SSD_INNER:].astype(bc_ref.dtype)

    h = h_ext[hb:hb + tm, :]
    c0 = XBC_COLS if latent else XBC_COLS + 2 * d
    for ref in out_refs:
        width = ref.shape[-1] if len(ref.shape) == 2 else ref.shape[0] * ref.shape[-1]
        y = _dot(h, w_ref[:, c0:c0 + width]).astype(ref.dtype)
        if len(ref.shape) == 3:
            for p in range(ref.shape[0]):
                ref[p] = y[:, p * LANES:(p + 1) * LANES]
        else:
            ref[...] = y
        c0 += width


def inproj0(x, nw, shift, scale, w, conv_w, conv_b, tm, latent=True):
    bsz, seq, d = x.shape
    ncols = w.shape[1]
    npair = d // LANES
    n_tiles = seq // tm
    per_tile = tm // SUBLANES
    n_halo = seq // SUBLANES
    tok = lambda width: pl.BlockSpec((None, tm, width), lambda b, i: (b, i, 0))
    pair_major = pl.BlockSpec((None, npair, tm, LANES), lambda b, i: (b, 0, i, 0))
    vec = pl.BlockSpec((None, 1, d), lambda b, i: (b, 0, 0))
    const = lambda shape: pl.BlockSpec(shape, lambda b, i: (0,) * len(shape))
    flat = lambda width, dtype: (jax.ShapeDtypeStruct((bsz, seq, width), dtype), tok(width))
    pairs = (jax.ShapeDtypeStruct((bsz, npair, seq, LANES), BF16), pair_major)
    outs = [flat(SSD_INNER, F32), flat(SSD_BC, BF16)]
    if latent:
        outs += [flat(d, F32), pairs]
    outs += [pairs, flat(d, BF16), flat(DT_COLS, F32)]
    return pl.pallas_call(
        functools.partial(_inproj0_kernel, n_tiles, latent),
        grid=(bsz, n_tiles),
        in_specs=[tok(d),
                  pl.BlockSpec((None, SUBLANES, d), lambda b, i: (b, jnp.maximum(i * per_tile - 1, 0), 0)),
                  pl.BlockSpec((None, SUBLANES, d),
                               lambda b, i: (b, jnp.minimum((i + 1) * per_tile, n_halo - 1), 0)),
                  const((1, d)), vec, vec,
                  _resident((d, ncols), lambda b, i: (0, 0)),
                  const((3, XBC_COLS)), const((1, XBC_COLS))],
        out_specs=tuple(spec for _, spec in outs),
        out_shape=tuple(shape for shape, _ in outs),
        compiler_params=_cparams(("arbitrary", "arbitrary")),
        name="inproj0",
    )(x, x, x, nw, shift, scale, w, conv_w, conv_b)


SSD_CHUNKS_PER_STEP = 2
SPLIT_CUMSUM = 3
SPLIT_EXPAND = 2


def _ssd_kernel(backward, n_ctx_steps, n_lat_steps, *refs):
    if backward:
        (xs_ref, bc_ref, dt_ref, xsc_ref, bcc_ref, dtc_ref, dtb_ref, alog_ref, e_ref,
         y_ref, state_ref) = refs
    else:
        (xs_ref, bc_ref, dt_ref, xsc_ref, bcc_ref, dtc_ref, dtb_ref, alog_ref, e_ref,
         dsk_ref, y_ref, state_ref) = refs
    q = SSD_CHUNK
    nch = SSD_CHUNKS_PER_STEP
    inner = SSD_INNER
    gn = SSD_STATE
    gcols = inner // SSD_GROUPS
    heads_per_group = SSD_HEADS // SSD_GROUPS
    c = pl.program_id(1)
    scan_dir = 1 if backward else 0

    ri = lax.broadcasted_iota(jnp.int32, (q, q), 0)
    ci = lax.broadcasted_iota(jnp.int32, (q, q), 1)
    contributes = (ri >= ci, ri <= ci)
    tri = [jnp.concatenate([jnp.where(m, 1.0, 0.0).astype(BF16)] * SPLIT_CUMSUM, axis=1) for m in contributes]
    lane = lax.broadcasted_iota(jnp.int32, (q, LANES), 1)
    low_half = lane < SSD_HEAD_DIM

    def decay_terms(dtraw, direction):
        cols = slice(direction * LANES, (direction + 1) * LANES)
        dt = _softplus(dtraw[:, cols] + dtb_ref[direction:direction + 1, :])
        a = dt * (-jnp.exp(alog_ref[direction:direction + 1, :]))
        a_cum = _dot(tri[direction], jnp.concatenate(_split_bf16(a, SPLIT_CUMSUM), axis=0))
        return dt, a_cum

    def expand(*vs):
        lhs = jnp.concatenate([jnp.concatenate(_split_bf16(v, SPLIT_EXPAND), axis=1) for v in vs], axis=0)
        full = _dot(lhs, e_ref[...])
        return [full[i * q:(i + 1) * q, :] for i in range(len(vs))]

    def process(x, bc16, dtraw, out_rows):
        want_y = out_rows is not None
        bm16 = bc16[:, :SSD_GROUPS * gn]
        cm16 = bc16[:, SSD_GROUPS * gn:]

        dt, a_cum = decay_terms(dtraw, scan_dir)
        edge = a_cum[0:1, :] if backward else a_cum[q - 1:q, :]
        ea_full, dtw_full = expand(jnp.exp(a_cum), dt * jnp.exp(edge - a_cum))
        chunk_decay = ea_full[0:1, :] if backward else ea_full[q - 1:q, :]
        xw = (x * dtw_full).astype(BF16)
        bts = [bm16[:, g * gn:(g + 1) * gn].astype(F32).T.astype(BF16) for g in range(SSD_GROUPS)]

        y_part = None
        if want_y and not backward:
            x16 = x.astype(BF16)
            dirs = [(d_, ac * LOG2_E) for d_, ac in ((dt, a_cum), decay_terms(dtraw, 1))]
            tr = [(d_.T, ac.T) for d_, ac in dirs]
            ys = []
            for g in range(SSD_GROUPS):
                cb = _dot_nt(cm16[:, g * gn:(g + 1) * gn], bm16[:, g * gn:(g + 1) * gn])
                for pair in range(heads_per_group // 2):
                    ms = []
                    for hh in range(2):
                        h = g * heads_per_group + pair * 2 + hh
                        w = None
                        for direction in range(2):
                            ac, (dt_t, ac_t) = dirs[direction][1], tr[direction]
                            seg = ac[:, h:h + 1] - ac_t[h:h + 1, :]
                            term = jnp.exp2(jnp.where(contributes[direction], seg, -jnp.inf)) * dt_t[h:h + 1, :]
                            w = term if w is None else w + term
                        ms.append((cb * w).astype(BF16))
                    m_pair = jnp.concatenate(ms, axis=1)
                    col0 = (g * heads_per_group + pair * 2) * SSD_HEAD_DIM
                    xp = x16[:, col0:col0 + LANES]
                    zero = jnp.zeros_like(xp)
                    x_bd = jnp.concatenate([jnp.where(low_half, xp, zero),
                                            jnp.where(low_half, zero, xp)], axis=0)
                    ys.append(_dot(m_pair, x_bd))
            y_part = jnp.concatenate(ys, axis=1) + x * dsk_ref[...]

        def finish():
            if want_y:
                y_off = jnp.concatenate(
                    [_dot(cm16[:, g * gn:(g + 1) * gn], state_ref[:, g * gcols:(g + 1) * gcols].astype(BF16))
                     for g in range(SSD_GROUPS)], axis=1) * ea_full
                y_ref[out_rows, :] = y_off if y_part is None else y_part + y_off
            for g in range(SSD_GROUPS):
                sl = slice(g * gcols, (g + 1) * gcols)
                state_ref[:, sl] = state_ref[:, sl] * chunk_decay[:, sl] + _dot(bts[g], xw[:, sl])

        return finish

    def run_block(x_ref_, bc_ref_, dt_ref_, r0, is_latent):
        order = range(nch - 1, -1, -1) if backward else range(nch)
        finishers = []
        for k in order:
            rows = slice(r0 + k * q, r0 + (k + 1) * q)
            out_rows = slice(k * q, (k + 1) * q) if is_latent else None
            finishers.append(process(x_ref_[rows, :], bc_ref_[rows, :], dt_ref_[rows, :], out_rows))
        for fin in finishers:
            fin()

    @pl.when(c == 0)
    def _():
        state_ref[...] = jnp.zeros_like(state_ref)

    blk = nch * q
    for step in range(n_ctx_steps):
        cs = (n_ctx_steps - 1 - step) if backward else step

        @pl.when(c == step)
        def _(cs=cs):
            run_block(xsc_ref, bcc_ref, dtc_ref, cs * blk, False)

    @pl.when(c >= n_ctx_steps)
    def _():
        run_block(xs_ref, bc_ref, dt_ref, 0, True)


def ssd_pass(backward, xs, bc, dt, xs_c, bc_c, dt_c, dt_bias_rows, a_log_rows, expand2,
             d_skip_row=None):
    bsz, seq, inner = xs.shape
    n_ctx = xs_c.shape[1]
    blk = SSD_CHUNK * SSD_CHUNKS_PER_STEP
    assert n_ctx % blk == 0 and seq % blk == 0
    ncs, nls = n_ctx // blk, seq // blk

    def lat(c):
        ls = jnp.maximum(c - ncs, 0)
        return (nls - 1 - ls) if backward else ls

    block = lambda width: pl.BlockSpec((None, blk, width), lambda b, c: (b, lat(c), 0))
    whole = lambda width: pl.BlockSpec((None, n_ctx, width), lambda b, c: (b, 0, 0))
    const = lambda shape: pl.BlockSpec(shape, lambda b, c: (0,) * len(shape))
    in_specs = [
        block(inner), block(SSD_BC), block(DT_COLS),
        whole(inner), whole(SSD_BC), whole(DT_COLS),
        const((2, LANES)), const((2, LANES)), const(expand2.shape),
    ]
    args = [xs, bc, dt, xs_c, bc_c, dt_c, dt_bias_rows, a_log_rows, expand2]
    if not backward:
        in_specs += [const((1, inner))]
        args += [d_skip_row]
    return pl.pallas_call(
        functools.partial(_ssd_kernel, backward, ncs, nls),
        grid=(bsz, ncs + nls),
        in_specs=in_specs,
        out_specs=block(inner),
        out_shape=jax.ShapeDtypeStruct((bsz, seq, inner), F32),
        scratch_shapes=[pltpu.VMEM((SSD_STATE, inner), F32)],
        compiler_params=_cparams(("arbitrary", "arbitrary")),
        name="ssd_bwd" if backward else "ssd_fwd",
    )(*args)


NA_GROUP_HEADS = 4
NA_ROWS_PER_STEP = 4


def _natten_kernel(grid_rows, q_ref, k_ref, v_ref, kc_ref, vc_ref, bias_ref, o_ref):
    w = GRID_W
    nwin = NA_WIN_R * w
    hg = NA_GROUP_HEADS
    gd = hg * NA_HEAD_DIM
    ngroups = NA_HEADS // hg
    nrows = NA_ROWS_PER_STEP
    lane_head = lax.broadcasted_iota(jnp.int32, (w, gd), 1) // NA_HEAD_DIM
    low_half = lax.broadcasted_iota(jnp.int32, (w, LANES), 1) < NA_HEAD_DIM
    r_base = pl.program_id(1) * nrows
    rs = [jnp.clip(r_base + i - NA_WIN_R // 2, 0, grid_rows - NA_WIN_R) for i in range(nrows)]
    k0 = [pl.multiple_of(rs[i] * w, w) for i in range(nrows)]
    dr0 = [rs[i] - (r_base + i) + NA_WIN_R - 1 for i in range(nrows)]

    def scores(g):
        s_win = [[] for _ in range(nrows)]
        s_ctx = [[] for _ in range(nrows)]
        for pp in range(hg // 2):
            pair = g * (hg // 2) + pp
            wqs = []
            for i in range(nrows):
                qp = q_ref[pair, i * w:(i + 1) * w, :]
                zero = jnp.zeros_like(qp)
                wqs.append(jnp.concatenate([jnp.where(low_half, qp, zero), jnp.where(low_half, zero, qp)], axis=0))
            sc = _dot_nt(jnp.concatenate(wqs, axis=0), kc_ref[pair])
            for i in range(nrows):
                s_win[i].append(_dot_nt(wqs[i], k_ref[pair, pl.ds(k0[i], nwin), :]))
                s_ctx[i].append(sc[i * 2 * w:(i + 1) * 2 * w, :])
        return [(jnp.concatenate(s_win[i], axis=0), jnp.concatenate(s_ctx[i], axis=0)) for i in range(nrows)]

    def softmax(g, i, s):
        s_win, s_ctx = s
        s_win = jnp.concatenate(
            [s_win[:, t * LANES:(t + 1) * LANES] + bias_ref[g, dr0[i] + 2 * t]
             for t in range(nwin // LANES)], axis=1)
        m = jnp.maximum(jnp.max(s_win, axis=-1, keepdims=True), jnp.max(s_ctx, axis=-1, keepdims=True))
        p_win = jnp.exp2(s_win - m)
        p_ctx = jnp.exp2(s_ctx - m)
        denom = jnp.sum(p_win, axis=-1, keepdims=True) + jnp.sum(p_ctx, axis=-1, keepdims=True)
        return p_win.astype(BF16), p_ctx.astype(BF16), denom

    def values(g, ps):
        cols = slice(g * gd, (g + 1) * gd)
        o_ctx = _dot(jnp.concatenate([p[1] for p in ps], axis=0), vc_ref[:, cols])
        for i in range(nrows):
            p_win, _, denom = ps[i]
            o = (_dot(p_win, v_ref[pl.ds(k0[i], nwin), cols]) + o_ctx[i * hg * w:(i + 1) * hg * w, :]) / denom
            acc = o[:w, :]
            for hh in range(1, hg):
                acc = jnp.where(lane_head == hh, o[hh * w:(hh + 1) * w, :], acc)
            o_ref[i * w:(i + 1) * w, cols] = acc.astype(o_ref.dtype)

    s, p = {}, {}
    for step in range(ngroups + 2):
        if step < ngroups:
            s[step] = scores(step)
        if 0 <= step - 1 < ngroups:
            sg = s.pop(step - 1)
            p[step - 1] = [softmax(step - 1, i, sg[i]) for i in range(nrows)]
        if 0 <= step - 2 < ngroups:
            values(step - 2, p.pop(step - 2))


def natten(q, k, v, kc, vc, bias2):
    bsz, seq, d = v.shape
    n_ctx = vc.shape[1]
    grid_rows = seq // GRID_W
    npair = NA_HEADS // 2
    rows_tok = NA_ROWS_PER_STEP * GRID_W
    return pl.pallas_call(
        functools.partial(_natten_kernel, grid_rows),
        grid=(bsz, grid_rows // NA_ROWS_PER_STEP),
        in_specs=[
            pl.BlockSpec((None, npair, rows_tok, LANES), lambda b, r: (b, 0, r, 0)),
            pl.BlockSpec((None, npair, seq, LANES), lambda b, r: (b, 0, 0, 0)),
            pl.BlockSpec((None, seq, d), lambda b, r: (b, 0, 0)),
            _resident((None, npair, n_ctx, LANES), lambda b, r: (b, 0, 0, 0)),
            _resident((None, n_ctx, d), lambda b, r: (b, 0, 0)),
            _resident(bias2.shape, lambda b, r: (0, 0, 0, 0)),
        ],
        out_specs=pl.BlockSpec((None, rows_tok, d), lambda b, r: (b, r, 0)),
        out_shape=jax.ShapeDtypeStruct((bsz, seq, d), BF16),
        compiler_params=_cparams(("arbitrary", "arbitrary")),
        name="natten",
    )(q, k, v, kc, vc, bias2)


def natten_bias_table(rpb):
    w = GRID_W
    hg = NA_GROUP_HEADS
    qc = np.arange(w)[:, None]
    kc = np.arange(w)[None, :]
    win_start = np.clip(qc - NA_WIN_C // 2, 0, w - NA_WIN_C)
    col_ok = (kc >= win_start) & (kc < win_start + NA_WIN_C)
    dc_idx = np.clip(kc - qc, -(NA_WIN_C - 1), NA_WIN_C - 1) + NA_WIN_C - 1
    ndc = 2 * NA_WIN_C - 1
    onehot = (np.arange(ndc)[:, None, None] == dc_idx[None]).astype(np.float32)
    ndr = 2 * NA_WIN_R - 2
    nr = 2 * NA_WIN_R - 1
    t = jnp.einsum("hrc,cqk->hrqk", rpb.astype(F32), jnp.asarray(onehot), precision=lax.Precision.HIGHEST)
    t = jnp.where(col_ok[None, None], t * LOG2_E, MASK_VALUE)

    def pair_rows_kernel(t_ref, o_ref):
        for dr in range(ndr):
            for hh in range(hg):
                o_ref[dr, hh * w:(hh + 1) * w, :] = jnp.concatenate([t_ref[hh, dr], t_ref[hh, dr + 1]], axis=1)

    return pl.pallas_call(
        pair_rows_kernel,
        grid=(NA_HEADS // hg,),
        in_specs=[pl.BlockSpec((hg, nr, w, w), lambda g: (g, 0, 0, 0))],
        out_specs=pl.BlockSpec((None, ndr, hg * w, 2 * w), lambda g: (g, 0, 0, 0)),
        out_shape=jax.ShapeDtypeStruct((NA_HEADS // hg, ndr, hg * w, 2 * w), F32),
        compiler_params=_cparams(("arbitrary",)),
        name="natten_bias_pairs",
    )(t)


def _mlp_kernel(ff_chunk, out_proj, final_norm, *refs):
    refs = list(refs)
    x = refs.pop(0)[...]
    if out_proj:
        ysf_ref, ysb_ref, z_ref, gnw_ref, yb_ref, wo_ref, gate_a_ref = refs[:7]
        del refs[:7]
        ka = ysf_ref.shape[-1]
        ys = (ysf_ref[...] + ysb_ref[...]) * _silu(z_ref[...])
        ys = (ys * lax.rsqrt(jnp.mean(ys * ys, axis=-1, keepdims=True) + RMS_EPS)) * gnw_ref[...]
        x = x + gate_a_ref[...] * (_dot(ys.astype(BF16), wo_ref[:ka, :]) + _dot(yb_ref[...], wo_ref[ka:, :]))
    nw_ref, sh_ref, sc_ref, gate_ref, w1_ref, w2_ref = refs[:6]
    fnw_ref = refs[6] if final_norm else None
    o_ref = refs[-1]
    h = _rms_mod(x, nw_ref[...], sh_ref[...], sc_ref[...]).astype(BF16)
    dff = w1_ref.shape[1]
    acc = None
    for c0 in range(0, dff, ff_chunk):
        a = jnp.maximum(_dot(h, w1_ref[:, c0:c0 + ff_chunk]), 0.0)
        part = _dot((a * a).astype(BF16), w2_ref[c0:c0 + ff_chunk, :])
        acc = part if acc is None else acc + part
    y = x + gate_ref[...] * acc
    if final_norm:
        ms = jnp.mean(y * y, axis=-1, keepdims=True)
        y = (y * lax.rsqrt(ms + RMS_EPS)) * fnw_ref[...]
    o_ref[...] = y


def mlp(x, nw, shift, scale, gate, w1, w2, layer, tm, ff_chunk, final_nw=None, out_proj=None):
    bsz, seq, d = x.shape
    tok = lambda width: pl.BlockSpec((None, tm, width), lambda b, i: (b, i, 0))
    vec = pl.BlockSpec((None, 1, d), lambda b, i: (b, 0, 0))
    row = pl.BlockSpec((1, d), lambda b, i: (0, 0))
    layer_w = lambda w: _resident((None,) + w.shape[1:], lambda b, i: (layer, 0, 0))
    in_specs = [tok(d)]
    args = [x]
    if out_proj is not None:
        ysf, ysb, z, gnw, yb, wo, gate_a = out_proj
        in_specs += [tok(ysf.shape[-1]), tok(ysb.shape[-1]), tok(z.shape[-1]),
                     pl.BlockSpec(gnw.shape, lambda b, i: (0, 0)),
                     tok(yb.shape[-1]), _resident(wo.shape, lambda b, i: (0, 0)), vec]
        args += [ysf, ysb, z, gnw, yb, wo, gate_a]
    in_specs += [row, vec, vec, vec, layer_w(w1), layer_w(w2)]
    args += [nw, shift, scale, gate, w1, w2]
    if final_nw is not None:
        in_specs.append(row)
        args.append(final_nw)
    return pl.pallas_call(
        functools.partial(_mlp_kernel, ff_chunk, out_proj is not None, final_nw is not None),
        grid=(bsz, seq // tm),
        in_specs=in_specs,
        out_specs=tok(d),
        out_shape=jax.ShapeDtypeStruct((bsz, seq, d), F32),
        compiler_params=_cparams(("arbitrary", "arbitrary")),
        name="mlp_final" if final_nw is not None else "mlp",
    )(*args)


def _shortconv_kernel(n_tiles, x_ref, xp_ref, xn_ref, nw_ref, sh_ref, sc_ref, gate_ref,
                      win_ref, cw_ref, wout_ref, o_ref):
    i = pl.program_id(1)
    tm, d = x_ref.shape
    inner = wout_ref.shape[0]
    hb = SUBLANES
    x = x_ref[...]
    x_ext = jnp.concatenate([xp_ref[...], x, xn_ref[...]], axis=0)
    h = _rms_mod(x_ext, nw_ref[...], sh_ref[...], sc_ref[...]).astype(BF16)
    gate_c = _dot(h, win_ref[:, inner:2 * inner])
    val = _dot(h, win_ref[:, 2 * inner:])
    u = gate_c * val
    rows = lax.broadcasted_iota(jnp.int32, u.shape, 0)
    outside = ((rows < hb) & (i == 0)) | ((rows >= tm + hb) & (i == n_tiles - 1))
    u = jnp.where(outside, 0.0, u)
    ext = tm + 2 * hb
    conv = (pltpu.roll(u, 1, 0) * cw_ref[0:1, :] + u * cw_ref[1:2, :]
            + pltpu.roll(u, ext - 1, 0) * cw_ref[2:3, :])[hb:hb + tm, :]
    gate_b = _dot(h[hb:hb + tm, :], win_ref[:, :inner])
    y = _dot((gate_b * conv).astype(BF16), wout_ref[...])
    o_ref[...] = x + gate_ref[...] * y


def shortconv(x, nw, shift, scale, gate, w_in, conv_w, w_out, tm):
    bsz, seq, d = x.shape
    n_tiles = seq // tm
    per_tile = tm // SUBLANES
    n_halo = seq // SUBLANES
    tok = pl.BlockSpec((None, tm, d), lambda b, i: (b, i, 0))
    vec = pl.BlockSpec((None, 1, d), lambda b, i: (b, 0, 0))
    return pl.pallas_call(
        functools.partial(_shortconv_kernel, n_tiles),
        grid=(bsz, n_tiles),
        in_specs=[
            tok,
            pl.BlockSpec((None, SUBLANES, d), lambda b, i: (b, jnp.maximum(i * per_tile - 1, 0), 0)),
            pl.BlockSpec((None, SUBLANES, d), lambda b, i: (b, jnp.minimum((i + 1) * per_tile, n_halo - 1), 0)),
            pl.BlockSpec((1, d), lambda b, i: (0, 0)), vec, vec, vec,
            _resident(w_in.shape, lambda b, i: (0, 0)),
            pl.BlockSpec(conv_w.shape, lambda b, i: (0, 0)),
            _resident(w_out.shape, lambda b, i: (0, 0)),
        ],
        out_specs=tok,
        out_shape=jax.ShapeDtypeStruct((bsz, seq, d), F32),
        compiler_params=_cparams(("arbitrary", "arbitrary")),
        name="shortconv",
    )(x, x, x, nw, shift, scale, gate, w_in, conv_w, w_out)


def _pad_lanes(row):
    return jnp.zeros((1, LANES), F32).at[0, :row.shape[0]].set(row.astype(F32))


def kernel(x, c, ctx, c_ctx, mod_w, mod_b, norm_mix_w, norm_mlp_w, mlp_w1, mlp_w2, ssdna_in_w, ssdna_conv_w,
           ssdna_conv_b, ssd_dt_bias, ssd_a_log, ssd_d, ssd_norm_w, na_rpb, ssdna_out_w, sc_in_w, sc_conv_w,
           sc_out_w, final_norm_w):
    bsz, seq, d = x.shape
    n_ctx = ctx.shape[1]
    tm = min(TOKEN_TILE, seq)
    inner = SSD_HEADS * SSD_HEAD_DIM
    gn2 = SSD_GROUPS * SSD_STATE

    mrows = -(-(bsz + 1) // SUBLANES) * SUBLANES
    cc = jnp.zeros((mrows, d), F32).at[:bsz].set(c).at[bsz].set(c_ctx)
    mod = modvec(cc, mod_w, mod_b).reshape(mod_w.shape[0], mrows, 6, d)
    vecs = lambda i: [mod[i, :bsz, j].reshape(bsz, 1, d) for j in range(6)]
    row = lambda v: v.reshape(1, -1).astype(F32)

    shift_a, scale_a, gate_a, shift_f, scale_f, gate_f = vecs(0)
    shift_c = jnp.broadcast_to(mod[0, bsz, 0].reshape(1, 1, d), (bsz, 1, d))
    scale_c = jnp.broadcast_to(mod[0, bsz, 1].reshape(1, 1, d), (bsz, 1, d))

    o_dt, o_k = inner + gn2, inner + gn2 + 2 * SSD_HEADS
    o_v = o_k + d
    o_c = o_v + d
    o_z, o_q = o_c + gn2, o_c + gn2 + inner
    segments = [(0, o_dt, 1.0), (o_c, gn2, 1.0), (o_z, inner, 1.0),
                (o_q, d, NA_HEAD_DIM ** -0.5 * LOG2_E),
                (o_k, d, 1.0), (o_v, d, 1.0), (o_dt, SSD_HEADS, 1.0), (o_dt + SSD_HEADS, SSD_HEADS, 1.0)]
    w0 = reorder_columns(ssdna_in_w, 0, segments)
    nw0 = row(norm_mix_w[0])
    conv_w = ssdna_conv_w[0]
    conv_b = row(ssdna_conv_b[0])
    xs, bc, z, q_l, k_l, v_l, dt_l = inproj0(x, nw0, shift_a, scale_a, w0, conv_w, conv_b, tm)
    xs_c, bc_c, k_c, v_c, dt_c = inproj0(ctx, nw0, shift_c, scale_c, w0, conv_w, conv_b, n_ctx, latent=False)

    d_skip_row = row(jnp.repeat(ssd_d[0], SSD_HEAD_DIM))
    expand = (np.arange(LANES)[:, None] == (np.arange(inner)[None, :] // SSD_HEAD_DIM)).astype(np.float32)
    expand2 = jnp.asarray(np.concatenate([expand] * SPLIT_EXPAND, axis=0), BF16)
    pad2 = lambda p: jnp.concatenate([_pad_lanes(p[0]), _pad_lanes(p[1])], axis=0)
    common = (xs, bc, dt_l, xs_c, bc_c, dt_c, pad2(ssd_dt_bias[0]), pad2(ssd_a_log[0]), expand2)
    y_fwd = ssd_pass(False, *common, d_skip_row=d_skip_row)
    y_bwd = ssd_pass(True, *common)

    y_na = natten(q_l, k_l, v_l, k_c, v_c, natten_bias_table(na_rpb[0]))
    w1_all, w2_all = mlp_w1.astype(BF16), mlp_w2.astype(BF16)
    x = mlp(x, row(norm_mlp_w[0]), shift_f, scale_f, gate_f, w1_all, w2_all, 0, tm, FF_CHUNK,
            out_proj=(y_fwd, y_bwd, z, row(ssd_norm_w[0]), y_na, ssdna_out_w[0].astype(BF16), gate_a))

    shift_a, scale_a, gate_a, shift_f, scale_f, gate_f = vecs(1)
    x = shortconv(x, row(norm_mix_w[1]), shift_a, scale_a, gate_a, sc_in_w[0].astype(BF16), sc_conv_w[0],
                  sc_out_w[0].astype(BF16), tm)
    x = mlp(x, row(norm_mlp_w[1]), shift_f, scale_f, gate_f, w1_all, w2_all, 1, tm, FF_CHUNK,
            final_nw=row(final_norm_w))
    return x
```
